```python
import math
import jax
import jax.numpy as jnp
from jax import lax
import numpy as np

D_MODEL = 1024
BATCH = 16
SEQ = 256
DEPTH = 2
DEC_BATCH = 4
DEC_SEQ = 1024
PAST_LEN = 256

GRID_W = 64
D_HY = 1024
HY_ORDER = 2
HY_BANDS = 16
HY_EMB = 1 + 2 * HY_BANDS
HY_FF = 64
HY_SHIFT = 0.05
HY_MIN_DECAY = math.log(1e-2) / -1.5
HY_MAX_DECAY = math.log(1e-2) / -0.3
D_ML = 1024
ML_HEADS = 4
ML_DH = D_ML // ML_HEADS
ML_CHUNK = 128
SHORT_CONV = 3
SPLIT_SIZES = (3 * D_HY, D_HY, 2 * D_ML, D_ML, D_ML, D_ML, 4 * ML_HEADS, D_MODEL, D_MODEL)
N_IN = sum(SPLIT_SIZES)
SPLIT_IDX = tuple(int(i) for i in np.cumsum(SPLIT_SIZES)[:-1])
EPS = 1e-6
NEG = -1e30
F32 = jnp.float32

kernel_name = 'hyena_mlstm_prefix_diffusion_step'


def _rmsnorm(x, g):
    xf = x.astype(F32)
    y = xf * lax.rsqrt(jnp.mean(xf * xf, axis=-1, keepdims=True) + EPS)
    return (y * g.astype(F32)).astype(x.dtype)


def _short_conv(x, w, b):
    L = x.shape[1]
    pad = SHORT_CONV // 2
    xp = jnp.pad(x, ((0, 0), (pad, pad), (0, 0)))
    y = b
    for j in range(SHORT_CONV):
        y = y + xp[:, j:j + L] * w[j]
    return y


def _hyena_filters(L, w1, b1, w2, b2, w3, b3, freq, decay):
    t = jnp.arange(L, dtype=F32) / L
    bands = jnp.arange(1, HY_BANDS + 1, dtype=F32)
    ang = 2.0 * math.pi * t[:, None] * bands[None, :]
    feats = jnp.concatenate([t[:, None], jnp.cos(ang), jnp.sin(ang)], axis=-1)
    f = freq.astype(F32)
    hdn = jnp.sin(f[0] * (feats @ w1.astype(F32) + b1.astype(F32)))
    hdn = jnp.sin(f[1] * (hdn @ w2.astype(F32) + b2.astype(F32)))
    h = (hdn @ w3.astype(F32) + b3.astype(F32)).reshape(L, HY_ORDER, 2, D_HY)
    window = jnp.exp(-t[:, None, None, None] * jnp.abs(decay.astype(F32))[None]) + HY_SHIFT
    h = h * window
    fwd = h[:, :, 0]
    bwd = h[1:, :, 1][::-1]
    l1 = jnp.sum(jnp.abs(fwd), axis=0) + jnp.sum(jnp.abs(bwd), axis=0)
    kern = jnp.concatenate([fwd, jnp.zeros((1, HY_ORDER, D_HY), F32), bwd], axis=0) / l1
    return jnp.fft.rfft(kern, axis=0)


def _hyena_mixer(u3, conv_w, conv_b, filt, bias):
    L = u3.shape[1]
    u3 = _short_conv(u3, conv_w, conv_b).astype(F32)
    x1, x2, v = jnp.split(u3, 3, axis=-1)
    gates = (x1, x2)
    bias = bias.astype(F32)
    z = v
    for o in range(HY_ORDER):
        zf = jnp.fft.rfft(z, n=2 * L, axis=1)
        y = jnp.fft.irfft(zf * filt[None, :, o], n=2 * L, axis=1)[:, :L]
        z = gates[o] * (y + bias[o] * z)
    return z


def _mlstm_chunkwise(q, k, v, logi, logf, C0, n0, m0):
    B, L, H, DH = q.shape
    NC = L // ML_CHUNK
    T = ML_CHUNK
    def to_chunks(a):
        a = a.reshape((B, NC, T) + a.shape[2:])
        return jnp.moveaxis(jnp.moveaxis(a, 1, 0), 2, -1 if a.ndim == 4 else 3)
    qc = q.reshape(B, NC, T, H, DH).transpose(1, 0, 3, 2, 4)
    kc = k.reshape(B, NC, T, H, DH).transpose(1, 0, 3, 2, 4)
    vc = v.reshape(B, NC, T, H, DH).transpose(1, 0, 3, 2, 4)
    ic = logi.reshape(B, NC, T, H).transpose(1, 0, 3, 2)
    fc = logf.reshape(B, NC, T, H).transpose(1, 0, 3, 2)
    tril = jnp.tril(jnp.ones((T, T), dtype=bool))

    def step(carry, xs):
        C, n, m = carry
        qb, kb, vb, ib, fb = xs
        b = jnp.cumsum(fb, axis=-1)
        dlog = jnp.where(tril, b[..., :, None] - b[..., None, :] + ib[..., None, :], NEG)
        inter = b + m[..., None]
        mt = jnp.maximum(inter, jnp.max(dlog, axis=-1))
        s = jnp.einsum('bhtd,bhsd->bhts', qb, kb) * jnp.exp(dlog - mt[..., None])
        iw = jnp.exp(inter - mt)
        num = jnp.einsum('bhts,bhsv->bhtv', s, vb) + iw[..., None] * jnp.einsum('bhtk,bhkv->bhtv', qb, C)
        den = jnp.sum(s, axis=-1) + iw * jnp.einsum('bhtk,bhk->bht', qb, n)
        h = num / jnp.maximum(jnp.abs(den), jnp.exp(-mt))[..., None]
        bl = b[..., -1]
        wlog = bl[..., None] - b + ib
        mn = jnp.maximum(bl + m, jnp.max(wlog, axis=-1))
        ws = jnp.exp(wlog - mn[..., None])
        dec = jnp.exp(bl + m - mn)
        Cn = dec[..., None, None] * C + jnp.einsum('bhs,bhsk,bhsv->bhkv', ws, kb, vb)
        nn = dec[..., None] * n + jnp.einsum('bhs,bhsk->bhk', ws, kb)
        return (Cn, nn, mn), h

    (C, n, m), hs = lax.scan(step, (C0, n0, m0), (qc, kc, vc, ic, fc))
    h = hs.transpose(1, 0, 3, 2, 4).reshape(B, L, H, DH)
    return h, C, n, m


def _mlstm_mixer(qk, v, gpre, conv_w, conv_b, if_b, states):
    B, L, _ = v.shape
    qk = jax.nn.silu(_short_conv(qk, conv_w, conv_b)).astype(F32)
    q, k = jnp.split(qk, 2, axis=-1)
    q = q.reshape(B, L, ML_HEADS, ML_DH)
    k = k.reshape(B, L, ML_HEADS, ML_DH) * (ML_DH ** -0.5)
    vf = v.astype(F32).reshape(B, L, ML_HEADS, ML_DH)
    g = gpre.astype(F32).reshape(B, L, 2, 2, ML_HEADS) + if_b.astype(F32)
    h_sum = None
    finals = []
    for d in range(2):
        logi = g[:, :, d, 0]
        logf = jax.nn.log_sigmoid(g[:, :, d, 1])
        seq = (q, k, vf, logi, logf)
        if d == 1:
            seq = tuple(jnp.flip(a, axis=1) for a in seq)
        C0, n0, m0 = (s.astype(F32) for s in states[d])
        h, C, n, m = _mlstm_chunkwise(seq[0], seq[1], seq[2], seq[3], seq[4], C0, n0, m0)
        if d == 1:
            h = jnp.flip(h, axis=1)
        h_sum = h if h_sum is None else h_sum + h
        finals.append((C, n, m))
    return h_sum, finals


def _layer(x, mod, states, norm_g, w_in, hy_conv_w, hy_conv_b, hy_w1, hy_b1, hy_w2, hy_b2,
           hy_w3, hy_b3, hy_freq, hy_decay, hy_bias, ml_conv_w, ml_conv_b, ml_if_b,
           ml_norm_g, w_pa, w_pb, w_out):
    B, L, _ = x.shape
    shift, scale, gate = jnp.split(mod, 3, axis=-1)
    h = _rmsnorm(x, norm_g) * (1.0 + scale[:, None]) + shift[:, None]
    u = h @ w_in
    hy_u, hy_z, ml_qk, ml_v, ml_o, ml_z, ml_g, g_a, g_b = jnp.split(u, SPLIT_IDX, axis=-1)
    filt = _hyena_filters(L, hy_w1, hy_b1, hy_w2, hy_b2, hy_w3, hy_b3, hy_freq, hy_decay)
    ya = _hyena_mixer(hy_u, hy_conv_w, hy_conv_b, filt, hy_bias).astype(x.dtype) * jax.nn.silu(hy_z)
    hm, finals = _mlstm_mixer(ml_qk, ml_v, ml_g, ml_conv_w, ml_conv_b, ml_if_b, states)
    hm = hm * lax.rsqrt(jnp.mean(hm * hm, axis=-1, keepdims=True) + EPS)
    hm = (hm.reshape(B, L, D_ML) * ml_norm_g.astype(F32)).astype(x.dtype)
    yb = hm * jax.nn.sigmoid(ml_o) * jax.nn.silu(ml_z)
    merged = jax.nn.sigmoid(g_a) * (ya @ w_pa) + jax.nn.sigmoid(g_b) * (yb @ w_pb)
    return x + gate[:, None] * (merged @ w_out), finals


def setup_inputs(seed: int = 0) -> dict:
    key = jax.random.key(seed)
    ks = jax.random.split(key, 32)
    def nrm(k, shape, s):
        return jax.random.normal(k, shape, F32) * s
    x_prompt = nrm(ks[0], (BATCH, SEQ, D_MODEL), 1.0)
    x_sample = nrm(ks[1], (DEC_BATCH, DEC_SEQ, D_MODEL), 1.0)
    state_C = nrm(ks[2], (DEC_BATCH, DEPTH, 2, ML_HEADS, ML_DH, ML_DH), 0.5)
    state_n = nrm(ks[3], (DEC_BATCH, DEPTH, 2, ML_HEADS, ML_DH), 0.5)
    state_m = nrm(ks[4], (DEC_BATCH, DEPTH, 2, ML_HEADS), 0.5)
    c = nrm(ks[5], (DEC_BATCH, D_MODEL), 1.0)
    c_ctx = nrm(ks[6], (D_MODEL,), 1.0)
    norm_g = 1.0 + nrm(ks[7], (DEPTH, D_MODEL), 0.02)
    w_ada = nrm(ks[8], (DEPTH, D_MODEL, 3 * D_MODEL), 0.5 * D_MODEL ** -0.5)
    b_ada = nrm(ks[9], (DEPTH, 3 * D_MODEL), 0.02)
    w_in = nrm(ks[10], (DEPTH, D_MODEL, N_IN), D_MODEL ** -0.5)
    hy_conv_w = nrm(ks[11], (DEPTH, SHORT_CONV, 3 * D_HY), SHORT_CONV ** -0.5)
    hy_conv_b = nrm(ks[12], (DEPTH, 3 * D_HY), 0.02)
    hy_w1 = nrm(ks[13], (DEPTH, HY_EMB, HY_FF), HY_EMB ** -0.5)
    hy_b1 = nrm(ks[14], (DEPTH, HY_FF), 0.1)
    hy_w2 = nrm(ks[15], (DEPTH, HY_FF, HY_FF), HY_FF ** -0.5)
    hy_b2 = nrm(ks[16], (DEPTH, HY_FF), 0.1)
    hy_w3 = nrm(ks[17], (DEPTH, HY_FF, HY_ORDER * 2 * D_HY), HY_FF ** -0.5)
    hy_b3 = nrm(ks[18], (DEPTH, HY_ORDER * 2 * D_HY), 0.1)
    hy_freq = 1.0 + nrm(ks[19], (DEPTH, 2, HY_FF), 0.1)
    decay_base = jnp.linspace(HY_MIN_DECAY, HY_MAX_DECAY, D_HY, dtype=F32)
    hy_decay = decay_base + nrm(ks[20], (DEPTH, HY_ORDER, 2, D_HY), 0.1)
    hy_bias = nrm(ks[21], (DEPTH, HY_ORDER, D_HY), 1.0)
    ml_conv_w = nrm(ks[22], (DEPTH, SHORT_CONV, 2 * D_ML), SHORT_CONV ** -0.5)
    ml_conv_b = nrm(ks[23], (DEPTH, 2 * D_ML), 0.02)
    i_b = nrm(ks[24], (DEPTH, 2, 1, ML_HEADS), 0.1)
    f_b = jnp.linspace(3.0, 6.0, ML_HEADS, dtype=F32) + nrm(ks[25], (DEPTH, 2, 1, ML_HEADS), 0.1)
    ml_if_b = jnp.concatenate([i_b, f_b], axis=2)
    ml_norm_g = 1.0 + nrm(ks[26], (DEPTH, D_ML), 0.02)
    w_pa = nrm(ks[27], (DEPTH, D_HY, D_MODEL), D_HY ** -0.5)
    w_pb = nrm(ks[28], (DEPTH, D_ML, D_MODEL), D_ML ** -0.5)
    w_out = nrm(ks[29], (DEPTH, D_MODEL, D_MODEL), D_MODEL ** -0.5)
    final_g = 1.0 + nrm(ks[30], (D_MODEL,), 0.02)
    return {'x_prompt': x_prompt, 'x_sample': x_sample, 'state_C': state_C, 'state_n': state_n,
            'state_m': state_m, 'c': c, 'c_ctx': c_ctx, 'norm_g': norm_g, 'w_ada': w_ada,
            'b_ada': b_ada, 'w_in': w_in, 'hy_conv_w': hy_conv_w, 'hy_conv_b': hy_conv_b,
            'hy_w1': hy_w1, 'hy_b1': hy_b1, 'hy_w2': hy_w2, 'hy_b2': hy_b2, 'hy_w3': hy_w3,
            'hy_b3': hy_b3, 'hy_freq': hy_freq, 'hy_decay': hy_decay, 'hy_bias': hy_bias,
            'ml_conv_w': ml_conv_w, 'ml_conv_b': ml_conv_b, 'ml_if_b': ml_if_b,
            'ml_norm_g': ml_norm_g, 'w_pa': w_pa, 'w_pb': w_pb, 'w_out': w_out,
            'final_g': final_g}


def reference(x_prompt, x_sample, state_C, state_n, state_m, c, c_ctx, norm_g, w_ada, b_ada,
              w_in, hy_conv_w, hy_conv_b, hy_w1, hy_b1, hy_w2, hy_b2, hy_w3, hy_b3, hy_freq,
              hy_decay, hy_bias, ml_conv_w, ml_conv_b, ml_if_b, ml_norm_g, w_pa, w_pb, w_out,
              final_g):
    B = x_prompt.shape[0]
    zero_state = (jnp.zeros((B, ML_HEADS, ML_DH, ML_DH), F32),
                  jnp.zeros((B, ML_HEADS, ML_DH), F32),
                  jnp.zeros((B, ML_HEADS), F32))
    ctx_states = (zero_state, zero_state)
    xp = x_prompt
    xs = x_sample
    new_C, new_n, new_m = [], [], []
    for l in range(DEPTH):
        weights = (norm_g[l], w_in[l], hy_conv_w[l], hy_conv_b[l], hy_w1[l], hy_b1[l], hy_w2[l],
                   hy_b2[l], hy_w3[l], hy_b3[l], hy_freq[l], hy_decay[l], hy_bias[l],
                   ml_conv_w[l], ml_conv_b[l], ml_if_b[l], ml_norm_g[l], w_pa[l], w_pb[l], w_out[l])
        mod_ctx = (jax.nn.silu(c_ctx) @ w_ada[l] + b_ada[l])[None]
        xp, fin = _layer(xp, mod_ctx, ctx_states, *weights)
        new_C.append(jnp.stack([fin[0][0], fin[1][0]], axis=1))
        new_n.append(jnp.stack([fin[0][1], fin[1][1]], axis=1))
        new_m.append(jnp.stack([fin[0][2], fin[1][2]], axis=1))
        mod_lat = jax.nn.silu(c) @ w_ada[l] + b_ada[l]
        cached = ((state_C[:, l, 0], state_n[:, l, 0], state_m[:, l, 0]),
                  (state_C[:, l, 1], state_n[:, l, 1], state_m[:, l, 1]))
        xs, _ = _layer(xs, mod_lat, cached, *weights)
    y_prompt = _rmsnorm(xp, final_g)
    y_sample = _rmsnorm(xs, final_g)
    new_state_C = jnp.stack(new_C, axis=1)
    new_state_n = jnp.stack(new_n, axis=1)
    new_state_m = jnp.stack(new_m, axis=1)
    return (y_prompt, y_sample, new_state_C, new_state_n, new_state_m)
```

```python
import functools
import math

import numpy as np
import jax
import jax.numpy as jnp
from jax import lax
from jax.experimental import pallas as pl
from jax.experimental.pallas import tpu as pltpu

F32 = jnp.float32
BF16 = jnp.bfloat16
HIGHEST = lax.Precision.HIGHEST

HY_ORDER = 2
HY_BANDS = 16
HY_SHIFT = 0.05
ML_CHUNK = 128
EPS = 1e-6
NEG = -1e30

LANES = 128
VMEM_LIMIT = 56 * 1024 * 1024
HY_CB = 256
OUT_TM = 512


def _dot(a, b):
    return jnp.dot(a, b, preferred_element_type=F32)


def _dot_hi(a, b):
    return jnp.dot(a, b, preferred_element_type=F32, precision=HIGHEST)


def _sigmoid(x):
    return 1.0 / (1.0 + jnp.exp(-x))


def _silu(x):
    return x * _sigmoid(x)


def _log_sigmoid(x):
    return jnp.minimum(x, 0.0) - jnp.log(1.0 + jnp.exp(-jnp.abs(x)))


def _params(*sem):
    return pltpu.CompilerParams(dimension_semantics=sem, vmem_limit_bytes=VMEM_LIMIT)


@functools.lru_cache(maxsize=None)
def _dft_tables(L):
    N = 2 * L
    k = np.arange(L, dtype=np.int64)[:, None]
    n = np.arange(L, dtype=np.int64)[None, :]
    ang = 2.0 * np.pi * ((k * n) % N).astype(np.float64) / N
    c = np.cos(ang)
    s = -np.sin(ang)
    s[0, :] = 1.0 - 2.0 * (np.arange(L) % 2)
    f = np.concatenate([c, s], axis=0).astype(np.float32)
    f_hi = f.astype(BF16)
    f_lo = (f - f_hi.astype(np.float32)).astype(BF16)
    return f_hi, f_lo


def _mod_kernel(c_ref, w_ref, b_ref, o_ref):
    o_ref[...] = _dot_hi(_silu(c_ref[...]), w_ref[...]) + b_ref[...]


def _mods(cvecs, w_ada, b_ada):
    depth, d, d3 = w_ada.shape
    r = cvecs.shape[0]
    tn = 512
    return pl.pallas_call(
        _mod_kernel,
        grid=(depth, d3 // tn),
        in_specs=[pl.BlockSpec((r, d), lambda l, j: (0, 0)),
                  pl.BlockSpec((None, d, tn), lambda l, j: (l, 0, j)),
                  pl.BlockSpec((None, 1, tn), lambda l, j: (l, 0, j))],
        out_specs=pl.BlockSpec((None, r, tn), lambda l, j: (l, 0, j)),
        out_shape=jax.ShapeDtypeStruct((depth, r, d3), F32),
        compiler_params=_params("arbitrary", "arbitrary"),
        name="adaln_mod",
    )(cvecs, w_ada, b_ada.reshape(depth, 1, d3))


def _norm_mod_kernel(x_ref, g_ref, mod_ref, o_ref):
    x = x_ref[...]
    y = x * lax.rsqrt(jnp.mean(x * x, axis=-1, keepdims=True) + EPS) * g_ref[...]
    o_ref[...] = (y * (1.0 + mod_ref[1:2, :]) + mod_ref[0:1, :]).astype(o_ref.dtype)


def _norm_mod(x, g, mod):
    bm, t, d = x.shape
    tm = min(t, 1024)
    return pl.pallas_call(
        _norm_mod_kernel,
        grid=(bm, t // tm),
        in_specs=[pl.BlockSpec((None, tm, d), lambda b, i: (b, i, 0)),
                  pl.BlockSpec((1, d), lambda b, i: (0, 0)),
                  pl.BlockSpec((None, 3, d), lambda b, i: (b, 0, 0))],
        out_specs=pl.BlockSpec((None, tm, d), lambda b, i: (b, i, 0)),
        out_shape=jax.ShapeDtypeStruct((bm, t, d), BF16),
        compiler_params=_params("arbitrary", "arbitrary"),
        name="norm_mod",
    )(x, g.reshape(1, d), mod)


def _final_norm_kernel(x_ref, g_ref, o_ref):
    x = x_ref[...]
    o_ref[...] = x * lax.rsqrt(jnp.mean(x * x, axis=-1, keepdims=True) + EPS) * g_ref[...]


def _final_norm(x, g):
    n, d = x.shape
    tm = 1024
    return pl.pallas_call(
        _final_norm_kernel,
        grid=(n // tm,),
        in_specs=[pl.BlockSpec((tm, d), lambda i: (i, 0)),
                  pl.BlockSpec((1, d), lambda i: (0, 0))],
        out_specs=pl.BlockSpec((tm, d), lambda i: (i, 0)),
        out_shape=jax.ShapeDtypeStruct((n, d), F32),
        compiler_params=_params("arbitrary"),
        name="final_norm",
    )(x, g.reshape(1, d))


def _dot3(f_hi, f_lo, a):
    a_hi = a.astype(BF16)
    a_lo = (a - a_hi.astype(F32)).astype(BF16)
    return _dot(f_hi, a_hi) + (_dot(f_lo, a_hi) + _dot(f_hi, a_lo))


def _filter_kernel(w1_ref, b1_ref, w2_ref, b2_ref, freq_ref, w3_00, w3_01, w3_10, w3_11,
                   b3_ref, decay_ref, fhi_ref, flo_ref, spec_ref, hdn_ref, *, L):
    cb = spec_ref.shape[-1]

    @pl.when(pl.program_id(0) == 0)
    def _():
        row = lax.broadcasted_iota(jnp.int32, (L, LANES), 0).astype(F32)
        lane = lax.broadcasted_iota(jnp.int32, (L, LANES), 1)
        t = row / L
        band = jnp.where(lane <= HY_BANDS, lane, lane - HY_BANDS).astype(F32)
        ang = 2.0 * math.pi * t * band
        feats = jnp.where(lane == 0, t,
                          jnp.where(lane <= HY_BANDS, jnp.cos(ang),
                                    jnp.where(lane <= 2 * HY_BANDS, jnp.sin(ang), 0.0)))
        hdn = jnp.sin(freq_ref[0:1, :] * (_dot_hi(feats, w1_ref[...]) + b1_ref[...]))
        hdn_ref[...] = jnp.sin(freq_ref[1:2, :] * (_dot_hi(hdn, w2_ref[...]) + b2_ref[...]))

    hdn = hdn_ref[...]
    row_i = lax.broadcasted_iota(jnp.int32, (L, cb), 0)
    t = row_i.astype(F32) / L
    first = row_i == 0
    sign = jnp.where((row_i & 1) == 0, 1.0, -1.0)
    w3 = ((w3_00, w3_01), (w3_10, w3_11))
    for o in range(HY_ORDER):
        hs = []
        for dr in range(2):
            j = 2 * o + dr
            hv = _dot_hi(hdn, w3[o][dr][...]) + b3_ref[j:j + 1, :]
            win = jnp.exp(-t * jnp.abs(decay_ref[j:j + 1, :])) + HY_SHIFT
            hs.append(hv * win)
        hf = hs[0]
        hb = jnp.where(first, 0.0, hs[1])
        l1 = jnp.sum(jnp.abs(hf), axis=0, keepdims=True) + jnp.sum(jnp.abs(hb), axis=0, keepdims=True)
        hf = hf / l1
        hb = hb / l1
        even = hf + hb
        odd = hf - hb
        re = _dot3(fhi_ref[0:L, :], flo_ref[0:L, :], even)
        im = _dot3(fhi_ref[L:2 * L, :], flo_ref[L:2 * L, :], odd)
        nyq = jnp.sum(even * sign, axis=0, keepdims=True)
        a = re * jnp.where(first, 0.5 / L, 1.0 / L)
        spec_ref[o, 0] = a
        spec_ref[o, 1] = jnp.where(first, 0.0, im * (1.0 / L))
        spec_ref[o, 2] = jnp.where(first, nyq * (0.5 / L), a)


def _hyena_spectra(L, l, hy_w1, hy_b1, hy_w2, hy_b2, hy_w3, hy_b3, hy_freq, hy_decay):
    emb, ff = hy_w1.shape[1:]
    d_hy = hy_decay.shape[-1]
    cb = HY_CB
    ncb = d_hy // cb
    w1p = jnp.zeros((LANES, ff), F32).at[:emb].set(hy_w1[l])
    f_hi, f_lo = _dft_tables(L)
    const = lambda j: (0, 0)

    def w3_spec(o, dr):
        return pl.BlockSpec((None, ff, cb), lambda j, o=o, dr=dr: (l, 0, (2 * o + dr) * ncb + j))

    return pl.pallas_call(
        functools.partial(_filter_kernel, L=L),
        grid=(ncb,),
        in_specs=[pl.BlockSpec((LANES, ff), const),
                  pl.BlockSpec((1, ff), const),
                  pl.BlockSpec((ff, ff), const),
                  pl.BlockSpec((1, ff), const),
                  pl.BlockSpec((2, ff), const),
                  w3_spec(0, 0), w3_spec(0, 1), w3_spec(1, 0), w3_spec(1, 1),
                  pl.BlockSpec((2 * HY_ORDER, cb), lambda j: (0, j)),
                  pl.BlockSpec((2 * HY_ORDER, cb), lambda j: (0, j)),
                  pl.BlockSpec((2 * L, L), const),
                  pl.BlockSpec((2 * L, L), const)],
        out_specs=pl.BlockSpec((HY_ORDER, 3, L, cb), lambda j: (0, 0, 0, j)),
        out_shape=jax.ShapeDtypeStruct((HY_ORDER, 3, L, d_hy), F32),
        scratch_shapes=[pltpu.VMEM((L, ff), F32)],
        compiler_params=_params("arbitrary"),
        name="hyena_filter",
    )(w1p, hy_b1[l].reshape(1, ff), hy_w2[l], hy_b2[l].reshape(1, ff), hy_freq[l],
      hy_w3, hy_w3, hy_w3, hy_w3,
      hy_b3[l].reshape(2 * HY_ORDER, d_hy), hy_decay[l].reshape(2 * HY_ORDER, d_hy),
      jnp.asarray(f_hi), jnp.asarray(f_lo))


def _short_conv3(x, w_ref, b):
    n = x.shape[0]
    row = lax.broadcasted_iota(jnp.int32, x.shape, 0)
    prev = jnp.where(row == 0, 0.0, pltpu.roll(x, 1, axis=0))
    nxt = jnp.where(row == n - 1, 0.0, pltpu.roll(x, n - 1, axis=0))
    return ((b + prev * w_ref[0:1, :]) + x * w_ref[1:2, :]) + nxt * w_ref[2:3, :]


def _hyena_kernel(h_ref, wx1_ref, wx2_ref, wv_ref, wz_ref, cw_ref, cb_ref, spec_ref, hbias_ref,
                  f_ref, g_ref, o_ref, *, nb, L):
    d = h_ref.shape[-1]
    hflat = h_ref[...].reshape(nb * L, d)
    u1 = _dot(hflat, wx1_ref[...])
    u2 = _dot(hflat, wx2_ref[...])
    uv = _dot(hflat, wv_ref[...])
    uz = _dot(hflat, wz_ref[...])
    for b in range(nb):
        rows = slice(b * L, (b + 1) * L)
        gates = (_short_conv3(u1[rows], cw_ref.at[0], cb_ref[0:1, :]),
                 _short_conv3(u2[rows], cw_ref.at[1], cb_ref[1:2, :]))
        z = _short_conv3(uv[rows], cw_ref.at[2], cb_ref[2:3, :])
        for o in range(HY_ORDER):
            zf = _dot(f_ref[...], z.astype(BF16))
            zc, zs = zf[0:L], zf[L:2 * L]
            a, bm, dm = spec_ref[o, 0], spec_ref[o, 1], spec_ref[o, 2]
            yc = zc * a - zs * bm
            ys = zc * bm + zs * dm
            ycat = jnp.concatenate([yc, ys], axis=0).astype(BF16)
            y = _dot(g_ref[...], ycat)
            z = gates[o] * (y + hbias_ref[o:o + 1, :] * z)
        o_ref[b] = (z * _silu(uz[rows])).astype(o_ref.dtype)


def _hyena(h, l, w_in_bf, conv_w, conv_b, spec, hy_bias):
    bsz, L, d = h.shape
    d_hy = hy_bias.shape[-1]
    cb = HY_CB
    ncb = d_hy // cb
    nb = max(1, min(bsz, 1024 // L))
    f_hi, _ = _dft_tables(L)
    f = jnp.asarray(f_hi)
    g = jnp.asarray(np.ascontiguousarray(f_hi.T))

    def w_spec(seg):
        return pl.BlockSpec((None, d, cb), lambda j, i, seg=seg: (l, 0, seg * ncb + j))

    return pl.pallas_call(
        functools.partial(_hyena_kernel, nb=nb, L=L),
        grid=(ncb, bsz // nb),
        in_specs=[pl.BlockSpec((nb, L, d), lambda j, i: (i, 0, 0)),
                  w_spec(0), w_spec(1), w_spec(2), w_spec(3),
                  pl.BlockSpec((3, 3, cb), lambda j, i: (0, 0, j)),
                  pl.BlockSpec((3, cb), lambda j, i: (0, j)),
                  pl.BlockSpec((HY_ORDER, 3, L, cb), lambda j, i: (0, 0, 0, j)),
                  pl.BlockSpec((HY_ORDER, cb), lambda j, i: (0, j)),
                  pl.BlockSpec((2 * L, L), lambda j, i: (0, 0)),
                  pl.BlockSpec((L, 2 * L), lambda j, i: (0, 0))],
        out_specs=pl.BlockSpec((nb, L, cb), lambda j, i: (i, 0, j)),
        out_shape=jax.ShapeDtypeStruct((bsz, L, d_hy), BF16),
        compiler_params=_params("arbitrary", "arbitrary"),
        name="hyena",
    )(h, w_in_bf, w_in_bf, w_in_bf, w_in_bf,
      conv_w.reshape(3, 3, d_hy).transpose(1, 0, 2), conv_b.reshape(3, d_hy), spec, hy_bias, f, g)


def _mlstm_kernel(*refs, L, has_state):
    (h_ref, wq_ref, wk_ref, wv_ref, wo_ref, wz_ref, wg_ref, gb_ref, cwq_ref, cwk_ref,
     cbq_ref, cbk_ref, ng_ref) = refs[:13]
    refs = refs[13:]
    if has_state:
        c0_ref, n0_ref, m0_ref = refs[:3]
        refs = refs[3:]
        (yb_ref,) = refs[:1]
        refs = refs[1:]
    else:
        yb_ref, cn_ref, nn_ref, mn_ref = refs[:4]
        refs = refs[4:]
    q_s, k_s, kt_s, v_s, lf_s, gt_s, lft_s, hsum_s, c_s = refs

    T = ML_CHUNK
    nc = L // T
    dh = q_s.shape[-1]
    hb = h_ref[...]
    q = _silu(_short_conv3(_dot(hb, wq_ref[...]), cwq_ref, cbq_ref[...]))
    k = _silu(_short_conv3(_dot(hb, wk_ref[...]), cwk_ref, cbk_ref[...])) * (dh ** -0.5)
    q_s[...] = q.astype(BF16)
    k_s[...] = k.astype(BF16)
    kt_s[...] = k.T.astype(BF16)
    v_s[...] = _dot(hb, wv_ref[...]).astype(BF16)
    g = _dot(hb, wg_ref[...]) + gb_ref[...]
    lf = _log_sigmoid(g)
    lf_s[...] = lf
    gt_s[...] = g.T[0:8, :]
    lft_s[...] = lf.T[0:8, :]

    ti = lax.broadcasted_iota(jnp.int32, (T, T), 0)
    si = lax.broadcasted_iota(jnp.int32, (T, T), 1)
    lower = (si <= ti).astype(F32)
    upper = (si >= ti).astype(F32)

    for dr in range(2):
        if has_state:
            c_s[...] = c0_ref[dr]
            n_init = n0_ref[dr]
            m_init = m0_ref[dr][:, 0:1]
        else:
            c_s[...] = jnp.zeros_like(c_s)
            n_init = jnp.zeros((1, dh), F32)
            m_init = jnp.zeros((1, 1), F32)

        def step(ci, carry, dr=dr):
            n_row, m = carry
            c = ci if dr == 0 else nc - 1 - ci
            r0 = pl.multiple_of(c * T, T)
            qc = q_s[pl.ds(r0, T), :]
            kc = k_s[pl.ds(r0, T), :]
            vc = v_s[pl.ds(r0, T), :]
            ktc = kt_s[:, pl.ds(r0, T)]
            lfc = lf_s[pl.ds(r0, T), :]
            lftc = lft_s[:, pl.ds(r0, T)]
            gtc = gt_s[:, pl.ds(r0, T)]
            if dr == 0:
                bcol = _dot_hi(lower, lfc)[:, 1:2]
                brow = _dot_hi(lftc, upper)[1:2, :]
                irow = gtc[0:1, :]
                mask = si <= ti
                bl = brow[:, T - 1:T]
            else:
                bcol = _dot_hi(upper, lfc)[:, 3:4]
                brow = _dot_hi(lftc, lower)[3:4, :]
                irow = gtc[2:3, :]
                mask = si >= ti
                bl = brow[:, 0:1]
            dlog = jnp.where(mask, bcol - brow + irow, NEG)
            inter = bcol + m
            mt = jnp.maximum(inter, jnp.max(dlog, axis=-1, keepdims=True))
            qk = lax.dot_general(qc, kc, (((1,), (1,)), ((), ())), preferred_element_type=F32)
            smat = qk * jnp.exp(dlog - mt)
            iw = jnp.exp(inter - mt)
            num = _dot(smat.astype(BF16), vc) + iw * _dot(qc, c_s[...].astype(BF16))
            qn = jnp.sum(qc.astype(F32) * n_row, axis=-1, keepdims=True)
            den = jnp.sum(smat, axis=-1, keepdims=True) + iw * qn
            hh = num / jnp.maximum(jnp.abs(den), jnp.exp(-mt))
            if dr == 0:
                hsum_s[pl.ds(r0, T), :] = hh
            else:
                hsum_s[pl.ds(r0, T), :] = hsum_s[pl.ds(r0, T), :] + hh
            wlog = bl - brow + irow
            mn = jnp.maximum(bl + m, jnp.max(wlog, axis=-1, keepdims=True))
            ws = jnp.exp(wlog - mn)
            dec = jnp.exp(bl + m - mn)
            kw = (ktc.astype(F32) * ws).astype(BF16)
            c_s[...] = dec * c_s[...] + _dot(kw, vc)
            ws8 = jnp.broadcast_to(ws, (8, T)).astype(BF16)
            n_new = dec * n_row + _dot(ws8, kc)[0:1, :]
            return n_new, mn

        n_fin, m_fin = lax.fori_loop(0, nc, step, (n_init, m_init))
        if not has_state:
            cn_ref[dr] = c_s[...]
            nn_ref[dr] = n_fin
            mn_ref[dr] = jnp.broadcast_to(m_fin, (1, LANES))

    hs = hsum_s[...]
    hm = hs * lax.rsqrt(jnp.mean(hs * hs, axis=-1, keepdims=True) + EPS) * ng_ref[...]
    uo = _dot(hb, wo_ref[...])
    uz = _dot(hb, wz_ref[...])
    yb_ref[...] = (hm * _sigmoid(uo) * _silu(uz)).astype(yb_ref.dtype)


def _mlstm(h, l, w_in_bf, wg_heads, gb_heads, conv_w, conv_b, norm_g, states, seg_off):
    bsz, L, d = h.shape
    heads = wg_heads.shape[0]
    d_ml = norm_g.shape[-1]
    dh = d_ml // heads
    has_state = states is not None
    qo, ko, vo, oo, zo = (s // dh for s in seg_off)

    def w_spec(off):
        return pl.BlockSpec((None, d, dh), lambda b, hd, off=off: (l, 0, off + hd))

    def vec_spec(rows, off):
        return pl.BlockSpec((rows, dh), lambda b, hd, off=off: (0, off + hd))

    in_specs = [pl.BlockSpec((None, L, d), lambda b, hd: (b, 0, 0)),
                w_spec(qo), w_spec(ko), w_spec(vo), w_spec(oo), w_spec(zo),
                pl.BlockSpec((None, d, LANES), lambda b, hd: (hd, 0, 0)),
                pl.BlockSpec((None, 1, LANES), lambda b, hd: (hd, 0, 0)),
                vec_spec(3, 0), vec_spec(3, heads), vec_spec(1, 0), vec_spec(1, heads),
                vec_spec(1, 0)]
    args = [h, w_in_bf, w_in_bf, w_in_bf, w_in_bf, w_in_bf, wg_heads, gb_heads,
            conv_w, conv_w, conv_b.reshape(1, -1), conv_b.reshape(1, -1), norm_g.reshape(1, -1)]
    yb_spec = pl.BlockSpec((None, L, dh), lambda b, hd: (b, 0, hd))
    yb_shape = jax.ShapeDtypeStruct((bsz, L, d_ml), BF16)
    c_spec = pl.BlockSpec((None, 2, None, dh, dh), lambda b, hd: (b, 0, hd, 0, 0))
    n_spec = pl.BlockSpec((None, 2, None, 1, dh), lambda b, hd: (b, 0, hd, 0, 0))
    m_spec = pl.BlockSpec((None, 2, None, 1, LANES), lambda b, hd: (b, 0, hd, 0, 0))
    if has_state:
        in_specs += [c_spec, n_spec, m_spec]
        args += list(states)
        out_specs, out_shape = yb_spec, yb_shape
    else:
        out_specs = (yb_spec, c_spec, n_spec, m_spec)
        out_shape = (yb_shape,
                     jax.ShapeDtypeStruct((bsz, 2, heads, dh, dh), F32),
                     jax.ShapeDtypeStruct((bsz, 2, heads, 1, dh), F32),
                     jax.ShapeDtypeStruct((bsz, 2, heads, 1, LANES), F32))
    scratch = [pltpu.VMEM((L, dh), BF16), pltpu.VMEM((L, dh), BF16), pltpu.VMEM((dh, L), BF16),
               pltpu.VMEM((L, dh), BF16), pltpu.VMEM((L, LANES), F32), pltpu.VMEM((8, L), F32),
               pltpu.VMEM((8, L), F32), pltpu.VMEM((L, dh), F32), pltpu.VMEM((dh, dh), F32)]
    return pl.pallas_call(
        functools.partial(_mlstm_kernel, L=L, has_state=has_state),
        grid=(bsz, heads),
        in_specs=in_specs,
        out_specs=out_specs,
        out_shape=out_shape,
        scratch_shapes=scratch,
        compiler_params=_params("arbitrary", "arbitrary"),
        name="mlstm",
    )(*args)


def _out_kernel(x_ref, h_ref, ya_ref, yb_ref, mod_ref, wga_ref, wgb_ref, wpa_ref, wpb_ref,
                wout_ref, o_ref):
    hb = h_ref[...]
    ga = _sigmoid(_dot(hb, wga_ref[...]))
    gb = _sigmoid(_dot(hb, wgb_ref[...]))
    merged = ga * _dot(ya_ref[...], wpa_ref[...]) + gb * _dot(yb_ref[...], wpb_ref[...])
    o_ref[...] = x_ref[...] + mod_ref[2:3, :] * _dot(merged.astype(BF16), wout_ref[...])


def _out(x, h, ya, yb, mod, l, w_gab, w_pa, w_pb, w_out):
    bm, t, d = x.shape
    tm = min(t, OUT_TM)
    tok = lambda b, i: (b, i, 0)
    wl = lambda b, i: (l, 0, 0)
    return pl.pallas_call(
        _out_kernel,
        grid=(bm, t // tm),
        in_specs=[pl.BlockSpec((None, tm, d), tok),
                  pl.BlockSpec((None, tm, d), tok),
                  pl.BlockSpec((None, tm, d), tok),
                  pl.BlockSpec((None, tm, d), tok),
                  pl.BlockSpec((None, 3, d), lambda b, i: (b, 0, 0)),
                  pl.BlockSpec((None, d, d), wl),
                  pl.BlockSpec((None, d, d), lambda b, i: (l, 0, 1)),
                  pl.BlockSpec((None, d, d), wl),
                  pl.BlockSpec((None, d, d), wl),
                  pl.BlockSpec((None, d, d), wl)],
        out_specs=pl.BlockSpec((None, tm, d), tok),
        out_shape=jax.ShapeDtypeStruct((bm, t, d), F32),
        compiler_params=_params("arbitrary", "arbitrary"),
        name="merge_out",
    )(x, h, ya, yb, mod, w_gab, w_gab, w_pa, w_pb, w_out)


def kernel(x_prompt, x_sample, state_C, state_n, state_m, c, c_ctx, norm_g, w_ada, b_ada, w_in, hy_conv_w, hy_conv_b, hy_w1, hy_b1, hy_w2, hy_b2, hy_w3, hy_b3, hy_freq, hy_decay, hy_bias, ml_conv_w, ml_conv_b, ml_if_b, ml_norm_g, w_pa, w_pb, w_out, final_g):
    depth, d, _ = w_in.shape
    d_hy = hy_bias.shape[-1]
    d_ml = ml_norm_g.shape[-1]
    heads = ml_if_b.shape[-1]
    bp, lp, _ = x_prompt.shape
    bs, ls, _ = x_sample.shape

    off_q = 4 * d_hy
    off_v = off_q + 2 * d_ml
    off_o = off_v + d_ml
    off_z = off_o + d_ml
    off_g = off_z + d_ml
    off_ga = off_g + 4 * heads
    seg_off = (off_q, off_q + d_ml, off_v, off_o, off_z)

    w_in_bf = w_in.astype(BF16)
    w_gab = w_in[:, :, off_ga:].astype(BF16)
    wg = w_in[:, :, off_g:off_ga].reshape(depth, d, 4, heads).transpose(0, 3, 1, 2)
    wg_heads = jnp.zeros((depth, heads, d, LANES), BF16).at[..., :4].set(wg.astype(BF16))
    gb = ml_if_b.reshape(depth, 4, heads).transpose(0, 2, 1)
    gb_heads = jnp.zeros((depth, heads, 1, LANES), F32).at[:, :, 0, :4].set(gb)
    w_pa_bf, w_pb_bf, w_out_bf = (w.astype(BF16) for w in (w_pa, w_pb, w_out))

    cvecs = jnp.zeros((8, d), F32).at[0].set(c_ctx).at[1:1 + bs].set(c)
    mods = _mods(cvecs, w_ada, b_ada).reshape(depth, 8, 3, d)

    xp = x_prompt.reshape(1, bp * lp, d)
    xs = x_sample
    new_c, new_n, new_m = [], [], []
    for l in range(depth):
        spectra = {}
        for L in sorted({lp, ls}):
            spectra[L] = _hyena_spectra(L, l, hy_w1, hy_b1, hy_w2, hy_b2, hy_w3, hy_b3,
                                        hy_freq, hy_decay)

        def layer(x, mod, bsz, L, states):
            h = _norm_mod(x, norm_g[l], mod)
            hseq = h.reshape(bsz, L, d)
            ya = _hyena(hseq, l, w_in_bf, hy_conv_w[l], hy_conv_b[l], spectra[L], hy_bias[l])
            res = _mlstm(hseq, l, w_in_bf, wg_heads[l], gb_heads[l], ml_conv_w[l], ml_conv_b[l],
                         ml_norm_g[l], states, seg_off)
            yb, fin = (res, None) if states is not None else (res[0], res[1:])
            xn = _out(x, h, ya.reshape(x.shape), yb.reshape(x.shape), mod, l,
                      w_gab, w_pa_bf, w_pb_bf, w_out_bf)
            return xn, fin

        xp, fin = layer(xp, mods[l, 0:1], bp, lp, None)
        new_c.append(fin[0])
        new_n.append(fin[1][:, :, :, 0, :])
        new_m.append(fin[2][:, :, :, 0, 0])
        cached = (state_C[:, l], state_n[:, l].reshape(bs, 2, heads, 1, -1),
                  jnp.broadcast_to(state_m[:, l][..., None, None], (bs, 2, heads, 1, LANES)))
        xs, _ = layer(xs, mods[l, 1:1 + bs], bs, ls, cached)

    y_prompt = _final_norm(xp.reshape(bp * lp, d), final_g).reshape(bp, lp, d)
    y_sample = _final_norm(xs.reshape(bs * ls, d), final_g).reshape(bs, ls, d)
    return (y_prompt, y_sample, jnp.stack(new_c, axis=1), jnp.stack(new_n, axis=1),
            jnp.stack(new_m, axis=1))
```

```python
import functools
import math

import numpy as np
import jax
import jax.numpy as jnp
from jax import lax
from jax.experimental import pallas as pl
from jax.experimental.pallas import tpu as pltpu

F32 = jnp.float32
BF16 = jnp.bfloat16
HIGHEST = lax.Precision.HIGHEST

HY_ORDER = 2
HY_BANDS = 16
HY_SHIFT = 0.05
ML_CHUNK = 128
EPS = 1e-6
NEG = -1e30

LANES = 128
VMEM_LIMIT = 56 * 1024 * 1024
HY_CB = 256
ML_ROWS = 1024
OUT_TM = 512


def _dot(a, b):
    return jnp.dot(a, b, preferred_element_type=F32)


def _dot_hi(a, b):
    return jnp.dot(a, b, preferred_element_type=F32, precision=HIGHEST)


def _sigmoid(x):
    return 1.0 / (1.0 + jnp.exp(-x))


def _silu(x):
    return x * _sigmoid(x)


def _log_sigmoid(x):
    return jnp.minimum(x, 0.0) - jnp.log(1.0 + jnp.exp(-jnp.abs(x)))


def _params(*sem):
    return pltpu.CompilerParams(dimension_semantics=sem, vmem_limit_bytes=VMEM_LIMIT)


@functools.lru_cache(maxsize=None)
def _dft_table(L):
    N = 2 * L
    k = np.arange(L, dtype=np.int64)[:, None]
    n = np.arange(L, dtype=np.int64)[None, :]
    ang = 2.0 * np.pi * ((k * n) % N).astype(np.float64) / N
    c = np.cos(ang)
    s = -np.sin(ang)
    s[0, :] = 1.0 - 2.0 * (np.arange(L) % 2)
    return np.concatenate([c, s], axis=0).astype(np.float32)


def _dft_operands(L):
    f = jnp.asarray(_dft_table(L))
    f_hi = f.astype(BF16)
    f_lo = (f - f_hi.astype(F32)).astype(BF16)
    return f_hi, f_lo, f_hi.T


def _mod_kernel(c_ref, w_ref, b_ref, o_ref):
    o_ref[...] = _dot_hi(_silu(c_ref[...]), w_ref[...]) + b_ref[...]


def _mods(cvecs, w_ada, b_ada):
    depth, d, d3 = w_ada.shape
    r = cvecs.shape[0]
    tn = 512
    return pl.pallas_call(
        _mod_kernel,
        grid=(depth, d3 // tn),
        in_specs=[pl.BlockSpec((r, d), lambda l, j: (0, 0)),
                  pl.BlockSpec((None, d, tn), lambda l, j: (l, 0, j)),
                  pl.BlockSpec((None, 1, tn), lambda l, j: (l, 0, j))],
        out_specs=pl.BlockSpec((None, r, tn), lambda l, j: (l, 0, j)),
        out_shape=jax.ShapeDtypeStruct((depth, r, d3), F32),
        compiler_params=_params("arbitrary", "arbitrary"),
        name="adaln_mod",
    )(cvecs, w_ada, b_ada.reshape(depth, 1, d3))


def _norm_mod_kernel(x_ref, g_ref, mod_ref, o_ref):
    x = x_ref[...]
    y = x * lax.rsqrt(jnp.mean(x * x, axis=-1, keepdims=True) + EPS) * g_ref[...]
    o_ref[...] = (y * (1.0 + mod_ref[1:2, :]) + mod_ref[0:1, :]).astype(o_ref.dtype)


def _norm_mod(x, g, mod):
    bm, t, d = x.shape
    tm = min(t, 1024)
    return pl.pallas_call(
        _norm_mod_kernel,
        grid=(bm, t // tm),
        in_specs=[pl.BlockSpec((None, tm, d), lambda b, i: (b, i, 0)),
                  pl.BlockSpec((1, d), lambda b, i: (0, 0)),
                  pl.BlockSpec((None, 3, d), lambda b, i: (b, 0, 0))],
        out_specs=pl.BlockSpec((None, tm, d), lambda b, i: (b, i, 0)),
        out_shape=jax.ShapeDtypeStruct((bm, t, d), BF16),
        compiler_params=_params("arbitrary", "arbitrary"),
        name="norm_mod",
    )(x, g.reshape(1, d), mod)


def _final_norm_kernel(x_ref, g_ref, o_ref):
    x = x_ref[...]
    o_ref[...] = x * lax.rsqrt(jnp.mean(x * x, axis=-1, keepdims=True) + EPS) * g_ref[...]


def _final_norm(x, g):
    n, d = x.shape
    tm = 1024
    return pl.pallas_call(
        _final_norm_kernel,
        grid=(n // tm,),
        in_specs=[pl.BlockSpec((tm, d), lambda i: (i, 0)),
                  pl.BlockSpec((1, d), lambda i: (0, 0))],
        out_specs=pl.BlockSpec((tm, d), lambda i: (i, 0)),
        out_shape=jax.ShapeDtypeStruct((n, d), F32),
        compiler_params=_params("arbitrary"),
        name="final_norm",
    )(x, g.reshape(1, d))


def _dot3(f_hi, f_lo, a):
    a_hi = a.astype(BF16)
    a_lo = (a - a_hi.astype(F32)).astype(BF16)
    return _dot(f_hi, a_hi) + (_dot(f_lo, a_hi) + _dot(f_hi, a_lo))


def _filter_kernel(w1_ref, b1_ref, w2_ref, b2_ref, freq_ref, w3_00, w3_01, w3_10, w3_11,
                   b3_ref, decay_ref, fhi_ref, flo_ref, spec_ref, hdn_ref, *, L):
    cb = spec_ref.shape[-1]

    @pl.when(pl.program_id(0) == 0)
    def _():
        row = lax.broadcasted_iota(jnp.int32, (L, LANES), 0).astype(F32)
        lane = lax.broadcasted_iota(jnp.int32, (L, LANES), 1)
        t = row / L
        band = jnp.where(lane <= HY_BANDS, lane, lane - HY_BANDS).astype(F32)
        ang = 2.0 * math.pi * t * band
        feats = jnp.where(lane == 0, t,
                          jnp.where(lane <= HY_BANDS, jnp.cos(ang),
                                    jnp.where(lane <= 2 * HY_BANDS, jnp.sin(ang), 0.0)))
        hdn = jnp.sin(freq_ref[0:1, :] * (_dot_hi(feats, w1_ref[...]) + b1_ref[...]))
        hdn_ref[...] = jnp.sin(freq_ref[1:2, :] * (_dot_hi(hdn, w2_ref[...]) + b2_ref[...]))

    hdn = hdn_ref[...]
    row_i = lax.broadcasted_iota(jnp.int32, (L, cb), 0)
    t = row_i.astype(F32) / L
    first = row_i == 0
    sign = jnp.where((row_i & 1) == 0, 1.0, -1.0)
    w3 = ((w3_00, w3_01), (w3_10, w3_11))
    for o in range(HY_ORDER):
        hs = []
        for dr in range(2):
            j = 2 * o + dr
            hv = _dot_hi(hdn, w3[o][dr][...]) + b3_ref[j:j + 1, :]
            win = jnp.exp(-t * jnp.abs(decay_ref[j:j + 1, :])) + HY_SHIFT
            hs.append(hv * win)
        hf = hs[0]
        hb = jnp.where(first, 0.0, hs[1])
        l1 = jnp.sum(jnp.abs(hf), axis=0, keepdims=True) + jnp.sum(jnp.abs(hb), axis=0, keepdims=True)
        hf = hf / l1
        hb = hb / l1
        even = hf + hb
        odd = hf - hb
        re = _dot3(fhi_ref[0:L, :], flo_ref[0:L, :], even)
        im = _dot3(fhi_ref[L:2 * L, :], flo_ref[L:2 * L, :], odd)
        nyq = jnp.sum(even * sign, axis=0, keepdims=True)
        a = re * jnp.where(first, 0.5 / L, 1.0 / L)
        spec_ref[o, 0] = a
        spec_ref[o, 1] = jnp.where(first, 0.0, im * (1.0 / L))
        spec_ref[o, 2] = jnp.where(first, nyq * (0.5 / L), a)


def _hyena_spectra(dft, l, hy_w1, hy_b1, hy_w2, hy_b2, hy_w3, hy_b3, hy_freq, hy_decay):
    emb, ff = hy_w1.shape[1:]
    d_hy = hy_decay.shape[-1]
    cb = HY_CB
    ncb = d_hy // cb
    w1p = jnp.zeros((LANES, ff), F32).at[:emb].set(hy_w1[l])
    f_hi, f_lo, _ = dft
    L = f_hi.shape[1]
    const = lambda j: (0, 0)

    def w3_spec(o, dr):
        return pl.BlockSpec((None, ff, cb), lambda j, o=o, dr=dr: (l, 0, (2 * o + dr) * ncb + j))

    return pl.pallas_call(
        functools.partial(_filter_kernel, L=L),
        grid=(ncb,),
        in_specs=[pl.BlockSpec((LANES, ff), const),
                  pl.BlockSpec((1, ff), const),
                  pl.BlockSpec((ff, ff), const),
                  pl.BlockSpec((1, ff), const),
                  pl.BlockSpec((2, ff), const),
                  w3_spec(0, 0), w3_spec(0, 1), w3_spec(1, 0), w3_spec(1, 1),
                  pl.BlockSpec((2 * HY_ORDER, cb), lambda j: (0, j)),
                  pl.BlockSpec((2 * HY_ORDER, cb), lambda j: (0, j)),
                  pl.BlockSpec((2 * L, L), const),
                  pl.BlockSpec((2 * L, L), const)],
        out_specs=pl.BlockSpec((HY_ORDER, 3, L, cb), lambda j: (0, 0, 0, j)),
        out_shape=jax.ShapeDtypeStruct((HY_ORDER, 3, L, d_hy), F32),
        scratch_shapes=[pltpu.VMEM((L, ff), F32)],
        compiler_params=_params("arbitrary"),
        name="hyena_filter",
    )(w1p, hy_b1[l].reshape(1, ff), hy_w2[l], hy_b2[l].reshape(1, ff), hy_freq[l],
      hy_w3, hy_w3, hy_w3, hy_w3,
      hy_b3[l].reshape(2 * HY_ORDER, d_hy), hy_decay[l].reshape(2 * HY_ORDER, d_hy),
      f_hi, f_lo)


def _short_conv3(x, w_ref, b, seq=None):
    n = x.shape[0]
    seq = n if seq is None else seq
    row = lax.broadcasted_iota(jnp.int32, x.shape, 0)
    first = functools.reduce(jnp.logical_or, [row == s for s in range(0, n, seq)])
    last = functools.reduce(jnp.logical_or, [row == s + seq - 1 for s in range(0, n, seq)])
    prev = jnp.where(first, 0.0, pltpu.roll(x, 1, axis=0))
    nxt = jnp.where(last, 0.0, pltpu.roll(x, n - 1, axis=0))
    return ((b + prev * w_ref[0:1, :]) + x * w_ref[1:2, :]) + nxt * w_ref[2:3, :]


def _hyena_kernel(h_ref, wx1_ref, wx2_ref, wv_ref, wz_ref, cw_ref, cb_ref, spec_ref, hbias_ref,
                  f_ref, g_ref, o_ref, *, nb, L):
    d = h_ref.shape[-1]
    hflat = h_ref[...].reshape(nb * L, d)
    u1 = _dot(hflat, wx1_ref[...])
    u2 = _dot(hflat, wx2_ref[...])
    uv = _dot(hflat, wv_ref[...])
    uz = _dot(hflat, wz_ref[...])
    for b in range(nb):
        rows = slice(b * L, (b + 1) * L)
        gates = (_short_conv3(u1[rows], cw_ref.at[0], cb_ref[0:1, :]),
                 _short_conv3(u2[rows], cw_ref.at[1], cb_ref[1:2, :]))
        z = _short_conv3(uv[rows], cw_ref.at[2], cb_ref[2:3, :])
        for o in range(HY_ORDER):
            zf = _dot(f_ref[...], z.astype(BF16))
            zc, zs = zf[0:L], zf[L:2 * L]
            a, bm, dm = spec_ref[o, 0], spec_ref[o, 1], spec_ref[o, 2]
            yc = zc * a - zs * bm
            ys = zc * bm + zs * dm
            ycat = jnp.concatenate([yc, ys], axis=0).astype(BF16)
            y = _dot(g_ref[...], ycat)
            z = gates[o] * (y + hbias_ref[o:o + 1, :] * z)
        o_ref[b] = (z * _silu(uz[rows])).astype(o_ref.dtype)


def _hyena(h, l, w_in_bf, conv_w, conv_b, spec, hy_bias, dft):
    bsz, L, d = h.shape
    d_hy = hy_bias.shape[-1]
    cb = HY_CB
    ncb = d_hy // cb
    nb = max(1, min(bsz, 1024 // L))
    f, _, g = dft

    def w_spec(seg):
        return pl.BlockSpec((None, d, cb), lambda j, i, seg=seg: (l, 0, seg * ncb + j))

    return pl.pallas_call(
        functools.partial(_hyena_kernel, nb=nb, L=L),
        grid=(ncb, bsz // nb),
        in_specs=[pl.BlockSpec((nb, L, d), lambda j, i: (i, 0, 0)),
                  w_spec(0), w_spec(1), w_spec(2), w_spec(3),
                  pl.BlockSpec((3, 3, cb), lambda j, i: (0, 0, j)),
                  pl.BlockSpec((3, cb), lambda j, i: (0, j)),
                  pl.BlockSpec((HY_ORDER, 3, L, cb), lambda j, i: (0, 0, 0, j)),
                  pl.BlockSpec((HY_ORDER, cb), lambda j, i: (0, j)),
                  pl.BlockSpec((2 * L, L), lambda j, i: (0, 0)),
                  pl.BlockSpec((L, 2 * L), lambda j, i: (0, 0))],
        out_specs=pl.BlockSpec((nb, L, cb), lambda j, i: (i, 0, j)),
        out_shape=jax.ShapeDtypeStruct((bsz, L, d_hy), BF16),
        compiler_params=_params("arbitrary", "arbitrary"),
        name="hyena",
    )(h, w_in_bf, w_in_bf, w_in_bf, w_in_bf,
      conv_w.reshape(3, 3, d_hy).transpose(1, 0, 2), conv_b.reshape(3, d_hy), spec, hy_bias, f, g)


def _mlstm_kernel(*refs, nb, L, has_state):
    (h_ref, wq_ref, wk_ref, wv_ref, wo_ref, wz_ref, wg_ref, gb_ref, cwq_ref, cwk_ref,
     cbq_ref, cbk_ref, ng_ref) = refs[:13]
    refs = refs[13:]
    if has_state:
        c0_ref, n0_ref, m0_ref, yb_ref = refs[:4]
        refs = refs[4:]
    else:
        yb_ref, cn_ref, nn_ref, mn_ref = refs[:4]
        refs = refs[4:]
    q_s, k_s, kt_s, v_s, g2_s, g2t_s, sc_s, hsum_s, c_s = refs

    T = ML_CHUNK
    nc = L // T
    R = nb * L
    d = h_ref.shape[-1]
    dh = q_s.shape[-1]
    hb = h_ref[...].reshape(R, d)
    q = _silu(_short_conv3(_dot(hb, wq_ref[...]), cwq_ref, cbq_ref[...], L))
    k = _silu(_short_conv3(_dot(hb, wk_ref[...]), cwk_ref, cbk_ref[...], L)) * (dh ** -0.5)
    q_s[...] = q.astype(BF16)
    k_s[...] = k.astype(BF16)
    kt_s[...] = k.T.astype(BF16)
    v_s[...] = _dot(hb, wv_ref[...]).astype(BF16)

    g = _dot(hb, wg_ref[...]) + gb_ref[...]
    lf = _log_sigmoid(g)
    ti = lax.broadcasted_iota(jnp.int32, (T, T), 0)
    si = lax.broadcasted_iota(jnp.int32, (T, T), 1)
    causal = si <= ti
    anti = si >= ti
    lower = causal.astype(F32)
    upper = anti.astype(F32)
    lane = lax.broadcasted_iota(jnp.int32, (T, LANES), 1)
    for j in range(nb * nc):
        rows = slice(j * T, (j + 1) * T)
        pre = _dot_hi(lower, lf[rows])
        suf = _dot_hi(upper, lf[rows])
        g2_s[rows, :] = jnp.where(lane == 1, pre, jnp.where(lane == 3, suf, g[rows]))
    g2t_s[...] = g2_s[...].T[0:8, :]

    zero11 = jnp.zeros((1, 1), F32)
    m_fin = []
    for b in range(nb):
        vals = [[None] * 6 for _ in range(nc)]
        for dr in range(2):
            m = m0_ref[b, dr][:, 0:1] if has_state else zero11
            for ci in range(nc):
                c = ci if dr == 0 else nc - 1 - ci
                cols = slice(b * L + c * T, b * L + (c + 1) * T)
                irow = g2t_s[2 * dr:2 * dr + 1, cols]
                brow = g2t_s[2 * dr + 1:2 * dr + 2, cols]
                bl = brow[:, T - 1:T] if dr == 0 else brow[:, 0:1]
                mw = jnp.max(bl - brow + irow, axis=-1, keepdims=True)
                mn = jnp.maximum(bl + m, mw)
                vals[c][3 * dr:3 * dr + 3] = [m, mn, bl]
                m = mn
            m_fin.append(m)
        for c in range(nc):
            rows8 = [jnp.broadcast_to(v, (1, LANES)) for v in vals[c]] + [jnp.zeros((2, LANES), F32)]
            sc_s[b * nc + c] = jnp.concatenate(rows8, axis=0)

    hsum_s[...] = jnp.zeros_like(hsum_s)
    for b in range(nb):
        for dr in range(2):
            c_s[2 * b + dr] = c0_ref[b, dr] if has_state else jnp.zeros((dh, dh), F32)
    if has_state:
        n_init = tuple(n0_ref[b, dr] for b in range(nb) for dr in range(2))
    else:
        n_init = tuple(jnp.zeros((1, dh), F32) for _ in range(2 * nb))

    def step(ci, n_rows):
        n_out = []
        for b in range(nb):
            for dr in range(2):
                c = ci if dr == 0 else nc - 1 - ci
                r0 = pl.multiple_of(b * L + c * T, T)
                qc = q_s[pl.ds(r0, T), :]
                kc = k_s[pl.ds(r0, T), :]
                vc = v_s[pl.ds(r0, T), :]
                ktc = kt_s[:, pl.ds(r0, T)]
                bcol = g2_s[pl.ds(r0, T), :][:, 2 * dr + 1:2 * dr + 2]
                g2tc = g2t_s[:, pl.ds(r0, T)]
                irow = g2tc[2 * dr:2 * dr + 1, :]
                brow = g2tc[2 * dr + 1:2 * dr + 2, :]
                sc = sc_s[b * nc + c]
                m_prev = sc[3 * dr:3 * dr + 1, 0:1]
                m_new = sc[3 * dr + 1:3 * dr + 2, 0:1]
                bl = sc[3 * dr + 2:3 * dr + 3, 0:1]
                n_row = n_rows[2 * b + dr]
                c_old = c_s[2 * b + dr]

                dlog = jnp.where(causal if dr == 0 else anti, bcol - brow + irow, NEG)
                inter = bcol + m_prev
                mt = jnp.maximum(inter, jnp.max(dlog, axis=-1, keepdims=True))
                smat = _dot(qc, ktc) * jnp.exp(dlog - mt)
                iw = jnp.exp(inter - mt)
                num = _dot(smat.astype(BF16), vc) + iw * _dot(qc, c_old.astype(BF16))
                qn = jnp.sum(qc.astype(F32) * n_row, axis=-1, keepdims=True)
                den = jnp.sum(smat, axis=-1, keepdims=True) + iw * qn
                hh = num / jnp.maximum(jnp.abs(den), jnp.exp(-mt))
                hsum_s[pl.ds(r0, T), :] = hsum_s[pl.ds(r0, T), :] + hh

                ws = jnp.exp(bl - brow + irow - m_new)
                dec = jnp.exp(bl + m_prev - m_new)
                kw = (ktc.astype(F32) * ws).astype(BF16)
                c_s[2 * b + dr] = dec * c_old + _dot(kw, vc)
                ws8 = jnp.broadcast_to(ws, (8, T)).astype(BF16)
                n_out.append(dec * n_row + _dot(ws8, kc)[0:1, :])
        return tuple(n_out)

    n_fin = lax.fori_loop(0, nc, step, n_init)
    if not has_state:
        for b in range(nb):
            for dr in range(2):
                cn_ref[b, dr] = c_s[2 * b + dr]
                nn_ref[b, dr] = n_fin[2 * b + dr]
                mn_ref[b, dr] = jnp.broadcast_to(m_fin[2 * b + dr], (1, LANES))

    hs = hsum_s[...]
    hm = hs * lax.rsqrt(jnp.mean(hs * hs, axis=-1, keepdims=True) + EPS) * ng_ref[...]
    uo = _dot(hb, wo_ref[...])
    uz = _dot(hb, wz_ref[...])
    yb_ref[...] = (hm * _sigmoid(uo) * _silu(uz)).astype(yb_ref.dtype).reshape(yb_ref.shape)


def _mlstm(h, l, w_in_bf, wg_heads, gb_heads, conv_w, conv_b, norm_g, states, seg_off):
    bsz, L, d = h.shape
    heads = wg_heads.shape[0]
    d_ml = norm_g.shape[-1]
    dh = d_ml // heads
    has_state = states is not None
    nb = max(1, min(bsz, ML_ROWS // L))
    nc = L // ML_CHUNK
    qo, ko, vo, oo, zo = (s // dh for s in seg_off)

    def w_spec(off):
        return pl.BlockSpec((None, d, dh), lambda b, hd, off=off: (l, 0, off + hd))

    def vec_spec(rows, off):
        return pl.BlockSpec((rows, dh), lambda b, hd, off=off: (0, off + hd))

    in_specs = [pl.BlockSpec((nb, L, d), lambda b, hd: (b, 0, 0)),
                w_spec(qo), w_spec(ko), w_spec(vo), w_spec(oo), w_spec(zo),
                pl.BlockSpec((None, d, LANES), lambda b, hd: (hd, 0, 0)),
                pl.BlockSpec((None, 1, LANES), lambda b, hd: (hd, 0, 0)),
                vec_spec(3, 0), vec_spec(3, heads), vec_spec(1, 0), vec_spec(1, heads),
                vec_spec(1, 0)]
    args = [h, w_in_bf, w_in_bf, w_in_bf, w_in_bf, w_in_bf, wg_heads, gb_heads,
            conv_w, conv_w, conv_b.reshape(1, -1), conv_b.reshape(1, -1), norm_g.reshape(1, -1)]
    yb_spec = pl.BlockSpec((nb, L, dh), lambda b, hd: (b, 0, hd))
    yb_shape = jax.ShapeDtypeStruct((bsz, L, d_ml), BF16)
    c_spec = pl.BlockSpec((nb, 2, None, dh, dh), lambda b, hd: (b, 0, hd, 0, 0))
    n_spec = pl.BlockSpec((nb, 2, None, 1, dh), lambda b, hd: (b, 0, hd, 0, 0))
    m_spec = pl.BlockSpec((nb, 2, None, 1, LANES), lambda b, hd: (b, 0, hd, 0, 0))
    if has_state:
        in_specs += [c_spec, n_spec, m_spec]
        args += list(states)
        out_specs, out_shape = yb_spec, yb_shape
    else:
        out_specs = (yb_spec, c_spec, n_spec, m_spec)
        out_shape = (yb_shape,
                     jax.ShapeDtypeStruct((bsz, 2, heads, dh, dh), F32),
                     jax.ShapeDtypeStruct((bsz, 2, heads, 1, dh), F32),
                     jax.ShapeDtypeStruct((bsz, 2, heads, 1, LANES), F32))
    rows = nb * L
    scratch = [pltpu.VMEM((rows, dh), BF16), pltpu.VMEM((rows, dh), BF16),
               pltpu.VMEM((dh, rows), BF16), pltpu.VMEM((rows, dh), BF16),
               pltpu.VMEM((rows, LANES), F32), pltpu.VMEM((8, rows), F32),
               pltpu.VMEM((nb * nc, 8, LANES), F32), pltpu.VMEM((rows, dh), F32),
               pltpu.VMEM((2 * nb, dh, dh), F32)]
    return pl.pallas_call(
        functools.partial(_mlstm_kernel, nb=nb, L=L, has_state=has_state),
        grid=(bsz // nb, heads),
        in_specs=in_specs,
        out_specs=out_specs,
        out_shape=out_shape,
        scratch_shapes=scratch,
        compiler_params=_params("arbitrary", "arbitrary"),
        name="mlstm",
    )(*args)


def _out_kernel(x_ref, h_ref, ya_ref, yb_ref, mod_ref, wga_ref, wgb_ref, wpa_ref, wpb_ref,
                wout_ref, o_ref):
    hb = h_ref[...]
    ga = _sigmoid(_dot(hb, wga_ref[...]))
    gb = _sigmoid(_dot(hb, wgb_ref[...]))
    merged = ga * _dot(ya_ref[...], wpa_ref[...]) + gb * _dot(yb_ref[...], wpb_ref[...])
    o_ref[...] = x_ref[...] + mod_ref[2:3, :] * _dot(merged.astype(BF16), wout_ref[...])


def _out(x, h, ya, yb, mod, l, w_gab, w_pa, w_pb, w_out):
    bm, t, d = x.shape
    tm = min(t, OUT_TM)
    tok = lambda b, i: (b, i, 0)
    wl = lambda b, i: (l, 0, 0)
    return pl.pallas_call(
        _out_kernel,
        grid=(bm, t // tm),
        in_specs=[pl.BlockSpec((None, tm, d), tok),
                  pl.BlockSpec((None, tm, d), tok),
                  pl.BlockSpec((None, tm, d), tok),
                  pl.BlockSpec((None, tm, d), tok),
                  pl.BlockSpec((None, 3, d), lambda b, i: (b, 0, 0)),
                  pl.BlockSpec((None, d, d), wl),
                  pl.BlockSpec((None, d, d), lambda b, i: (l, 0, 1)),
                  pl.BlockSpec((None, d, d), wl),
                  pl.BlockSpec((None, d, d), wl),
                  pl.BlockSpec((None, d, d), wl)],
        out_specs=pl.BlockSpec((None, tm, d), tok),
        out_shape=jax.ShapeDtypeStruct((bm, t, d), F32),
        compiler_params=_params("arbitrary", "arbitrary"),
        name="merge_out",
    )(x, h, ya, yb, mod, w_gab, w_gab, w_pa, w_pb, w_out)


def kernel(x_prompt, x_sample, state_C, state_n, state_m, c, c_ctx, norm_g, w_ada, b_ada, w_in, hy_conv_w, hy_conv_b, hy_w1, hy_b1, hy_w2, hy_b2, hy_w3, hy_b3, hy_freq, hy_decay, hy_bias, ml_conv_w, ml_conv_b, ml_if_b, ml_norm_g, w_pa, w_pb, w_out, final_g):
    depth, d, _ = w_in.shape
    d_hy = hy_bias.shape[-1]
    d_ml = ml_norm_g.shape[-1]
    heads = ml_if_b.shape[-1]
    bp, lp, _ = x_prompt.shape
    bs, ls, _ = x_sample.shape

    off_q = 4 * d_hy
    off_v = off_q + 2 * d_ml
    off_o = off_v + d_ml
    off_z = off_o + d_ml
    off_g = off_z + d_ml
    off_ga = off_g + 4 * heads
    seg_off = (off_q, off_q + d_ml, off_v, off_o, off_z)

    w_in_bf = w_in.astype(BF16)
    w_gab = w_in[:, :, off_ga:].astype(BF16)
    wg = w_in[:, :, off_g:off_ga].reshape(depth, d, 4, heads).transpose(0, 3, 1, 2)
    wg_heads = jnp.zeros((depth, heads, d, LANES), BF16).at[..., :4].set(wg.astype(BF16))
    gb = ml_if_b.reshape(depth, 4, heads).transpose(0, 2, 1)
    gb_heads = jnp.zeros((depth, heads, 1, LANES), F32).at[:, :, 0, :4].set(gb)
    w_pa_bf, w_pb_bf, w_out_bf = (w.astype(BF16) for w in (w_pa, w_pb, w_out))

    cvecs = jnp.zeros((8, d), F32).at[0].set(c_ctx).at[1:1 + bs].set(c)
    mods = _mods(cvecs, w_ada, b_ada).reshape(depth, 8, 3, d)

    xp = x_prompt.reshape(1, bp * lp, d)
    xs = x_sample
    new_c, new_n, new_m = [], [], []
    dft = {L: _dft_operands(L) for L in sorted({lp, ls})}
    for l in range(depth):
        spectra = {}
        for L in sorted({lp, ls}):
            spectra[L] = _hyena_spectra(dft[L], l, hy_w1, hy_b1, hy_w2, hy_b2, hy_w3, hy_b3,
                                        hy_freq, hy_decay)

        def layer(x, mod, bsz, L, states):
            h = _norm_mod(x, norm_g[l], mod)
            hseq = h.reshape(bsz, L, d)
            ya = _hyena(hseq, l, w_in_bf, hy_conv_w[l], hy_conv_b[l], spectra[L], hy_bias[l],
                        dft[L])
            res = _mlstm(hseq, l, w_in_bf, wg_heads[l], gb_heads[l], ml_conv_w[l], ml_conv_b[l],
                         ml_norm_g[l], states, seg_off)
            yb, fin = (res, None) if states is not None else (res[0], res[1:])
            xn = _out(x, h, ya.reshape(x.shape), yb.reshape(x.shape), mod, l,
                      w_gab, w_pa_bf, w_pb_bf, w_out_bf)
            return xn, fin

        xp, fin = layer(xp, mods[l, 0:1], bp, lp, None)
        new_c.append(fin[0])
        new_n.append(fin[1][:, :, :, 0, :])
        new_m.append(fin[2][:, :, :, 0, 0])
        cached = (state_C[:, l], state_n[:, l].reshape(bs, 2, heads, 1, -1),
                  jnp.broadcast_to(state_m[:, l][..., None, None], (bs, 2, heads, 1, LANES)))
        xs, _ = layer(xs, mods[l, 1:1 + bs], bs, ls, cached)

    y_prompt = _final_norm(xp.reshape(bp * lp, d), final_g).reshape(bp, lp, d)
    y_sample = _final_norm(xs.reshape(bs * ls, d), final_g).reshape(bs, ls, d)
    return (y_prompt, y_sample, jnp.stack(new_c, axis=1), jnp.stack(new_n, axis=1),
            jnp.stack(new_m, axis=1))
```

```python
import functools
import math

import numpy as np
import jax
import jax.numpy as jnp
from jax import lax
from jax.experimental import pallas as pl
from jax.experimental.pallas import tpu as pltpu

F32 = jnp.float32
BF16 = jnp.bfloat16
HIGHEST = lax.Precision.HIGHEST

HY_ORDER = 2
HY_BANDS = 16
HY_SHIFT = 0.05
ML_CHUNK = 128
EPS = 1e-6
NEG = -1e30

LANES = 128
VMEM_LIMIT = 56 * 1024 * 1024
HY_CB = 256
ML_ROWS = 1024
HY_ROWS = 2048
OUT_TM = 512


def _dot(a, b):
    return jnp.dot(a, b, preferred_element_type=F32)


def _dot_hi(a, b):
    return jnp.dot(a, b, preferred_element_type=F32, precision=HIGHEST)


def _sigmoid(x):
    return 1.0 / (1.0 + jnp.exp(-x))


def _silu(x):
    return x * _sigmoid(x)


def _log_sigmoid(x):
    return jnp.minimum(x, 0.0) - jnp.log(1.0 + jnp.exp(-jnp.abs(x)))


def _params(*sem):
    return pltpu.CompilerParams(dimension_semantics=sem, vmem_limit_bytes=VMEM_LIMIT)


@functools.lru_cache(maxsize=None)
def _dft_table(L):
    N = 2 * L
    k = np.arange(L, dtype=np.int64)[:, None]
    n = np.arange(L, dtype=np.int64)[None, :]
    ang = 2.0 * np.pi * ((k * n) % N).astype(np.float64) / N
    c = np.cos(ang)
    s = -np.sin(ang)
    s[0, :] = 1.0 - 2.0 * (np.arange(L) % 2)
    return np.concatenate([c, s], axis=0).astype(np.float32)


def _dft_operands(L):
    f = jnp.asarray(_dft_table(L))
    f_hi = f.astype(BF16)
    f_lo = (f - f_hi.astype(F32)).astype(BF16)
    return f_hi, f_lo, f_hi.T


def _mod_kernel(c_ref, w_ref, b_ref, o_ref):
    o_ref[...] = _dot_hi(_silu(c_ref[...]), w_ref[...]) + b_ref[...]


def _mods(cvecs, w_ada, b_ada):
    depth, d, d3 = w_ada.shape
    r = cvecs.shape[0]
    tn = 512
    return pl.pallas_call(
        _mod_kernel,
        grid=(depth, d3 // tn),
        in_specs=[pl.BlockSpec((r, d), lambda l, j: (0, 0)),
                  pl.BlockSpec((None, d, tn), lambda l, j: (l, 0, j)),
                  pl.BlockSpec((None, 1, tn), lambda l, j: (l, 0, j))],
        out_specs=pl.BlockSpec((None, r, tn), lambda l, j: (l, 0, j)),
        out_shape=jax.ShapeDtypeStruct((depth, r, d3), F32),
        compiler_params=_params("arbitrary", "arbitrary"),
        name="adaln_mod",
    )(cvecs, w_ada, b_ada.reshape(depth, 1, d3))


def _norm_mod_kernel(x_ref, g_ref, mod_ref, o_ref):
    x = x_ref[...]
    y = x * lax.rsqrt(jnp.mean(x * x, axis=-1, keepdims=True) + EPS) * g_ref[...]
    o_ref[...] = (y * (1.0 + mod_ref[1:2, :]) + mod_ref[0:1, :]).astype(o_ref.dtype)


def _norm_mod(x, g, mod):
    bm, t, d = x.shape
    tm = min(t, 1024)
    return pl.pallas_call(
        _norm_mod_kernel,
        grid=(bm, t // tm),
        in_specs=[pl.BlockSpec((None, tm, d), lambda b, i: (b, i, 0)),
                  pl.BlockSpec((1, d), lambda b, i: (0, 0)),
                  pl.BlockSpec((None, 3, d), lambda b, i: (b, 0, 0))],
        out_specs=pl.BlockSpec((None, tm, d), lambda b, i: (b, i, 0)),
        out_shape=jax.ShapeDtypeStruct((bm, t, d), BF16),
        compiler_params=_params("arbitrary", "arbitrary"),
        name="norm_mod",
    )(x, g.reshape(1, d), mod)


def _final_norm_kernel(x_ref, g_ref, o_ref):
    x = x_ref[...]
    o_ref[...] = x * lax.rsqrt(jnp.mean(x * x, axis=-1, keepdims=True) + EPS) * g_ref[...]


def _final_norm(x, g):
    n, d = x.shape
    tm = 1024
    return pl.pallas_call(
        _final_norm_kernel,
        grid=(n // tm,),
        in_specs=[pl.BlockSpec((tm, d), lambda i: (i, 0)),
                  pl.BlockSpec((1, d), lambda i: (0, 0))],
        out_specs=pl.BlockSpec((tm, d), lambda i: (i, 0)),
        out_shape=jax.ShapeDtypeStruct((n, d), F32),
        compiler_params=_params("arbitrary"),
        name="final_norm",
    )(x, g.reshape(1, d))


def _dot3(f_hi, f_lo, a):
    a_hi = a.astype(BF16)
    a_lo = (a - a_hi.astype(F32)).astype(BF16)
    return _dot(f_hi, a_hi) + (_dot(f_lo, a_hi) + _dot(f_hi, a_lo))


def _filter_kernel(w1_ref, b1_ref, w2_ref, b2_ref, freq_ref, w3_00, w3_01, w3_10, w3_11,
                   b3_ref, decay_ref, fhi_ref, flo_ref, spec_ref, hdn_ref, *, L):
    cb = spec_ref.shape[-1]

    @pl.when(pl.program_id(0) == 0)
    def _():
        row = lax.broadcasted_iota(jnp.int32, (L, LANES), 0).astype(F32)
        lane = lax.broadcasted_iota(jnp.int32, (L, LANES), 1)
        t = row / L
        band = jnp.where(lane <= HY_BANDS, lane, lane - HY_BANDS).astype(F32)
        ang = 2.0 * math.pi * t * band
        feats = jnp.where(lane == 0, t,
                          jnp.where(lane <= HY_BANDS, jnp.cos(ang),
                                    jnp.where(lane <= 2 * HY_BANDS, jnp.sin(ang), 0.0)))
        hdn = jnp.sin(freq_ref[0:1, :] * (_dot_hi(feats, w1_ref[...]) + b1_ref[...]))
        hdn_ref[...] = jnp.sin(freq_ref[1:2, :] * (_dot_hi(hdn, w2_ref[...]) + b2_ref[...]))

    hdn = hdn_ref[...]
    row_i = lax.broadcasted_iota(jnp.int32, (L, cb), 0)
    t = row_i.astype(F32) / L
    first = row_i == 0
    sign = jnp.where((row_i & 1) == 0, 1.0, -1.0)
    w3 = ((w3_00, w3_01), (w3_10, w3_11))
    for o in range(HY_ORDER):
        hs = []
        for dr in range(2):
            j = 2 * o + dr
            hv = _dot_hi(hdn, w3[o][dr][...]) + b3_ref[j:j + 1, :]
            win = jnp.exp(-t * jnp.abs(decay_ref[j:j + 1, :])) + HY_SHIFT
            hs.append(hv * win)
        hf = hs[0]
        hb = jnp.where(first, 0.0, hs[1])
        l1 = jnp.sum(jnp.abs(hf), axis=0, keepdims=True) + jnp.sum(jnp.abs(hb), axis=0, keepdims=True)
        hf = hf / l1
        hb = hb / l1
        even = hf + hb
        odd = hf - hb
        re = _dot3(fhi_ref[0:L, :], flo_ref[0:L, :], even)
        im = _dot3(fhi_ref[L:2 * L, :], flo_ref[L:2 * L, :], odd)
        nyq = jnp.sum(even * sign, axis=0, keepdims=True)
        a = re * jnp.where(first, 0.5 / L, 1.0 / L)
        spec_ref[o, 0] = a
        spec_ref[o, 1] = jnp.where(first, 0.0, im * (1.0 / L))
        spec_ref[o, 2] = jnp.where(first, nyq * (0.5 / L), a)


def _hyena_spectra(dft, l, hy_w1, hy_b1, hy_w2, hy_b2, hy_w3, hy_b3, hy_freq, hy_decay):
    emb, ff = hy_w1.shape[1:]
    d_hy = hy_decay.shape[-1]
    cb = HY_CB
    ncb = d_hy // cb
    w1p = jnp.zeros((LANES, ff), F32).at[:emb].set(hy_w1[l])
    f_hi, f_lo, _ = dft
    L = f_hi.shape[1]
    const = lambda j: (0, 0)

    def w3_spec(o, dr):
        return pl.BlockSpec((None, ff, cb), lambda j, o=o, dr=dr: (l, 0, (2 * o + dr) * ncb + j))

    return pl.pallas_call(
        functools.partial(_filter_kernel, L=L),
        grid=(ncb,),
        in_specs=[pl.BlockSpec((LANES, ff), const),
                  pl.BlockSpec((1, ff), const),
                  pl.BlockSpec((ff, ff), const),
                  pl.BlockSpec((1, ff), const),
                  pl.BlockSpec((2, ff), const),
                  w3_spec(0, 0), w3_spec(0, 1), w3_spec(1, 0), w3_spec(1, 1),
                  pl.BlockSpec((2 * HY_ORDER, cb), lambda j: (0, j)),
                  pl.BlockSpec((2 * HY_ORDER, cb), lambda j: (0, j)),
                  pl.BlockSpec((2 * L, L), const),
                  pl.BlockSpec((2 * L, L), const)],
        out_specs=pl.BlockSpec((HY_ORDER, 3, L, cb), lambda j: (0, 0, 0, j)),
        out_shape=jax.ShapeDtypeStruct((HY_ORDER, 3, L, d_hy), F32),
        scratch_shapes=[pltpu.VMEM((L, ff), F32)],
        compiler_params=_params("arbitrary"),
        name="hyena_filter",
    )(w1p, hy_b1[l].reshape(1, ff), hy_w2[l], hy_b2[l].reshape(1, ff), hy_freq[l],
      hy_w3, hy_w3, hy_w3, hy_w3,
      hy_b3[l].reshape(2 * HY_ORDER, d_hy), hy_decay[l].reshape(2 * HY_ORDER, d_hy),
      f_hi, f_lo)


def _short_conv3(x, w_ref, b, seq=None):
    n = x.shape[0]
    seq = n if seq is None else seq
    row = lax.broadcasted_iota(jnp.int32, x.shape, 0)
    first = functools.reduce(jnp.logical_or, [row == s for s in range(0, n, seq)])
    last = functools.reduce(jnp.logical_or, [row == s + seq - 1 for s in range(0, n, seq)])
    prev = jnp.where(first, 0.0, pltpu.roll(x, 1, axis=0))
    nxt = jnp.where(last, 0.0, pltpu.roll(x, n - 1, axis=0))
    return ((b + prev * w_ref[0:1, :]) + x * w_ref[1:2, :]) + nxt * w_ref[2:3, :]


def _hyena_kernel(h_ref, wx1_ref, wx2_ref, wv_ref, wz_ref, cw_ref, cb_ref, spec_ref, hbias_ref,
                  f_ref, g_ref, o_ref, *, nb, L):
    seqs = range(nb)
    z = [_short_conv3(_dot(h_ref[b], wv_ref[...]), cw_ref.at[2], cb_ref[2:3, :]) for b in seqs]
    gate_w = (wx1_ref, wx2_ref)
    for o in range(HY_ORDER):
        zf = [_dot(f_ref[...], z[b].astype(BF16)) for b in seqs]
        gate = [_short_conv3(_dot(h_ref[b], gate_w[o][...]), cw_ref.at[o], cb_ref[o:o + 1, :])
                for b in seqs]
        ycat = []
        for b in seqs:
            zc, zs = zf[b][0:L], zf[b][L:2 * L]
            a, bm, dm = spec_ref[o, 0], spec_ref[o, 1], spec_ref[o, 2]
            ycat.append(jnp.concatenate([zc * a - zs * bm, zc * bm + zs * dm], axis=0).astype(BF16))
        y = [_dot(g_ref[...], ycat[b]) for b in seqs]
        z = [gate[b] * (y[b] + hbias_ref[o:o + 1, :] * z[b]) for b in seqs]
    for b in seqs:
        o_ref[b] = (z[b] * _silu(_dot(h_ref[b], wz_ref[...]))).astype(o_ref.dtype)


def _hyena(h, l, w_in_bf, conv_w, conv_b, spec, hy_bias, dft):
    bsz, L, d = h.shape
    d_hy = hy_bias.shape[-1]
    cb = HY_CB
    ncb = d_hy // cb
    nb = max(1, min(bsz, HY_ROWS // L))
    f, _, g = dft
    once = pl.Buffered(1)

    def w_spec(seg):
        return pl.BlockSpec((None, d, cb), lambda j, i, seg=seg: (l, 0, seg * ncb + j))

    return pl.pallas_call(
        functools.partial(_hyena_kernel, nb=nb, L=L),
        grid=(ncb, bsz // nb),
        in_specs=[pl.BlockSpec((nb, L, d), lambda j, i: (i, 0, 0)),
                  w_spec(0), w_spec(1), w_spec(2), w_spec(3),
                  pl.BlockSpec((3, 3, cb), lambda j, i: (0, 0, j)),
                  pl.BlockSpec((3, cb), lambda j, i: (0, j)),
                  pl.BlockSpec((HY_ORDER, 3, L, cb), lambda j, i: (0, 0, 0, j), pipeline_mode=once),
                  pl.BlockSpec((HY_ORDER, cb), lambda j, i: (0, j)),
                  pl.BlockSpec((2 * L, L), lambda j, i: (0, 0), pipeline_mode=once),
                  pl.BlockSpec((L, 2 * L), lambda j, i: (0, 0), pipeline_mode=once)],
        out_specs=pl.BlockSpec((nb, L, cb), lambda j, i: (i, 0, j)),
        out_shape=jax.ShapeDtypeStruct((bsz, L, d_hy), BF16),
        compiler_params=_params("arbitrary", "arbitrary"),
        name="hyena",
    )(h, w_in_bf, w_in_bf, w_in_bf, w_in_bf,
      conv_w.reshape(3, 3, d_hy).transpose(1, 0, 2), conv_b.reshape(3, d_hy), spec, hy_bias, f, g)


def _mlstm_kernel(*refs, nb, L, has_state, n_aliased):
    (h_ref, wq_ref, wk_ref, wv_ref, wo_ref, wz_ref, wg_ref, gb_ref, cwq_ref, cwk_ref,
     cbq_ref, cbk_ref, ng_ref) = refs[:13]
    refs = refs[13:]
    if has_state:
        c0_ref, n0_ref, m0_ref, yb_ref = refs[:4]
        refs = refs[4:]
    else:
        yb_ref, cn_ref, nn_ref, mn_ref = refs[n_aliased:n_aliased + 4]
        refs = refs[n_aliased + 4:]
    q_s, k_s, kt_s, v_s, g2_s, g2t_s, sc_s, hsum_s, c_s = refs

    T = ML_CHUNK
    nc = L // T
    R = nb * L
    d = h_ref.shape[-1]
    dh = q_s.shape[-1]
    hb = h_ref[...].reshape(R, d)
    q = _silu(_short_conv3(_dot(hb, wq_ref[...]), cwq_ref, cbq_ref[...], L))
    k = _silu(_short_conv3(_dot(hb, wk_ref[...]), cwk_ref, cbk_ref[...], L)) * (dh ** -0.5)
    q_s[...] = q.astype(BF16)
    k_s[...] = k.astype(BF16)
    kt_s[...] = k.T.astype(BF16)
    v_s[...] = _dot(hb, wv_ref[...]).astype(BF16)

    g = _dot(hb, wg_ref[...]) + gb_ref[...]
    lf = _log_sigmoid(g)
    ti = lax.broadcasted_iota(jnp.int32, (T, T), 0)
    si = lax.broadcasted_iota(jnp.int32, (T, T), 1)
    causal = si <= ti
    anti = si >= ti
    lower = causal.astype(F32)
    upper = anti.astype(F32)
    lane = lax.broadcasted_iota(jnp.int32, (T, LANES), 1)
    for j in range(nb * nc):
        rows = slice(j * T, (j + 1) * T)
        pre = _dot_hi(lower, lf[rows])
        suf = _dot_hi(upper, lf[rows])
        g2_s[rows, :] = jnp.where(lane == 1, pre, jnp.where(lane == 3, suf, g[rows]))
    g2t_s[...] = g2_s[...].T[0:8, :]

    zero11 = jnp.zeros((1, 1), F32)
    m_fin = []
    for b in range(nb):
        vals = [[None] * 6 for _ in range(nc)]
        for dr in range(2):
            m = m0_ref[b, dr][:, 0:1] if has_state else zero11
            for ci in range(nc):
                c = ci if dr == 0 else nc - 1 - ci
                cols = slice(b * L + c * T, b * L + (c + 1) * T)
                irow = g2t_s[2 * dr:2 * dr + 1, cols]
                brow = g2t_s[2 * dr + 1:2 * dr + 2, cols]
                bl = brow[:, T - 1:T] if dr == 0 else brow[:, 0:1]
                mw = jnp.max(bl - brow + irow, axis=-1, keepdims=True)
                mn = jnp.maximum(bl + m, mw)
                vals[c][3 * dr:3 * dr + 3] = [m, mn, bl]
                m = mn
            m_fin.append(m)
        for c in range(nc):
            rows8 = [jnp.broadcast_to(v, (1, LANES)) for v in vals[c]] + [jnp.zeros((2, LANES), F32)]
            sc_s[b * nc + c] = jnp.concatenate(rows8, axis=0)

    hsum_s[...] = jnp.zeros_like(hsum_s)
    for b in range(nb):
        for dr in range(2):
            c_s[2 * b + dr] = c0_ref[b, dr] if has_state else jnp.zeros((dh, dh), F32)
    if has_state:
        n_init = tuple(n0_ref[b, dr] for b in range(nb) for dr in range(2))
    else:
        n_init = tuple(jnp.zeros((1, dh), F32) for _ in range(2 * nb))

    def step(ci, n_rows):
        n_out = []
        for b in range(nb):
            for dr in range(2):
                c = ci if dr == 0 else nc - 1 - ci
                r0 = pl.multiple_of(b * L + c * T, T)
                qc = q_s[pl.ds(r0, T), :]
                kc = k_s[pl.ds(r0, T), :]
                vc = v_s[pl.ds(r0, T), :]
                ktc = kt_s[:, pl.ds(r0, T)]
                bcol = g2_s[pl.ds(r0, T), :][:, 2 * dr + 1:2 * dr + 2]
                g2tc = g2t_s[:, pl.ds(r0, T)]
                irow = g2tc[2 * dr:2 * dr + 1, :]
                brow = g2tc[2 * dr + 1:2 * dr + 2, :]
                sc = sc_s[b * nc + c]
                m_prev = sc[3 * dr:3 * dr + 1, 0:1]
                m_new = sc[3 * dr + 1:3 * dr + 2, 0:1]
                bl = sc[3 * dr + 2:3 * dr + 3, 0:1]
                n_row = n_rows[2 * b + dr]
                c_old = c_s[2 * b + dr]

                dlog = jnp.where(causal if dr == 0 else anti, bcol - brow + irow, NEG)
                inter = bcol + m_prev
                mt = jnp.maximum(inter, jnp.max(dlog, axis=-1, keepdims=True))
                smat = _dot(qc, ktc) * jnp.exp(dlog - mt)
                iw = jnp.exp(inter - mt)
                num = _dot(smat.astype(BF16), vc) + iw * _dot(qc, c_old.astype(BF16))
                qn = jnp.sum(qc.astype(F32) * n_row, axis=-1, keepdims=True)
                den = jnp.sum(smat, axis=-1, keepdims=True) + iw * qn
                hh = num / jnp.maximum(jnp.abs(den), jnp.exp(-mt))
                hsum_s[pl.ds(r0, T), :] = hsum_s[pl.ds(r0, T), :] + hh

                ws = jnp.exp(bl - brow + irow - m_new)
                dec = jnp.exp(bl + m_prev - m_new)
                kw = (ktc.astype(F32) * ws).astype(BF16)
                c_s[2 * b + dr] = dec * c_old + _dot(kw, vc)
                ws8 = jnp.broadcast_to(ws, (8, T)).astype(BF16)
                n_out.append(dec * n_row + _dot(ws8, kc)[0:1, :])
        return tuple(n_out)

    n_fin = lax.fori_loop(0, nc, step, n_init)
    if not has_state:
        for b in range(nb):
            for dr in range(2):
                cn_ref[b, dr] = c_s[2 * b + dr]
                nn_ref[b, dr] = n_fin[2 * b + dr]
                mn_ref[b, dr] = jnp.broadcast_to(m_fin[2 * b + dr], (1, LANES))

    hs = hsum_s[...]
    hm = hs * lax.rsqrt(jnp.mean(hs * hs, axis=-1, keepdims=True) + EPS) * ng_ref[...]
    uo = _dot(hb, wo_ref[...])
    uz = _dot(hb, wz_ref[...])
    yb_ref[...] = (hm * _sigmoid(uo) * _silu(uz)).astype(yb_ref.dtype).reshape(yb_ref.shape)


def _mlstm(h, l, depth, w_in_bf, wg_heads, gb_heads, conv_w, conv_b, norm_g, states, carried,
           seg_off):
    bsz, L, d = h.shape
    heads = wg_heads.shape[0]
    d_ml = norm_g.shape[-1]
    dh = d_ml // heads
    has_state = states is not None
    nb = max(1, min(bsz, ML_ROWS // L))
    nc = L // ML_CHUNK
    qo, ko, vo, oo, zo = (s // dh for s in seg_off)

    def w_spec(off):
        return pl.BlockSpec((None, d, dh), lambda b, hd, off=off: (l, 0, off + hd))

    def vec_spec(rows, off):
        return pl.BlockSpec((rows, dh), lambda b, hd, off=off: (0, off + hd))

    in_specs = [pl.BlockSpec((nb, L, d), lambda b, hd: (b, 0, 0)),
                w_spec(qo), w_spec(ko), w_spec(vo), w_spec(oo), w_spec(zo),
                pl.BlockSpec((None, d, LANES), lambda b, hd: (hd, 0, 0)),
                pl.BlockSpec((None, 1, LANES), lambda b, hd: (hd, 0, 0)),
                vec_spec(3, 0), vec_spec(3, heads), vec_spec(1, 0), vec_spec(1, heads),
                vec_spec(1, 0)]
    args = [h, w_in_bf, w_in_bf, w_in_bf, w_in_bf, w_in_bf, wg_heads, gb_heads,
            conv_w, conv_w, conv_b.reshape(1, -1), conv_b.reshape(1, -1), norm_g.reshape(1, -1)]
    yb_spec = pl.BlockSpec((nb, L, dh), lambda b, hd: (b, 0, hd))
    yb_shape = jax.ShapeDtypeStruct((bsz, L, d_ml), BF16)
    state_idx = lambda b, hd: (b, l, 0, hd, 0, 0)
    c_spec = pl.BlockSpec((nb, None, 2, None, dh, dh), state_idx)
    n_spec = pl.BlockSpec((nb, None, 2, None, 1, dh), state_idx)
    m_spec = pl.BlockSpec((nb, None, 2, None, 1, LANES), state_idx)
    aliases = {}
    if has_state:
        in_specs += [c_spec, n_spec, m_spec]
        args += list(states)
        out_specs, out_shape = yb_spec, yb_shape
    else:
        if carried is not None:
            aliases = {len(args) + i: 1 + i for i in range(3)}
            in_specs += [pl.BlockSpec(memory_space=pl.ANY)] * 3
            args += list(carried)
        out_specs = (yb_spec, c_spec, n_spec, m_spec)
        out_shape = (yb_shape,
                     jax.ShapeDtypeStruct((bsz, depth, 2, heads, dh, dh), F32),
                     jax.ShapeDtypeStruct((bsz, depth, 2, heads, 1, dh), F32),
                     jax.ShapeDtypeStruct((bsz, depth, 2, heads, 1, LANES), F32))
    rows = nb * L
    scratch = [pltpu.VMEM((rows, dh), BF16), pltpu.VMEM((rows, dh), BF16),
               pltpu.VMEM((dh, rows), BF16), pltpu.VMEM((rows, dh), BF16),
               pltpu.VMEM((rows, LANES), F32), pltpu.VMEM((8, rows), F32),
               pltpu.VMEM((nb * nc, 8, LANES), F32), pltpu.VMEM((rows, dh), F32),
               pltpu.VMEM((2 * nb, dh, dh), F32)]
    return pl.pallas_call(
        functools.partial(_mlstm_kernel, nb=nb, L=L, has_state=has_state, n_aliased=len(aliases)),
        grid=(bsz // nb, heads),
        in_specs=in_specs,
        out_specs=out_specs,
        out_shape=out_shape,
        input_output_aliases=aliases,
        scratch_shapes=scratch,
        compiler_params=_params("arbitrary", "arbitrary"),
        name="mlstm",
    )(*args)


def _out_kernel(x_ref, h_ref, ya_ref, yb_ref, mod_ref, wga_ref, wgb_ref, wpa_ref, wpb_ref,
                wout_ref, o_ref):
    hb = h_ref[...]
    ga = _sigmoid(_dot(hb, wga_ref[...]))
    gb = _sigmoid(_dot(hb, wgb_ref[...]))
    merged = ga * _dot(ya_ref[...], wpa_ref[...]) + gb * _dot(yb_ref[...], wpb_ref[...])
    o_ref[...] = x_ref[...] + mod_ref[2:3, :] * _dot(merged.astype(BF16), wout_ref[...])


def _out(x, h, ya, yb, mod, l, w_gab, w_pa, w_pb, w_out):
    bm, t, d = x.shape
    tm = min(t, OUT_TM)
    tok = lambda b, i: (b, i, 0)
    wl = lambda b, i: (l, 0, 0)
    return pl.pallas_call(
        _out_kernel,
        grid=(bm, t // tm),
        in_specs=[pl.BlockSpec((None, tm, d), tok),
                  pl.BlockSpec((None, tm, d), tok),
                  pl.BlockSpec((None, tm, d), tok),
                  pl.BlockSpec((None, tm, d), tok),
                  pl.BlockSpec((None, 3, d), lambda b, i: (b, 0, 0)),
                  pl.BlockSpec((None, d, d), wl),
                  pl.BlockSpec((None, d, d), lambda b, i: (l, 0, 1)),
                  pl.BlockSpec((None, d, d), wl),
                  pl.BlockSpec((None, d, d), wl),
                  pl.BlockSpec((None, d, d), wl)],
        out_specs=pl.BlockSpec((None, tm, d), tok),
        out_shape=jax.ShapeDtypeStruct((bm, t, d), F32),
        compiler_params=_params("arbitrary", "arbitrary"),
        name="merge_out",
    )(x, h, ya, yb, mod, w_gab, w_gab, w_pa, w_pb, w_out)


def kernel(x_prompt, x_sample, state_C, state_n, state_m, c, c_ctx, norm_g, w_ada, b_ada, w_in, hy_conv_w, hy_conv_b, hy_w1, hy_b1, hy_w2, hy_b2, hy_w3, hy_b3, hy_freq, hy_decay, hy_bias, ml_conv_w, ml_conv_b, ml_if_b, ml_norm_g, w_pa, w_pb, w_out, final_g):
    depth, d, _ = w_in.shape
    d_hy = hy_bias.shape[-1]
    d_ml = ml_norm_g.shape[-1]
    heads = ml_if_b.shape[-1]
    bp, lp, _ = x_prompt.shape
    bs, ls, _ = x_sample.shape

    off_q = 4 * d_hy
    off_v = off_q + 2 * d_ml
    off_o = off_v + d_ml
    off_z = off_o + d_ml
    off_g = off_z + d_ml
    off_ga = off_g + 4 * heads
    seg_off = (off_q, off_q + d_ml, off_v, off_o, off_z)

    w_in_bf = w_in.astype(BF16)
    w_gab = w_in[:, :, off_ga:].astype(BF16)
    wg = w_in[:, :, off_g:off_ga].reshape(depth, d, 4, heads).transpose(0, 3, 1, 2)
    wg_heads = jnp.zeros((depth, heads, d, LANES), BF16).at[..., :4].set(wg.astype(BF16))
    gb = ml_if_b.reshape(depth, 4, heads).transpose(0, 2, 1)
    gb_heads = jnp.zeros((depth, heads, 1, LANES), F32).at[:, :, 0, :4].set(gb)
    w_pa_bf, w_pb_bf, w_out_bf = (w.astype(BF16) for w in (w_pa, w_pb, w_out))

    cvecs = jnp.zeros((8, d), F32).at[0].set(c_ctx).at[1:1 + bs].set(c)
    mods = _mods(cvecs, w_ada, b_ada).reshape(depth, 8, 3, d)

    xp = x_prompt.reshape(1, bp * lp, d)
    xs = x_sample
    cached = (state_C, state_n.reshape(bs, depth, 2, heads, 1, -1),
              jnp.broadcast_to(state_m[..., None, None], (bs, depth, 2, heads, 1, LANES)))
    fin = None
    dft = {L: _dft_operands(L) for L in sorted({lp, ls})}
    for l in range(depth):
        spectra = {}
        for L in sorted({lp, ls}):
            spectra[L] = _hyena_spectra(dft[L], l, hy_w1, hy_b1, hy_w2, hy_b2, hy_w3, hy_b3,
                                        hy_freq, hy_decay)

        def layer(x, mod, bsz, L, states, carried):
            h = _norm_mod(x, norm_g[l], mod)
            hseq = h.reshape(bsz, L, d)
            ya = _hyena(hseq, l, w_in_bf, hy_conv_w[l], hy_conv_b[l], spectra[L], hy_bias[l],
                        dft[L])
            res = _mlstm(hseq, l, depth, w_in_bf, wg_heads[l], gb_heads[l], ml_conv_w[l],
                         ml_conv_b[l], ml_norm_g[l], states, carried, seg_off)
            yb, fin = (res, None) if states is not None else (res[0], res[1:])
            xn = _out(x, h, ya.reshape(x.shape), yb.reshape(x.shape), mod, l,
                      w_gab, w_pa_bf, w_pb_bf, w_out_bf)
            return xn, fin

        xp, fin = layer(xp, mods[l, 0:1], bp, lp, None, fin)
        xs, _ = layer(xs, mods[l, 1:1 + bs], bs, ls, cached, None)

    y_prompt = _final_norm(xp.reshape(bp * lp, d), final_g).reshape(bp, lp, d)
    y_sample = _final_norm(xs.reshape(bs * ls, d), final_g).reshape(bs, ls, d)
    return (y_prompt, y_sample, fin[0], fin[1][:, :, :, :, 0, :], fin[2][:, :, :, :, 0, 0])
```

```python
import functools
import math

import numpy as np
import jax
import jax.numpy as jnp
from jax import lax
from jax.experimental import pallas as pl
from jax.experimental.pallas import tpu as pltpu

F32 = jnp.float32
BF16 = jnp.bfloat16
HIGHEST = lax.Precision.HIGHEST

HY_ORDER = 2
HY_BANDS = 16
HY_SHIFT = 0.05
ML_CHUNK = 128
EPS = 1e-6
NEG = -1e30

LANES = 128
VMEM_LIMIT = 56 * 1024 * 1024
HY_CB = 256
ML_ROWS = 1024
HY_ROWS = 2048
OUT_TM = 512


def _dot(a, b):
    return jnp.dot(a, b, preferred_element_type=F32)


def _dot_hi(a, b):
    return jnp.dot(a, b, preferred_element_type=F32, precision=HIGHEST)


def _sigmoid(x):
    return 1.0 / (1.0 + jnp.exp(-x))


def _silu(x):
    return x * _sigmoid(x)


def _log_sigmoid(x):
    return jnp.minimum(x, 0.0) - jnp.log(1.0 + jnp.exp(-jnp.abs(x)))


def _params(*sem):
    return pltpu.CompilerParams(dimension_semantics=sem, vmem_limit_bytes=VMEM_LIMIT)


@functools.lru_cache(maxsize=None)
def _dft_table(L):
    N = 2 * L
    k = np.arange(L, dtype=np.int64)[:, None]
    n = np.arange(L, dtype=np.int64)[None, :]
    ang = 2.0 * np.pi * ((k * n) % N).astype(np.float64) / N
    c = np.cos(ang)
    s = -np.sin(ang)
    s[0, :] = 1.0 - 2.0 * (np.arange(L) % 2)
    return np.concatenate([c, s], axis=0).astype(np.float32)


def _dft_operands(L):
    f = jnp.asarray(_dft_table(L))
    f_hi = f.astype(BF16)
    f_lo = (f - f_hi.astype(F32)).astype(BF16)
    return f_hi, f_lo, f_hi.T


def _mod_kernel(c_ref, w_ref, b_ref, o_ref):
    o_ref[...] = _dot_hi(_silu(c_ref[...]), w_ref[...]) + b_ref[...]


def _mods(cvecs, w_ada, b_ada):
    depth, d, d3 = w_ada.shape
    r = cvecs.shape[0]
    tn = 512
    return pl.pallas_call(
        _mod_kernel,
        grid=(depth, d3 // tn),
        in_specs=[pl.BlockSpec((r, d), lambda l, j: (0, 0)),
                  pl.BlockSpec((None, d, tn), lambda l, j: (l, 0, j)),
                  pl.BlockSpec((None, 1, tn), lambda l, j: (l, 0, j))],
        out_specs=pl.BlockSpec((None, r, tn), lambda l, j: (l, 0, j)),
        out_shape=jax.ShapeDtypeStruct((depth, r, d3), F32),
        compiler_params=_params("arbitrary", "arbitrary"),
        name="adaln_mod",
    )(cvecs, w_ada, b_ada.reshape(depth, 1, d3))


def _norm_mod_kernel(x_ref, g_ref, mod_ref, o_ref):
    x = x_ref[...]
    y = x * lax.rsqrt(jnp.mean(x * x, axis=-1, keepdims=True) + EPS) * g_ref[...]
    o_ref[...] = (y * (1.0 + mod_ref[1:2, :]) + mod_ref[0:1, :]).astype(o_ref.dtype)


def _norm_mod(x, g, mod):
    bm, t, d = x.shape
    tm = min(t, 1024)
    return pl.pallas_call(
        _norm_mod_kernel,
        grid=(bm, t // tm),
        in_specs=[pl.BlockSpec((None, tm, d), lambda b, i: (b, i, 0)),
                  pl.BlockSpec((1, d), lambda b, i: (0, 0)),
                  pl.BlockSpec((None, 3, d), lambda b, i: (b, 0, 0))],
        out_specs=pl.BlockSpec((None, tm, d), lambda b, i: (b, i, 0)),
        out_shape=jax.ShapeDtypeStruct((bm, t, d), BF16),
        compiler_params=_params("arbitrary", "arbitrary"),
        name="norm_mod",
    )(x, g.reshape(1, d), mod)


def _dot3(f_hi, f_lo, a):
    a_hi = a.astype(BF16)
    a_lo = (a - a_hi.astype(F32)).astype(BF16)
    return _dot(f_hi, a_hi) + (_dot(f_lo, a_hi) + _dot(f_hi, a_lo))


def _filter_kernel(w1_ref, b1_ref, w2_ref, b2_ref, freq_ref, w3_00, w3_01, w3_10, w3_11,
                   b3_ref, decay_ref, fhi_ref, flo_ref, spec_ref, hdn_ref, *, L):
    cb = spec_ref.shape[-1]

    @pl.when(pl.program_id(0) == 0)
    def _():
        row = lax.broadcasted_iota(jnp.int32, (L, LANES), 0).astype(F32)
        lane = lax.broadcasted_iota(jnp.int32, (L, LANES), 1)
        t = row / L
        band = jnp.where(lane <= HY_BANDS, lane, lane - HY_BANDS).astype(F32)
        ang = 2.0 * math.pi * t * band
        feats = jnp.where(lane == 0, t,
                          jnp.where(lane <= HY_BANDS, jnp.cos(ang),
                                    jnp.where(lane <= 2 * HY_BANDS, jnp.sin(ang), 0.0)))
        hdn = jnp.sin(freq_ref[0:1, :] * (_dot_hi(feats, w1_ref[...]) + b1_ref[...]))
        hdn_ref[...] = jnp.sin(freq_ref[1:2, :] * (_dot_hi(hdn, w2_ref[...]) + b2_ref[...]))

    hdn = hdn_ref[...]
    row_i = lax.broadcasted_iota(jnp.int32, (L, cb), 0)
    t = row_i.astype(F32) / L
    first = row_i == 0
    sign = jnp.where((row_i & 1) == 0, 1.0, -1.0)
    w3 = ((w3_00, w3_01), (w3_10, w3_11))
    for o in range(HY_ORDER):
        hs = []
        for dr in range(2):
            j = 2 * o + dr
            hv = _dot_hi(hdn, w3[o][dr][...]) + b3_ref[j:j + 1, :]
            win = jnp.exp(-t * jnp.abs(decay_ref[j:j + 1, :])) + HY_SHIFT
            hs.append(hv * win)
        hf = hs[0]
        hb = jnp.where(first, 0.0, hs[1])
        l1 = jnp.sum(jnp.abs(hf), axis=0, keepdims=True) + jnp.sum(jnp.abs(hb), axis=0, keepdims=True)
        hf = hf / l1
        hb = hb / l1
        even = hf + hb
        odd = hf - hb
        re = _dot3(fhi_ref[0:L, :], flo_ref[0:L, :], even)
        im = _dot3(fhi_ref[L:2 * L, :], flo_ref[L:2 * L, :], odd)
        nyq = jnp.sum(even * sign, axis=0, keepdims=True)
        a = re * jnp.where(first, 0.5 / L, 1.0 / L)
        spec_ref[o, 0] = a
        spec_ref[o, 1] = jnp.where(first, 0.0, im * (1.0 / L))
        spec_ref[o, 2] = jnp.where(first, nyq * (0.5 / L), a)


def _hyena_spectra(dft, l, hy_w1, hy_b1, hy_w2, hy_b2, hy_w3, hy_b3, hy_freq, hy_decay):
    emb, ff = hy_w1.shape[1:]
    d_hy = hy_decay.shape[-1]
    cb = HY_CB
    ncb = d_hy // cb
    w1p = jnp.zeros((LANES, ff), F32).at[:emb].set(hy_w1[l])
    f_hi, f_lo, _ = dft
    L = f_hi.shape[1]
    const = lambda j: (0, 0)

    def w3_spec(o, dr):
        return pl.BlockSpec((None, ff, cb), lambda j, o=o, dr=dr: (l, 0, (2 * o + dr) * ncb + j))

    return pl.pallas_call(
        functools.partial(_filter_kernel, L=L),
        grid=(ncb,),
        in_specs=[pl.BlockSpec((LANES, ff), const),
                  pl.BlockSpec((1, ff), const),
                  pl.BlockSpec((ff, ff), const),
                  pl.BlockSpec((1, ff), const),
                  pl.BlockSpec((2, ff), const),
                  w3_spec(0, 0), w3_spec(0, 1), w3_spec(1, 0), w3_spec(1, 1),
                  pl.BlockSpec((2 * HY_ORDER, cb), lambda j: (0, j)),
                  pl.BlockSpec((2 * HY_ORDER, cb), lambda j: (0, j)),
                  pl.BlockSpec((2 * L, L), const),
                  pl.BlockSpec((2 * L, L), const)],
        out_specs=pl.BlockSpec((HY_ORDER, 3, L, cb), lambda j: (0, 0, 0, j)),
        out_shape=jax.ShapeDtypeStruct((HY_ORDER, 3, L, d_hy), F32),
        scratch_shapes=[pltpu.VMEM((L, ff), F32)],
        compiler_params=_params("arbitrary"),
        name="hyena_filter",
    )(w1p, hy_b1[l].reshape(1, ff), hy_w2[l], hy_b2[l].reshape(1, ff), hy_freq[l],
      hy_w3, hy_w3, hy_w3, hy_w3,
      hy_b3[l].reshape(2 * HY_ORDER, d_hy), hy_decay[l].reshape(2 * HY_ORDER, d_hy),
      f_hi, f_lo)


def _short_conv3(x, w_ref, b, seq=None):
    n = x.shape[0]
    seq = n if seq is None else seq
    row = lax.broadcasted_iota(jnp.int32, x.shape, 0)
    first = functools.reduce(jnp.logical_or, [row == s for s in range(0, n, seq)])
    last = functools.reduce(jnp.logical_or, [row == s + seq - 1 for s in range(0, n, seq)])
    prev = jnp.where(first, 0.0, pltpu.roll(x, 1, axis=0))
    nxt = jnp.where(last, 0.0, pltpu.roll(x, n - 1, axis=0))
    return ((b + prev * w_ref[0:1, :]) + x * w_ref[1:2, :]) + nxt * w_ref[2:3, :]


def _hyena_kernel(h_ref, wx1_ref, wx2_ref, wv_ref, wz_ref, cw_ref, cb_ref, spec_ref, hbias_ref,
                  f_ref, g_ref, o_ref, *, nb, L):
    seqs = range(nb)
    wv = wv_ref[...].astype(BF16)
    z = [_short_conv3(_dot(h_ref[b], wv), cw_ref.at[2], cb_ref[2:3, :]) for b in seqs]
    gate_w = (wx1_ref, wx2_ref)
    for o in range(HY_ORDER):
        zf = [_dot(f_ref[...], z[b].astype(BF16)) for b in seqs]
        wg = gate_w[o][...].astype(BF16)
        gate = [_short_conv3(_dot(h_ref[b], wg), cw_ref.at[o], cb_ref[o:o + 1, :]) for b in seqs]
        ycat = []
        for b in seqs:
            zc, zs = zf[b][0:L], zf[b][L:2 * L]
            a, bm, dm = spec_ref[o, 0], spec_ref[o, 1], spec_ref[o, 2]
            ycat.append(jnp.concatenate([zc * a - zs * bm, zc * bm + zs * dm], axis=0).astype(BF16))
        y = [_dot(g_ref[...], ycat[b]) for b in seqs]
        z = [gate[b] * (y[b] + hbias_ref[o:o + 1, :] * z[b]) for b in seqs]
    wz = wz_ref[...].astype(BF16)
    for b in seqs:
        o_ref[b] = (z[b] * _silu(_dot(h_ref[b], wz))).astype(o_ref.dtype)


def _hyena(h, l, w_in, conv_w, conv_b, spec, hy_bias, dft):
    bsz, L, d = h.shape
    d_hy = hy_bias.shape[-1]
    cb = HY_CB
    ncb = d_hy // cb
    nb = max(1, min(bsz, HY_ROWS // L))
    f, _, g = dft
    once = pl.Buffered(1)

    def w_spec(seg):
        return pl.BlockSpec((None, d, cb), lambda j, i, seg=seg: (l, 0, seg * ncb + j))

    return pl.pallas_call(
        functools.partial(_hyena_kernel, nb=nb, L=L),
        grid=(ncb, bsz // nb),
        in_specs=[pl.BlockSpec((nb, L, d), lambda j, i: (i, 0, 0)),
                  w_spec(0), w_spec(1), w_spec(2), w_spec(3),
                  pl.BlockSpec((3, 3, cb), lambda j, i: (0, 0, j)),
                  pl.BlockSpec((3, cb), lambda j, i: (0, j)),
                  pl.BlockSpec((HY_ORDER, 3, L, cb), lambda j, i: (0, 0, 0, j), pipeline_mode=once),
                  pl.BlockSpec((HY_ORDER, cb), lambda j, i: (0, j)),
                  pl.BlockSpec((2 * L, L), lambda j, i: (0, 0), pipeline_mode=once),
                  pl.BlockSpec((L, 2 * L), lambda j, i: (0, 0), pipeline_mode=once)],
        out_specs=pl.BlockSpec((nb, L, cb), lambda j, i: (i, 0, j)),
        out_shape=jax.ShapeDtypeStruct((bsz, L, d_hy), BF16),
        compiler_params=_params("arbitrary", "arbitrary"),
        name="hyena",
    )(h, w_in, w_in, w_in, w_in,
      conv_w.reshape(3, 3, d_hy).transpose(1, 0, 2), conv_b.reshape(3, d_hy), spec, hy_bias, f, g)


def _mlstm_kernel(*refs, nb, L, has_state, n_aliased):
    (h_ref, wq_ref, wk_ref, wv_ref, wo_ref, wz_ref, wg_ref, gb_ref, cwq_ref, cwk_ref,
     cbq_ref, cbk_ref, ng_ref) = refs[:13]
    refs = refs[13:]
    if has_state:
        c0_ref, n0_ref, m0_ref, yb_ref = refs[:4]
        refs = refs[4:]
    else:
        yb_ref, cn_ref, nn_ref, mn_ref = refs[n_aliased:n_aliased + 4]
        refs = refs[n_aliased + 4:]
    q_s, k_s, kt_s, v_s, g2_s, g2t_s, sc_s, hsum_s, c_s = refs

    T = ML_CHUNK
    nc = L // T
    R = nb * L
    d = h_ref.shape[-1]
    dh = q_s.shape[-1]
    hb = h_ref[...].reshape(R, d)
    q = _silu(_short_conv3(_dot(hb, wq_ref[...].astype(BF16)), cwq_ref, cbq_ref[...], L))
    k = _silu(_short_conv3(_dot(hb, wk_ref[...].astype(BF16)), cwk_ref, cbk_ref[...], L)) * (dh ** -0.5)
    q_s[...] = q.astype(BF16)
    k_s[...] = k.astype(BF16)
    kt_s[...] = k.T.astype(BF16)
    v_s[...] = _dot(hb, wv_ref[...].astype(BF16)).astype(BF16)

    g = _dot(hb, wg_ref[...]) + gb_ref[...]
    lf = _log_sigmoid(g)
    ti = lax.broadcasted_iota(jnp.int32, (T, T), 0)
    si = lax.broadcasted_iota(jnp.int32, (T, T), 1)
    causal = si <= ti
    anti = si >= ti
    lower = causal.astype(F32)
    upper = anti.astype(F32)
    lane = lax.broadcasted_iota(jnp.int32, (T, LANES), 1)
    for j in range(nb * nc):
        rows = slice(j * T, (j + 1) * T)
        pre = _dot_hi(lower, lf[rows])
        suf = _dot_hi(upper, lf[rows])
        g2_s[rows, :] = jnp.where(lane == 1, pre, jnp.where(lane == 3, suf, g[rows]))
    g2t_s[...] = g2_s[...].T[0:8, :]

    zero11 = jnp.zeros((1, 1), F32)
    m_fin = []
    for b in range(nb):
        vals = [[None] * 6 for _ in range(nc)]
        for dr in range(2):
            m = m0_ref[b, dr][:, 0:1] if has_state else zero11
            for ci in range(nc):
                c = ci if dr == 0 else nc - 1 - ci
                cols = slice(b * L + c * T, b * L + (c + 1) * T)
                irow = g2t_s[2 * dr:2 * dr + 1, cols]
                brow = g2t_s[2 * dr + 1:2 * dr + 2, cols]
                bl = brow[:, T - 1:T] if dr == 0 else brow[:, 0:1]
                mw = jnp.max(bl - brow + irow, axis=-1, keepdims=True)
                mn = jnp.maximum(bl + m, mw)
                vals[c][3 * dr:3 * dr + 3] = [m, mn, bl]
                m = mn
            m_fin.append(m)
        for c in range(nc):
            rows8 = [jnp.broadcast_to(v, (1, LANES)) for v in vals[c]] + [jnp.zeros((2, LANES), F32)]
            sc_s[b * nc + c] = jnp.concatenate(rows8, axis=0)

    hsum_s[...] = jnp.zeros_like(hsum_s)
    for b in range(nb):
        for dr in range(2):
            c_s[2 * b + dr] = c0_ref[b, dr] if has_state else jnp.zeros((dh, dh), F32)
    if has_state:
        n_init = tuple(n0_ref[b, dr] for b in range(nb) for dr in range(2))
    else:
        n_init = tuple(jnp.zeros((1, dh), F32) for _ in range(2 * nb))

    def step(ci, n_rows):
        n_out = []
        for b in range(nb):
            for dr in range(2):
                c = ci if dr == 0 else nc - 1 - ci
                r0 = pl.multiple_of(b * L + c * T, T)
                qc = q_s[pl.ds(r0, T), :]
                kc = k_s[pl.ds(r0, T), :]
                vc = v_s[pl.ds(r0, T), :]
                ktc = kt_s[:, pl.ds(r0, T)]
                bcol = g2_s[pl.ds(r0, T), :][:, 2 * dr + 1:2 * dr + 2]
                g2tc = g2t_s[:, pl.ds(r0, T)]
                irow = g2tc[2 * dr:2 * dr + 1, :]
                brow = g2tc[2 * dr + 1:2 * dr + 2, :]
                sc = sc_s[b * nc + c]
                m_prev = sc[3 * dr:3 * dr + 1, 0:1]
                m_new = sc[3 * dr + 1:3 * dr + 2, 0:1]
                bl = sc[3 * dr + 2:3 * dr + 3, 0:1]
                n_row = n_rows[2 * b + dr]
                c_old = c_s[2 * b + dr]

                dlog = jnp.where(causal if dr == 0 else anti, bcol - brow + irow, NEG)
                inter = bcol + m_prev
                mt = jnp.maximum(inter, jnp.max(dlog, axis=-1, keepdims=True))
                smat = _dot(qc, ktc) * jnp.exp(dlog - mt)
                iw = jnp.exp(inter - mt)
                num = _dot(smat.astype(BF16), vc) + iw * _dot(qc, c_old.astype(BF16))
                qn = jnp.sum(qc.astype(F32) * n_row, axis=-1, keepdims=True)
                den = jnp.sum(smat, axis=-1, keepdims=True) + iw * qn
                hh = num / jnp.maximum(jnp.abs(den), jnp.exp(-mt))
                hsum_s[pl.ds(r0, T), :] = hsum_s[pl.ds(r0, T), :] + hh

                ws = jnp.exp(bl - brow + irow - m_new)
                dec = jnp.exp(bl + m_prev - m_new)
                kw = (ktc.astype(F32) * ws).astype(BF16)
                c_s[2 * b + dr] = dec * c_old + _dot(kw, vc)
                ws8 = jnp.broadcast_to(ws, (8, T)).astype(BF16)
                n_out.append(dec * n_row + _dot(ws8, kc)[0:1, :])
        return tuple(n_out)

    n_fin = lax.fori_loop(0, nc, step, n_init)
    if not has_state:
        for b in range(nb):
            for dr in range(2):
                cn_ref[b, dr] = c_s[2 * b + dr]
                nn_ref[b, dr] = n_fin[2 * b + dr]
                mn_ref[b, dr] = jnp.broadcast_to(m_fin[2 * b + dr], (1, LANES))

    hs = hsum_s[...]
    hm = hs * lax.rsqrt(jnp.mean(hs * hs, axis=-1, keepdims=True) + EPS) * ng_ref[...]
    uo = _dot(hb, wo_ref[...].astype(BF16))
    uz = _dot(hb, wz_ref[...].astype(BF16))
    yb_ref[...] = (hm * _sigmoid(uo) * _silu(uz)).astype(yb_ref.dtype).reshape(yb_ref.shape)


def _mlstm(h, l, depth, w_in, wg_heads, gb_heads, conv_w, conv_b, norm_g, states, carried,
           seg_off):
    bsz, L, d = h.shape
    heads = wg_heads.shape[0]
    d_ml = norm_g.shape[-1]
    dh = d_ml // heads
    has_state = states is not None
    nb = max(1, min(bsz, ML_ROWS // L))
    nc = L // ML_CHUNK
    qo, ko, vo, oo, zo = (s // dh for s in seg_off)

    def w_spec(off):
        return pl.BlockSpec((None, d, dh), lambda b, hd, off=off: (l, 0, off + hd))

    def vec_spec(rows, off):
        return pl.BlockSpec((rows, dh), lambda b, hd, off=off: (0, off + hd))

    in_specs = [pl.BlockSpec((nb, L, d), lambda b, hd: (b, 0, 0)),
                w_spec(qo), w_spec(ko), w_spec(vo), w_spec(oo), w_spec(zo),
                pl.BlockSpec((None, d, LANES), lambda b, hd: (hd, 0, 0)),
                pl.BlockSpec((None, 1, LANES), lambda b, hd: (hd, 0, 0)),
                vec_spec(3, 0), vec_spec(3, heads), vec_spec(1, 0), vec_spec(1, heads),
                vec_spec(1, 0)]
    args = [h, w_in, w_in, w_in, w_in, w_in, wg_heads, gb_heads,
            conv_w, conv_w, conv_b.reshape(1, -1), conv_b.reshape(1, -1), norm_g.reshape(1, -1)]
    yb_spec = pl.BlockSpec((nb, L, dh), lambda b, hd: (b, 0, hd))
    yb_shape = jax.ShapeDtypeStruct((bsz, L, d_ml), BF16)
    state_idx = lambda b, hd: (b, l, 0, hd, 0, 0)
    c_spec = pl.BlockSpec((nb, None, 2, None, dh, dh), state_idx)
    n_spec = pl.BlockSpec((nb, None, 2, None, 1, dh), state_idx)
    m_spec = pl.BlockSpec((nb, None, 2, None, 1, LANES), state_idx)
    aliases = {}
    if has_state:
        in_specs += [c_spec, n_spec, m_spec]
        args += list(states)
        out_specs, out_shape = yb_spec, yb_shape
    else:
        if carried is not None:
            aliases = {len(args) + i: 1 + i for i in range(3)}
            in_specs += [pl.BlockSpec(memory_space=pl.ANY)] * 3
            args += list(carried)
        out_specs = (yb_spec, c_spec, n_spec, m_spec)
        out_shape = (yb_shape,
                     jax.ShapeDtypeStruct((bsz, depth, 2, heads, dh, dh), F32),
                     jax.ShapeDtypeStruct((bsz, depth, 2, heads, 1, dh), F32),
                     jax.ShapeDtypeStruct((bsz, depth, 2, heads, 1, LANES), F32))
    rows = nb * L
    scratch = [pltpu.VMEM((rows, dh), BF16), pltpu.VMEM((rows, dh), BF16),
               pltpu.VMEM((dh, rows), BF16), pltpu.VMEM((rows, dh), BF16),
               pltpu.VMEM((rows, LANES), F32), pltpu.VMEM((8, rows), F32),
               pltpu.VMEM((nb * nc, 8, LANES), F32), pltpu.VMEM((rows, dh), F32),
               pltpu.VMEM((2 * nb, dh, dh), F32)]
    return pl.pallas_call(
        functools.partial(_mlstm_kernel, nb=nb, L=L, has_state=has_state, n_aliased=len(aliases)),
        grid=(bsz // nb, heads),
        in_specs=in_specs,
        out_specs=out_specs,
        out_shape=out_shape,
        input_output_aliases=aliases,
        scratch_shapes=scratch,
        compiler_params=_params("arbitrary", "arbitrary"),
        name="mlstm",
    )(*args)


def _out_kernel(x_ref, h_ref, ya_ref, yb_ref, mod_ref, wga_ref, wgb_ref, wpa_ref, wpb_ref,
                wout_ref, g_ref, *rest, last):
    hb = h_ref[...]
    ga = _sigmoid(_dot(hb, wga_ref[...]))
    gb = _sigmoid(_dot(hb, wgb_ref[...]))
    merged = ga * _dot(ya_ref[...], wpa_ref[...]) + gb * _dot(yb_ref[...], wpb_ref[...])
    xn = x_ref[...] + mod_ref[2:3, :] * _dot(merged.astype(BF16), wout_ref[...])
    normed = xn * lax.rsqrt(jnp.mean(xn * xn, axis=-1, keepdims=True) + EPS) * g_ref[...]
    if last:
        (y_ref,) = rest
        y_ref[...] = normed
    else:
        modn_ref, xo_ref, ho_ref = rest
        xo_ref[...] = xn
        ho_ref[...] = (normed * (1.0 + modn_ref[1:2, :]) + modn_ref[0:1, :]).astype(ho_ref.dtype)


def _out(x, h, ya, yb, mod, l, w_gab, w_pa, w_pb, w_out, g_next, mod_next):
    bm, t, d = x.shape
    tm = min(t, OUT_TM)
    last = mod_next is None
    tok = lambda b, i: (b, i, 0)
    wl = lambda b, i: (l, 0, 0)
    tok_spec = pl.BlockSpec((None, tm, d), tok)
    mod_spec = pl.BlockSpec((None, 3, d), lambda b, i: (b, 0, 0))
    in_specs = [tok_spec, tok_spec, tok_spec, tok_spec, mod_spec,
                pl.BlockSpec((None, d, d), wl),
                pl.BlockSpec((None, d, d), lambda b, i: (l, 0, 1)),
                pl.BlockSpec((None, d, d), wl),
                pl.BlockSpec((None, d, d), wl),
                pl.BlockSpec((None, d, d), wl),
                pl.BlockSpec((1, d), lambda b, i: (0, 0))]
    args = [x, h, ya, yb, mod, w_gab, w_gab, w_pa, w_pb, w_out, g_next.reshape(1, d)]
    if last:
        out_specs, out_shape = tok_spec, jax.ShapeDtypeStruct((bm, t, d), F32)
    else:
        in_specs.append(mod_spec)
        args.append(mod_next)
        out_specs = (tok_spec, tok_spec)
        out_shape = (jax.ShapeDtypeStruct((bm, t, d), F32), jax.ShapeDtypeStruct((bm, t, d), BF16))
    return pl.pallas_call(
        functools.partial(_out_kernel, last=last),
        grid=(bm, t // tm),
        in_specs=in_specs,
        out_specs=out_specs,
        out_shape=out_shape,
        compiler_params=_params("arbitrary", "arbitrary"),
        name="merge_out",
    )(*args)


def kernel(x_prompt, x_sample, state_C, state_n, state_m, c, c_ctx, norm_g, w_ada, b_ada, w_in, hy_conv_w, hy_conv_b, hy_w1, hy_b1, hy_w2, hy_b2, hy_w3, hy_b3, hy_freq, hy_decay, hy_bias, ml_conv_w, ml_conv_b, ml_if_b, ml_norm_g, w_pa, w_pb, w_out, final_g):
    depth, d, _ = w_in.shape
    d_hy = hy_bias.shape[-1]
    d_ml = ml_norm_g.shape[-1]
    heads = ml_if_b.shape[-1]
    bp, lp, _ = x_prompt.shape
    bs, ls, _ = x_sample.shape

    off_q = 4 * d_hy
    off_v = off_q + 2 * d_ml
    off_o = off_v + d_ml
    off_z = off_o + d_ml
    off_g = off_z + d_ml
    off_ga = off_g + 4 * heads
    seg_off = (off_q, off_q + d_ml, off_v, off_o, off_z)

    w_gab = w_in[:, :, off_ga:].astype(BF16)
    wg = w_in[:, :, off_g:off_ga].reshape(depth, d, 4, heads).transpose(0, 3, 1, 2)
    wg_heads = jnp.zeros((depth, heads, d, LANES), BF16).at[..., :4].set(wg.astype(BF16))
    gb = ml_if_b.reshape(depth, 4, heads).transpose(0, 2, 1)
    gb_heads = jnp.zeros((depth, heads, 1, LANES), F32).at[:, :, 0, :4].set(gb)
    w_pa_bf, w_pb_bf, w_out_bf = (w.astype(BF16) for w in (w_pa, w_pb, w_out))

    cvecs = jnp.zeros((8, d), F32).at[0].set(c_ctx).at[1:1 + bs].set(c)
    mods = _mods(cvecs, w_ada, b_ada).reshape(depth, 8, 3, d)

    xp = x_prompt.reshape(1, bp * lp, d)
    xs = x_sample
    cached = (state_C, state_n.reshape(bs, depth, 2, heads, 1, -1),
              jnp.broadcast_to(state_m[..., None, None], (bs, depth, 2, heads, 1, LANES)))
    fin = None
    hp = _norm_mod(xp, norm_g[0], mods[0, 0:1])
    hs = _norm_mod(xs, norm_g[0], mods[0, 1:1 + bs])
    dft = {L: _dft_operands(L) for L in sorted({lp, ls})}
    for l in range(depth):
        spectra = {}
        for L in sorted({lp, ls}):
            spectra[L] = _hyena_spectra(dft[L], l, hy_w1, hy_b1, hy_w2, hy_b2, hy_w3, hy_b3,
                                        hy_freq, hy_decay)

        last = l == depth - 1
        g_next = final_g if last else norm_g[l + 1]

        def layer(x, h, rows, bsz, L, states, carried):
            mod = mods[l, rows]
            hseq = h.reshape(bsz, L, d)
            ya = _hyena(hseq, l, w_in, hy_conv_w[l], hy_conv_b[l], spectra[L], hy_bias[l], dft[L])
            res = _mlstm(hseq, l, depth, w_in, wg_heads[l], gb_heads[l], ml_conv_w[l],
                         ml_conv_b[l], ml_norm_g[l], states, carried, seg_off)
            yb, fin = (res, None) if states is not None else (res[0], res[1:])
            res = _out(x, h, ya.reshape(x.shape), yb.reshape(x.shape), mod, l,
                       w_gab, w_pa_bf, w_pb_bf, w_out_bf, g_next,
                       None if last else mods[l + 1, rows])
            return (res, None, fin) if last else (res[0], res[1], fin)

        xp, hp, fin = layer(xp, hp, slice(0, 1), bp, lp, None, fin)
        xs, hs, _ = layer(xs, hs, slice(1, 1 + bs), bs, ls, cached, None)

    return (xp.reshape(bp, lp, d), xs, fin[0], fin[1][:, :, :, :, 0, :], fin[2][:, :, :, :, 0, 0])
```

```python
import functools
import math

import numpy as np
import jax
import jax.numpy as jnp
from jax import lax
from jax.experimental import pallas as pl
from jax.experimental.pallas import tpu as pltpu

F32 = jnp.float32
BF16 = jnp.bfloat16
HIGHEST = lax.Precision.HIGHEST

HY_ORDER = 2
HY_BANDS = 16
HY_SHIFT = 0.05
ML_CHUNK = 128
EPS = 1e-6
NEG = -1e30

LANES = 128
VMEM_LIMIT = 56 * 1024 * 1024
HY_CB = 256
ML_ROWS = 1024
HY_ROWS = 2048
ML_STREAMS = 8
OUT_TM = 512


def _dot(a, b):
    return jnp.dot(a, b, preferred_element_type=F32)


def _dot_nt(a, bt):
    return lax.dot_general(a, bt, (((1,), (1,)), ((), ())), preferred_element_type=F32)


def _dot_hi(a, b):
    return jnp.dot(a, b, preferred_element_type=F32, precision=HIGHEST)


def _sigmoid(x):
    return 1.0 / (1.0 + jnp.exp(-x))


def _silu(x):
    return x * _sigmoid(x)


def _log_sigmoid(x):
    return jnp.minimum(x, 0.0) - jnp.log(1.0 + jnp.exp(-jnp.abs(x)))


def _params(*sem):
    return pltpu.CompilerParams(dimension_semantics=sem, vmem_limit_bytes=VMEM_LIMIT)


@functools.lru_cache(maxsize=None)
def _dft_table(L):
    N = 2 * L
    k = np.arange(L, dtype=np.int64)[:, None]
    n = np.arange(L, dtype=np.int64)[None, :]
    ang = 2.0 * np.pi * ((k * n) % N).astype(np.float64) / N
    c = np.cos(ang)
    s = -np.sin(ang)
    s[0, :] = 1.0 - 2.0 * (np.arange(L) % 2)
    return np.concatenate([c, s], axis=0).astype(np.float32)


def _dft_operands(L):
    f = jnp.asarray(_dft_table(L))
    f_hi = f.astype(BF16)
    f_lo = (f - f_hi.astype(F32)).astype(BF16)
    return f_hi, f_lo, f_hi.T


def _mod_kernel(c_ref, w_ref, b_ref, o_ref):
    o_ref[...] = _dot_hi(_silu(c_ref[...]), w_ref[...]) + b_ref[...]


def _mods(cvecs, w_ada, b_ada):
    depth, d, d3 = w_ada.shape
    r = cvecs.shape[0]
    tn = 512
    return pl.pallas_call(
        _mod_kernel,
        grid=(depth, d3 // tn),
        in_specs=[pl.BlockSpec((r, d), lambda l, j: (0, 0)),
                  pl.BlockSpec((None, d, tn), lambda l, j: (l, 0, j)),
                  pl.BlockSpec((None, 1, tn), lambda l, j: (l, 0, j))],
        out_specs=pl.BlockSpec((None, r, tn), lambda l, j: (l, 0, j)),
        out_shape=jax.ShapeDtypeStruct((depth, r, d3), F32),
        compiler_params=_params("arbitrary", "arbitrary"),
        name="adaln_mod",
    )(cvecs, w_ada, b_ada.reshape(depth, 1, d3))


def _norm_mod_kernel(x_ref, g_ref, mod_ref, o_ref):
    x = x_ref[...]
    y = x * lax.rsqrt(jnp.mean(x * x, axis=-1, keepdims=True) + EPS) * g_ref[...]
    o_ref[...] = (y * (1.0 + mod_ref[1:2, :]) + mod_ref[0:1, :]).astype(o_ref.dtype)


def _norm_mod(x, g, mod):
    bm, t, d = x.shape
    tm = min(t, 1024)
    return pl.pallas_call(
        _norm_mod_kernel,
        grid=(bm, t // tm),
        in_specs=[pl.BlockSpec((None, tm, d), lambda b, i: (b, i, 0)),
                  pl.BlockSpec((1, d), lambda b, i: (0, 0)),
                  pl.BlockSpec((None, 3, d), lambda b, i: (b, 0, 0))],
        out_specs=pl.BlockSpec((None, tm, d), lambda b, i: (b, i, 0)),
        out_shape=jax.ShapeDtypeStruct((bm, t, d), BF16),
        compiler_params=_params("arbitrary", "arbitrary"),
        name="norm_mod",
    )(x, g.reshape(1, d), mod)


def _dot3(f_hi, f_lo, a):
    a_hi = a.astype(BF16)
    a_lo = (a - a_hi.astype(F32)).astype(BF16)
    return _dot(f_hi, a_hi) + (_dot(f_lo, a_hi) + _dot(f_hi, a_lo))


def _filter_kernel(w1_ref, b1_ref, w2_ref, b2_ref, freq_ref, w3_00, w3_01, w3_10, w3_11,
                   b3_ref, decay_ref, fhi_ref, flo_ref, spec_ref, hdn_ref, *, L):
    cb = spec_ref.shape[-1]

    @pl.when(pl.program_id(0) == 0)
    def _():
        row = lax.broadcasted_iota(jnp.int32, (L, LANES), 0).astype(F32)
        lane = lax.broadcasted_iota(jnp.int32, (L, LANES), 1)
        t = row / L
        band = jnp.where(lane <= HY_BANDS, lane, lane - HY_BANDS).astype(F32)
        ang = 2.0 * math.pi * t * band
        feats = jnp.where(lane == 0, t,
                          jnp.where(lane <= HY_BANDS, jnp.cos(ang),
                                    jnp.where(lane <= 2 * HY_BANDS, jnp.sin(ang), 0.0)))
        hdn = jnp.sin(freq_ref[0:1, :] * (_dot_hi(feats, w1_ref[...]) + b1_ref[...]))
        hdn_ref[...] = jnp.sin(freq_ref[1:2, :] * (_dot_hi(hdn, w2_ref[...]) + b2_ref[...]))

    hdn = hdn_ref[...]
    row_i = lax.broadcasted_iota(jnp.int32, (L, cb), 0)
    t = row_i.astype(F32) / L
    first = row_i == 0
    sign = jnp.where((row_i & 1) == 0, 1.0, -1.0)
    w3 = ((w3_00, w3_01), (w3_10, w3_11))
    for o in range(HY_ORDER):
        hs = []
        for dr in range(2):
            j = 2 * o + dr
            hv = _dot_hi(hdn, w3[o][dr][...]) + b3_ref[j:j + 1, :]
            win = jnp.exp(-t * jnp.abs(decay_ref[j:j + 1, :])) + HY_SHIFT
            hs.append(hv * win)
        hf = hs[0]
        hb = jnp.where(first, 0.0, hs[1])
        l1 = jnp.sum(jnp.abs(hf), axis=0, keepdims=True) + jnp.sum(jnp.abs(hb), axis=0, keepdims=True)
        hf = hf / l1
        hb = hb / l1
        even = hf + hb
        odd = hf - hb
        re = _dot3(fhi_ref[0:L, :], flo_ref[0:L, :], even)
        im = _dot3(fhi_ref[L:2 * L, :], flo_ref[L:2 * L, :], odd)
        nyq = jnp.sum(even * sign, axis=0, keepdims=True)
        a = re * jnp.where(first, 0.5 / L, 1.0 / L)
        spec_ref[o, 0] = a
        spec_ref[o, 1] = jnp.where(first, 0.0, im * (1.0 / L))
        spec_ref[o, 2] = jnp.where(first, nyq * (0.5 / L), a)


def _hyena_spectra(dft, l, hy_w1, hy_b1, hy_w2, hy_b2, hy_w3, hy_b3, hy_freq, hy_decay):
    emb, ff = hy_w1.shape[1:]
    d_hy = hy_decay.shape[-1]
    cb = HY_CB
    ncb = d_hy // cb
    w1p = jnp.zeros((LANES, ff), F32).at[:emb].set(hy_w1[l])
    f_hi, f_lo, _ = dft
    L = f_hi.shape[1]
    const = lambda j: (0, 0)

    def w3_spec(o, dr):
        return pl.BlockSpec((None, ff, cb), lambda j, o=o, dr=dr: (l, 0, (2 * o + dr) * ncb + j))

    return pl.pallas_call(
        functools.partial(_filter_kernel, L=L),
        grid=(ncb,),
        in_specs=[pl.BlockSpec((LANES, ff), const),
                  pl.BlockSpec((1, ff), const),
                  pl.BlockSpec((ff, ff), const),
                  pl.BlockSpec((1, ff), const),
                  pl.BlockSpec((2, ff), const),
                  w3_spec(0, 0), w3_spec(0, 1), w3_spec(1, 0), w3_spec(1, 1),
                  pl.BlockSpec((2 * HY_ORDER, cb), lambda j: (0, j)),
                  pl.BlockSpec((2 * HY_ORDER, cb), lambda j: (0, j)),
                  pl.BlockSpec((2 * L, L), const),
                  pl.BlockSpec((2 * L, L), const)],
        out_specs=pl.BlockSpec((HY_ORDER, 3, L, cb), lambda j: (0, 0, 0, j)),
        out_shape=jax.ShapeDtypeStruct((HY_ORDER, 3, L, d_hy), F32),
        scratch_shapes=[pltpu.VMEM((L, ff), F32)],
        compiler_params=_params("arbitrary"),
        name="hyena_filter",
    )(w1p, hy_b1[l].reshape(1, ff), hy_w2[l], hy_b2[l].reshape(1, ff), hy_freq[l],
      hy_w3, hy_w3, hy_w3, hy_w3,
      hy_b3[l].reshape(2 * HY_ORDER, d_hy), hy_decay[l].reshape(2 * HY_ORDER, d_hy),
      f_hi, f_lo)


def _short_conv3(x, w_ref, b, seq=None):
    n = x.shape[0]
    seq = n if seq is None else seq
    row = lax.broadcasted_iota(jnp.int32, x.shape, 0)
    first = functools.reduce(jnp.logical_or, [row == s for s in range(0, n, seq)])
    last = functools.reduce(jnp.logical_or, [row == s + seq - 1 for s in range(0, n, seq)])
    prev = jnp.where(first, 0.0, pltpu.roll(x, 1, axis=0))
    nxt = jnp.where(last, 0.0, pltpu.roll(x, n - 1, axis=0))
    return ((b + prev * w_ref[0:1, :]) + x * w_ref[1:2, :]) + nxt * w_ref[2:3, :]


def _hyena_kernel(h_ref, wx1_ref, wx2_ref, wv_ref, wz_ref, cw_ref, cb_ref, spec_ref, hbias_ref,
                  f_ref, g_ref, o_ref, *, nb, L):
    seqs = range(nb)
    wv = wv_ref[...].astype(BF16)
    z = [_short_conv3(_dot_nt(h_ref[b], wv), cw_ref.at[2], cb_ref[2:3, :]) for b in seqs]
    gate_w = (wx1_ref, wx2_ref)
    for o in range(HY_ORDER):
        zf = [_dot(f_ref[...], z[b].astype(BF16)) for b in seqs]
        wg = gate_w[o][...].astype(BF16)
        gate = [_short_conv3(_dot_nt(h_ref[b], wg), cw_ref.at[o], cb_ref[o:o + 1, :]) for b in seqs]
        ycat = []
        for b in seqs:
            zc, zs = zf[b][0:L], zf[b][L:2 * L]
            a, bm, dm = spec_ref[o, 0], spec_ref[o, 1], spec_ref[o, 2]
            ycat.append(jnp.concatenate([zc * a - zs * bm, zc * bm + zs * dm], axis=0).astype(BF16))
        y = [_dot(g_ref[...], ycat[b]) for b in seqs]
        z = [gate[b] * (y[b] + hbias_ref[o:o + 1, :] * z[b]) for b in seqs]
    wz = wz_ref[...].astype(BF16)
    for b in seqs:
        o_ref[b] = (z[b] * _silu(_dot_nt(h_ref[b], wz))).astype(o_ref.dtype)


def _hyena(h, l, w_in, conv_w, conv_b, spec, hy_bias, dft):
    bsz, L, d = h.shape
    d_hy = hy_bias.shape[-1]
    cb = HY_CB
    ncb = d_hy // cb
    nb = max(1, min(bsz, HY_ROWS // L))
    f, _, g = dft
    once = pl.Buffered(1)

    def w_spec(seg):
        return pl.BlockSpec((None, cb, d), lambda j, i, seg=seg: (l, seg * ncb + j, 0))

    return pl.pallas_call(
        functools.partial(_hyena_kernel, nb=nb, L=L),
        grid=(ncb, bsz // nb),
        in_specs=[pl.BlockSpec((nb, L, d), lambda j, i: (i, 0, 0)),
                  w_spec(0), w_spec(1), w_spec(2), w_spec(3),
                  pl.BlockSpec((3, 3, cb), lambda j, i: (0, 0, j)),
                  pl.BlockSpec((3, cb), lambda j, i: (0, j)),
                  pl.BlockSpec((HY_ORDER, 3, L, cb), lambda j, i: (0, 0, 0, j), pipeline_mode=once),
                  pl.BlockSpec((HY_ORDER, cb), lambda j, i: (0, j)),
                  pl.BlockSpec((2 * L, L), lambda j, i: (0, 0), pipeline_mode=once),
                  pl.BlockSpec((L, 2 * L), lambda j, i: (0, 0), pipeline_mode=once)],
        out_specs=pl.BlockSpec((nb, L, cb), lambda j, i: (i, 0, j)),
        out_shape=jax.ShapeDtypeStruct((bsz, L, d_hy), BF16),
        compiler_params=_params("arbitrary", "arbitrary"),
        name="hyena",
    )(h, w_in, w_in, w_in, w_in,
      conv_w.reshape(3, 3, d_hy).transpose(1, 0, 2), conv_b.reshape(3, d_hy), spec, hy_bias, f, g)


def _mlstm_kernel(*refs, nb, L, has_state, n_aliased):
    (h_ref, wq_ref, wk_ref, wv_ref, wo_ref, wz_ref, wg_ref, gb_ref, cwq_ref, cwk_ref,
     cbq_ref, cbk_ref, ng_ref) = refs[:13]
    refs = refs[13:]
    if has_state:
        c0_ref, n0_ref, m0_ref, yb_ref = refs[:4]
        refs = refs[4:]
    else:
        yb_ref, cn_ref, nn_ref, mn_ref = refs[n_aliased:n_aliased + 4]
        refs = refs[n_aliased + 4:]
    q_s, kt_s, v_s, g2_s, g2t_s, br_s, sc_s, hsum_s, c_s = refs

    T = ML_CHUNK
    nc = L // T
    R = nb * L
    d = h_ref.shape[-1]
    dh = q_s.shape[-1]
    hb = h_ref[...].reshape(R, d)
    q = _silu(_short_conv3(_dot_nt(hb, wq_ref[...].astype(BF16)), cwq_ref, cbq_ref[...], L))
    k = _silu(_short_conv3(_dot_nt(hb, wk_ref[...].astype(BF16)), cwk_ref, cbk_ref[...], L)) * (dh ** -0.5)
    q_s[...] = q.astype(BF16)
    kt_s[...] = k.T.astype(BF16)
    v_s[:, 0:dh] = _dot_nt(hb, wv_ref[...].astype(BF16)).astype(BF16)
    v_s[:, dh:dh + LANES] = jnp.ones((R, LANES), BF16)

    g = _dot_nt(hb, wg_ref[...]) + gb_ref[...]
    lf = _log_sigmoid(g)
    ti = lax.broadcasted_iota(jnp.int32, (T, T), 0)
    si = lax.broadcasted_iota(jnp.int32, (T, T), 1)
    causal = si <= ti
    anti = si >= ti
    lower = causal.astype(F32)
    upper = anti.astype(F32)
    lane = lax.broadcasted_iota(jnp.int32, (T, LANES), 1)
    for j in range(nb * nc):
        rows = slice(j * T, (j + 1) * T)
        pre = _dot_hi(lower, lf[rows])
        suf = _dot_hi(upper, lf[rows])
        g2_s[rows, :] = jnp.where(lane == 1, pre, jnp.where(lane == 3, suf, g[rows]))
        br_s[0, rows, :] = jnp.broadcast_to(pre[:, 1:2], (T, LANES))
        br_s[1, rows, :] = jnp.broadcast_to(suf[:, 3:4], (T, LANES))
    g2t_s[...] = g2_s[...].T[0:8, :]

    zero11 = jnp.zeros((1, 1), F32)
    m_fin = []
    for b in range(nb):
        vals = [[None] * 6 for _ in range(nc)]
        for dr in range(2):
            m = m0_ref[b, dr][:, 0:1] if has_state else zero11
            for ci in range(nc):
                c = ci if dr == 0 else nc - 1 - ci
                cols = slice(b * L + c * T, b * L + (c + 1) * T)
                irow = g2t_s[2 * dr:2 * dr + 1, cols]
                brow = g2t_s[2 * dr + 1:2 * dr + 2, cols]
                bl = brow[:, T - 1:T] if dr == 0 else brow[:, 0:1]
                mw = jnp.max(bl - brow + irow, axis=-1, keepdims=True)
                mn = jnp.maximum(bl + m, mw)
                vals[c][3 * dr:3 * dr + 3] = [m, mn, bl]
                m = mn
            m_fin.append(m)
        for c in range(nc):
            rows8 = [jnp.broadcast_to(v, (1, LANES)) for v in vals[c]] + [jnp.zeros((2, LANES), F32)]
            sc_s[b * nc + c] = jnp.concatenate(rows8, axis=0)

    hsum_s[...] = jnp.zeros_like(hsum_s)
    for b in range(nb):
        for dr in range(2):
            if has_state:
                c_s[2 * b + dr, :, 0:dh] = c0_ref[b, dr]
                c_s[2 * b + dr, :, dh:dh + LANES] = jnp.broadcast_to(n0_ref[b, dr], (LANES, dh)).T
            else:
                c_s[2 * b + dr] = jnp.zeros((dh, dh + LANES), F32)

    def step(ci, carry):
        for b in range(nb):
            for dr in range(2):
                c = ci if dr == 0 else nc - 1 - ci
                r0 = pl.multiple_of(b * L + c * T, T)
                qc = q_s[pl.ds(r0, T), :]
                vc = v_s[pl.ds(r0, T), :]
                ktc = kt_s[:, pl.ds(r0, T)]
                b_rep = br_s[dr, pl.ds(r0, T), :]
                g2tc = g2t_s[:, pl.ds(r0, T)]
                irow = g2tc[2 * dr:2 * dr + 1, :]
                brow = g2tc[2 * dr + 1:2 * dr + 2, :]
                sc = sc_s[b * nc + c]
                m_prev = sc[3 * dr:3 * dr + 1, :]
                m_new = sc[3 * dr + 1:3 * dr + 2, :]
                bl = sc[3 * dr + 2:3 * dr + 3, :]
                c_old = c_s[2 * b + dr]

                dlog = jnp.where(causal if dr == 0 else anti, b_rep - brow + irow, NEG)
                inter = b_rep + m_prev
                mt = jnp.maximum(inter, jnp.max(dlog, axis=-1, keepdims=True))
                smat = _dot(qc, ktc) * jnp.exp(dlog - mt)
                iw = jnp.exp(inter - mt)
                ext = (_dot(smat.astype(BF16), vc)
                       + jnp.concatenate([iw, iw, iw], axis=-1) * _dot(qc, c_old.astype(BF16)))
                inv = 1.0 / jnp.maximum(jnp.abs(ext[:, dh:dh + LANES]), jnp.exp(-mt))
                hh = ext[:, 0:dh] * jnp.concatenate([inv, inv], axis=-1)
                hsum_s[pl.ds(r0, T), :] = hsum_s[pl.ds(r0, T), :] + hh

                ws = jnp.exp(bl - brow + irow - m_new)
                dec = jnp.exp(bl + m_prev - m_new)[:, 0:1]
                kw = (ktc.astype(F32) * ws).astype(BF16)
                c_s[2 * b + dr] = dec * c_old + _dot(kw, vc)
        return carry

    lax.fori_loop(0, nc, step, 0, unroll=max(1, min(nc, ML_STREAMS // (2 * nb))))
    if not has_state:
        for b in range(nb):
            for dr in range(2):
                cn_ref[b, dr] = c_s[2 * b + dr, :, 0:dh]
                nn_ref[b, dr] = c_s[2 * b + dr, :, dh:dh + LANES].T[0:1, :]
                mn_ref[b, dr] = jnp.broadcast_to(m_fin[2 * b + dr], (1, LANES))

    hs = hsum_s[...]
    hm = hs * lax.rsqrt(jnp.mean(hs * hs, axis=-1, keepdims=True) + EPS) * ng_ref[...]
    uo = _dot_nt(hb, wo_ref[...].astype(BF16))
    uz = _dot_nt(hb, wz_ref[...].astype(BF16))
    yb_ref[...] = (hm * _sigmoid(uo) * _silu(uz)).astype(yb_ref.dtype).reshape(yb_ref.shape)


def _mlstm(h, l, depth, w_in, wg_heads, gb_heads, conv_w, conv_b, norm_g, states, carried,
           seg_off):
    bsz, L, d = h.shape
    heads = wg_heads.shape[0]
    d_ml = norm_g.shape[-1]
    dh = d_ml // heads
    has_state = states is not None
    nb = max(1, min(bsz, ML_ROWS // L))
    nc = L // ML_CHUNK
    qo, ko, vo, oo, zo = (s // dh for s in seg_off)

    def w_spec(off):
        return pl.BlockSpec((None, dh, d), lambda b, hd, off=off: (l, off + hd, 0))

    def vec_spec(rows, off):
        return pl.BlockSpec((rows, dh), lambda b, hd, off=off: (0, off + hd))

    in_specs = [pl.BlockSpec((nb, L, d), lambda b, hd: (b, 0, 0)),
                w_spec(qo), w_spec(ko), w_spec(vo), w_spec(oo), w_spec(zo),
                pl.BlockSpec((None, LANES, d), lambda b, hd: (hd, 0, 0)),
                pl.BlockSpec((None, 1, LANES), lambda b, hd: (hd, 0, 0)),
                vec_spec(3, 0), vec_spec(3, heads), vec_spec(1, 0), vec_spec(1, heads),
                vec_spec(1, 0)]
    args = [h, w_in, w_in, w_in, w_in, w_in, wg_heads, gb_heads,
            conv_w, conv_w, conv_b.reshape(1, -1), conv_b.reshape(1, -1), norm_g.reshape(1, -1)]
    yb_spec = pl.BlockSpec((nb, L, dh), lambda b, hd: (b, 0, hd))
    yb_shape = jax.ShapeDtypeStruct((bsz, L, d_ml), BF16)
    state_idx = lambda b, hd: (b, l, 0, hd, 0, 0)
    c_spec = pl.BlockSpec((nb, None, 2, None, dh, dh), state_idx)
    n_spec = pl.BlockSpec((nb, None, 2, None, 1, dh), state_idx)
    m_spec = pl.BlockSpec((nb, None, 2, None, 1, LANES), state_idx)
    aliases = {}
    if has_state:
        in_specs += [c_spec, n_spec, m_spec]
        args += list(states)
        out_specs, out_shape = yb_spec, yb_shape
    else:
        if carried is not None:
            aliases = {len(args) + i: 1 + i for i in range(3)}
            in_specs += [pl.BlockSpec(memory_space=pl.ANY)] * 3
            args += list(carried)
        out_specs = (yb_spec, c_spec, n_spec, m_spec)
        out_shape = (yb_shape,
                     jax.ShapeDtypeStruct((bsz, depth, 2, heads, dh, dh), F32),
                     jax.ShapeDtypeStruct((bsz, depth, 2, heads, 1, dh), F32),
                     jax.ShapeDtypeStruct((bsz, depth, 2, heads, 1, LANES), F32))
    rows = nb * L
    scratch = [pltpu.VMEM((rows, dh), BF16),
               pltpu.VMEM((dh, rows), BF16),
               pltpu.VMEM((rows, dh + LANES), BF16),
               pltpu.VMEM((rows, LANES), F32),
               pltpu.VMEM((8, rows), F32),
               pltpu.VMEM((2, rows, LANES), F32),
               pltpu.VMEM((nb * nc, 8, LANES), F32),
               pltpu.VMEM((rows, dh), F32),
               pltpu.VMEM((2 * nb, dh, dh + LANES), F32)]
    return pl.pallas_call(
        functools.partial(_mlstm_kernel, nb=nb, L=L, has_state=has_state, n_aliased=len(aliases)),
        grid=(bsz // nb, heads),
        in_specs=in_specs,
        out_specs=out_specs,
        out_shape=out_shape,
        input_output_aliases=aliases,
        scratch_shapes=scratch,
        compiler_params=_params("arbitrary", "arbitrary"),
        name="mlstm",
    )(*args)


def _out_kernel(x_ref, h_ref, ya_ref, yb_ref, mod_ref, wga_ref, wgb_ref, wpa_ref, wpb_ref,
                wout_ref, g_ref, *rest, last):
    hb = h_ref[...]
    ga = _sigmoid(_dot_nt(hb, wga_ref[...]))
    gb = _sigmoid(_dot_nt(hb, wgb_ref[...]))
    merged = ga * _dot(ya_ref[...], wpa_ref[...]) + gb * _dot(yb_ref[...], wpb_ref[...])
    xn = x_ref[...] + mod_ref[2:3, :] * _dot(merged.astype(BF16), wout_ref[...])
    normed = xn * lax.rsqrt(jnp.mean(xn * xn, axis=-1, keepdims=True) + EPS) * g_ref[...]
    if last:
        (y_ref,) = rest
        y_ref[...] = normed
    else:
        modn_ref, xo_ref, ho_ref = rest
        xo_ref[...] = xn
        ho_ref[...] = (normed * (1.0 + modn_ref[1:2, :]) + modn_ref[0:1, :]).astype(ho_ref.dtype)


def _out(x, h, ya, yb, mod, l, w_gab, w_pa, w_pb, w_out, g_next, mod_next):
    bm, t, d = x.shape
    tm = min(t, OUT_TM)
    last = mod_next is None
    tok = lambda b, i: (b, i, 0)
    wl = lambda b, i: (l, 0, 0)
    tok_spec = pl.BlockSpec((None, tm, d), tok)
    mod_spec = pl.BlockSpec((None, 3, d), lambda b, i: (b, 0, 0))
    in_specs = [tok_spec, tok_spec, tok_spec, tok_spec, mod_spec,
                pl.BlockSpec((None, d, d), wl),
                pl.BlockSpec((None, d, d), lambda b, i: (l, 1, 0)),
                pl.BlockSpec((None, d, d), wl),
                pl.BlockSpec((None, d, d), wl),
                pl.BlockSpec((None, d, d), wl),
                pl.BlockSpec((1, d), lambda b, i: (0, 0))]
    args = [x, h, ya, yb, mod, w_gab, w_gab, w_pa, w_pb, w_out, g_next.reshape(1, d)]
    if last:
        out_specs, out_shape = tok_spec, jax.ShapeDtypeStruct((bm, t, d), F32)
    else:
        in_specs.append(mod_spec)
        args.append(mod_next)
        out_specs = (tok_spec, tok_spec)
        out_shape = (jax.ShapeDtypeStruct((bm, t, d), F32), jax.ShapeDtypeStruct((bm, t, d), BF16))
    return pl.pallas_call(
        functools.partial(_out_kernel, last=last),
        grid=(bm, t // tm),
        in_specs=in_specs,
        out_specs=out_specs,
        out_shape=out_shape,
        compiler_params=_params("arbitrary", "arbitrary"),
        name="merge_out",
    )(*args)


def kernel(x_prompt, x_sample, state_C, state_n, state_m, c, c_ctx, norm_g, w_ada, b_ada, w_in, hy_conv_w, hy_conv_b, hy_w1, hy_b1, hy_w2, hy_b2, hy_w3, hy_b3, hy_freq, hy_decay, hy_bias, ml_conv_w, ml_conv_b, ml_if_b, ml_norm_g, w_pa, w_pb, w_out, final_g):
    depth, d, _ = w_in.shape
    d_hy = hy_bias.shape[-1]
    d_ml = ml_norm_g.shape[-1]
    heads = ml_if_b.shape[-1]
    bp, lp, _ = x_prompt.shape
    bs, ls, _ = x_sample.shape

    off_q = 4 * d_hy
    off_v = off_q + 2 * d_ml
    off_o = off_v + d_ml
    off_z = off_o + d_ml
    off_g = off_z + d_ml
    off_ga = off_g + 4 * heads
    seg_off = (off_q, off_q + d_ml, off_v, off_o, off_z)

    w_t = jnp.transpose(w_in, (0, 2, 1))
    w_gab = w_t[:, off_ga:, :].astype(BF16)
    wg = w_t[:, off_g:off_ga, :].reshape(depth, 4, heads, d).transpose(0, 2, 1, 3)
    wg_heads = jnp.zeros((depth, heads, LANES, d), BF16).at[:, :, :4, :].set(wg.astype(BF16))
    gb = ml_if_b.reshape(depth, 4, heads).transpose(0, 2, 1)
    gb_heads = jnp.zeros((depth, heads, 1, LANES), F32).at[:, :, 0, :4].set(gb)
    w_pa_bf, w_pb_bf, w_out_bf = (w.astype(BF16) for w in (w_pa, w_pb, w_out))

    cvecs = jnp.zeros((8, d), F32).at[0].set(c_ctx).at[1:1 + bs].set(c)
    mods = _mods(cvecs, w_ada, b_ada).reshape(depth, 8, 3, d)

    xp = x_prompt.reshape(1, bp * lp, d)
    xs = x_sample
    cached = (state_C, state_n.reshape(bs, depth, 2, heads, 1, -1),
              jnp.broadcast_to(state_m[..., None, None], (bs, depth, 2, heads, 1, LANES)))
    fin = None
    hp = _norm_mod(xp, norm_g[0], mods[0, 0:1])
    hs = _norm_mod(xs, norm_g[0], mods[0, 1:1 + bs])
    dft = {L: _dft_operands(L) for L in sorted({lp, ls})}
    for l in range(depth):
        spectra = {}
        for L in sorted({lp, ls}):
            spectra[L] = _hyena_spectra(dft[L], l, hy_w1, hy_b1, hy_w2, hy_b2, hy_w3, hy_b3,
                                        hy_freq, hy_decay)

        last = l == depth - 1
        g_next = final_g if last else norm_g[l + 1]

        def layer(x, h, rows, bsz, L, states, carried):
            mod = mods[l, rows]
            hseq = h.reshape(bsz, L, d)
            ya = _hyena(hseq, l, w_t, hy_conv_w[l], hy_conv_b[l], spectra[L], hy_bias[l], dft[L])
            res = _mlstm(hseq, l, depth, w_t, wg_heads[l], gb_heads[l], ml_conv_w[l],
                         ml_conv_b[l], ml_norm_g[l], states, carried, seg_off)
            yb, fin = (res, None) if states is not None else (res[0], res[1:])
            res = _out(x, h, ya.reshape(x.shape), yb.reshape(x.shape), mod, l,
                       w_gab, w_pa_bf, w_pb_bf, w_out_bf, g_next,
                       None if last else mods[l + 1, rows])
            return (res, None, fin) if last else (res[0], res[1], fin)

        xp, hp, fin = layer(xp, hp, slice(0, 1), bp, lp, None, fin)
        xs, hs, _ = layer(xs, hs, slice(1, 1 + bs), bs, ls, cached, None)

    return (xp.reshape(bp, lp, d), xs, fin[0], fin[1][:, :, :, :, 0, :], fin[2][:, :, :, :, 0, 0])
```

```python
import functools
import math

import numpy as np
import jax
import jax.numpy as jnp
from jax import lax
from jax.experimental import pallas as pl
from jax.experimental.pallas import tpu as pltpu

F32 = jnp.float32
BF16 = jnp.bfloat16
HIGHEST = lax.Precision.HIGHEST

HY_ORDER = 2
HY_BANDS = 16
HY_SHIFT = 0.05
ML_CHUNK = 128
EPS = 1e-6
NEG = -1e30

LANES = 128
VMEM_LIMIT = 56 * 1024 * 1024
HY_CB = 256
ML_ROWS = 1024
HY_ROWS = 2048
ML_STREAMS = 8
OUT_TM = 512


def _dot(a, b):
    return jnp.dot(a, b, preferred_element_type=F32)


def _dot_nt(a, bt):
    return lax.dot_general(a, bt, (((1,), (1,)), ((), ())), preferred_element_type=F32)


def _dot_hi(a, b):
    return jnp.dot(a, b, preferred_element_type=F32, precision=HIGHEST)


def _sigmoid(x):
    return 1.0 / (1.0 + jnp.exp(-x))


def _silu(x):
    return x * _sigmoid(x)


def _log_sigmoid(x):
    return jnp.minimum(x, 0.0) - jnp.log(1.0 + jnp.exp(-jnp.abs(x)))


def _params(*sem):
    return pltpu.CompilerParams(dimension_semantics=sem, vmem_limit_bytes=VMEM_LIMIT)


@functools.lru_cache(maxsize=None)
def _dft_table(L):
    N = 2 * L
    k = np.arange(L, dtype=np.int64)[:, None]
    n = np.arange(L, dtype=np.int64)[None, :]
    ang = 2.0 * np.pi * ((k * n) % N).astype(np.float64) / N
    c = np.cos(ang)
    s = -np.sin(ang)
    s[0, :] = 1.0 - 2.0 * (np.arange(L) % 2)
    return np.concatenate([c, s], axis=0).astype(np.float32)


def _dft_operands(L):
    f = jnp.asarray(_dft_table(L))
    f_hi = f.astype(BF16)
    f_lo = (f - f_hi.astype(F32)).astype(BF16)
    return f_hi, f_lo, f_hi.T


def _mod_kernel(c_ref, w_ref, b_ref, o_ref):
    o_ref[...] = _dot_hi(_silu(c_ref[...]), w_ref[...]) + b_ref[...]


def _mods(cvecs, w_ada, b_ada):
    depth, d, d3 = w_ada.shape
    r = cvecs.shape[0]
    tn = 512
    return pl.pallas_call(
        _mod_kernel,
        grid=(depth, d3 // tn),
        in_specs=[pl.BlockSpec((r, d), lambda l, j: (0, 0)),
                  pl.BlockSpec((None, d, tn), lambda l, j: (l, 0, j)),
                  pl.BlockSpec((None, 1, tn), lambda l, j: (l, 0, j))],
        out_specs=pl.BlockSpec((None, r, tn), lambda l, j: (l, 0, j)),
        out_shape=jax.ShapeDtypeStruct((depth, r, d3), F32),
        compiler_params=_params("arbitrary", "arbitrary"),
        name="adaln_mod",
    )(cvecs, w_ada, b_ada.reshape(depth, 1, d3))


def _norm_mod_kernel(x_ref, g_ref, mod_ref, o_ref):
    x = x_ref[...]
    y = x * lax.rsqrt(jnp.mean(x * x, axis=-1, keepdims=True) + EPS) * g_ref[...]
    o_ref[...] = (y * (1.0 + mod_ref[1:2, :]) + mod_ref[0:1, :]).astype(o_ref.dtype)


def _norm_mod(x, g, mod):
    bm, t, d = x.shape
    tm = min(t, 1024)
    return pl.pallas_call(
        _norm_mod_kernel,
        grid=(bm, t // tm),
        in_specs=[pl.BlockSpec((None, tm, d), lambda b, i: (b, i, 0)),
                  pl.BlockSpec((1, d), lambda b, i: (0, 0)),
                  pl.BlockSpec((None, 3, d), lambda b, i: (b, 0, 0))],
        out_specs=pl.BlockSpec((None, tm, d), lambda b, i: (b, i, 0)),
        out_shape=jax.ShapeDtypeStruct((bm, t, d), BF16),
        compiler_params=_params("arbitrary", "arbitrary"),
        name="norm_mod",
    )(x, g.reshape(1, d), mod)


def _dot3(f_hi, f_lo, a):
    a_hi = a.astype(BF16)
    a_lo = (a - a_hi.astype(F32)).astype(BF16)
    return _dot(f_hi, a_hi) + (_dot(f_lo, a_hi) + _dot(f_hi, a_lo))


def _filter_kernel(w1_ref, b1_ref, w2_ref, b2_ref, freq_ref, w3_00, w3_01, w3_10, w3_11,
                   b3_ref, decay_ref, fhi_ref, flo_ref, spec_ref, hdn_ref, *, L):
    cb = spec_ref.shape[-1]

    @pl.when(pl.program_id(0) == 0)
    def _():
        row = lax.broadcasted_iota(jnp.int32, (L, LANES), 0).astype(F32)
        lane = lax.broadcasted_iota(jnp.int32, (L, LANES), 1)
        t = row / L
        band = jnp.where(lane <= HY_BANDS, lane, lane - HY_BANDS).astype(F32)
        ang = 2.0 * math.pi * t * band
        feats = jnp.where(lane == 0, t,
                          jnp.where(lane <= HY_BANDS, jnp.cos(ang),
                                    jnp.where(lane <= 2 * HY_BANDS, jnp.sin(ang), 0.0)))
        hdn = jnp.sin(freq_ref[0:1, :] * (_dot_hi(feats, w1_ref[...]) + b1_ref[...]))
        hdn_ref[...] = jnp.sin(freq_ref[1:2, :] * (_dot_hi(hdn, w2_ref[...]) + b2_ref[...]))

    hdn = hdn_ref[...]
    row_i = lax.broadcasted_iota(jnp.int32, (L, cb), 0)
    t = row_i.astype(F32) / L
    first = row_i == 0
    sign = jnp.where((row_i & 1) == 0, 1.0, -1.0)
    w3 = ((w3_00, w3_01), (w3_10, w3_11))
    for o in range(HY_ORDER):
        hs = []
        for dr in range(2):
            j = 2 * o + dr
            hv = _dot_hi(hdn, w3[o][dr][...]) + b3_ref[j:j + 1, :]
            win = jnp.exp(-t * jnp.abs(decay_ref[j:j + 1, :])) + HY_SHIFT
            hs.append(hv * win)
        hf = hs[0]
        hb = jnp.where(first, 0.0, hs[1])
        l1 = jnp.sum(jnp.abs(hf), axis=0, keepdims=True) + jnp.sum(jnp.abs(hb), axis=0, keepdims=True)
        hf = hf / l1
        hb = hb / l1
        even = hf + hb
        odd = hf - hb
        re = _dot3(fhi_ref[0:L, :], flo_ref[0:L, :], even)
        im = _dot3(fhi_ref[L:2 * L, :], flo_ref[L:2 * L, :], odd)
        nyq = jnp.sum(even * sign, axis=0, keepdims=True)
        a = re * jnp.where(first, 0.5 / L, 1.0 / L)
        spec_ref[o, 0] = a
        spec_ref[o, 1] = jnp.where(first, 0.0, im * (1.0 / L))
        spec_ref[o, 2] = jnp.where(first, nyq * (0.5 / L), a)


def _hyena_spectra(dft, l, hy_w1, hy_b1, hy_w2, hy_b2, hy_w3, hy_b3, hy_freq, hy_decay):
    emb, ff = hy_w1.shape[1:]
    d_hy = hy_decay.shape[-1]
    cb = HY_CB
    ncb = d_hy // cb
    w1p = jnp.zeros((LANES, ff), F32).at[:emb].set(hy_w1[l])
    f_hi, f_lo, _ = dft
    L = f_hi.shape[1]
    const = lambda j: (0, 0)

    def w3_spec(o, dr):
        return pl.BlockSpec((None, ff, cb), lambda j, o=o, dr=dr: (l, 0, (2 * o + dr) * ncb + j))

    return pl.pallas_call(
        functools.partial(_filter_kernel, L=L),
        grid=(ncb,),
        in_specs=[pl.BlockSpec((LANES, ff), const),
                  pl.BlockSpec((1, ff), const),
                  pl.BlockSpec((ff, ff), const),
                  pl.BlockSpec((1, ff), const),
                  pl.BlockSpec((2, ff), const),
                  w3_spec(0, 0), w3_spec(0, 1), w3_spec(1, 0), w3_spec(1, 1),
                  pl.BlockSpec((2 * HY_ORDER, cb), lambda j: (0, j)),
                  pl.BlockSpec((2 * HY_ORDER, cb), lambda j: (0, j)),
                  pl.BlockSpec((2 * L, L), const),
                  pl.BlockSpec((2 * L, L), const)],
        out_specs=pl.BlockSpec((HY_ORDER, 3, L, cb), lambda j: (0, 0, 0, j)),
        out_shape=jax.ShapeDtypeStruct((HY_ORDER, 3, L, d_hy), F32),
        scratch_shapes=[pltpu.VMEM((L, ff), F32)],
        compiler_params=_params("arbitrary"),
        name="hyena_filter",
    )(w1p, hy_b1[l].reshape(1, ff), hy_w2[l], hy_b2[l].reshape(1, ff), hy_freq[l],
      hy_w3, hy_w3, hy_w3, hy_w3,
      hy_b3[l].reshape(2 * HY_ORDER, d_hy), hy_decay[l].reshape(2 * HY_ORDER, d_hy),
      f_hi, f_lo)


def _short_conv3(x, w_ref, b, seq=None):
    n = x.shape[0]
    seq = n if seq is None else seq
    row = lax.broadcasted_iota(jnp.int32, x.shape, 0)
    first = functools.reduce(jnp.logical_or, [row == s for s in range(0, n, seq)])
    last = functools.reduce(jnp.logical_or, [row == s + seq - 1 for s in range(0, n, seq)])
    prev = jnp.where(first, 0.0, pltpu.roll(x, 1, axis=0))
    nxt = jnp.where(last, 0.0, pltpu.roll(x, n - 1, axis=0))
    return ((b + prev * w_ref[0:1, :]) + x * w_ref[1:2, :]) + nxt * w_ref[2:3, :]


def _hyena_kernel(h_ref, wx1_ref, wx2_ref, wv_ref, wz_ref, cw_ref, cb_ref, spec_ref, hbias_ref,
                  f_ref, g_ref, o_ref, *, nb, L):
    seqs = range(nb)
    wv = wv_ref[...].astype(BF16)
    z = [_short_conv3(_dot_nt(h_ref[b], wv), cw_ref.at[2], cb_ref[2:3, :]) for b in seqs]
    gate_w = (wx1_ref, wx2_ref)
    for o in range(HY_ORDER):
        zf = [_dot(f_ref[...], z[b].astype(BF16)) for b in seqs]
        wg = gate_w[o][...].astype(BF16)
        gate = [_short_conv3(_dot_nt(h_ref[b], wg), cw_ref.at[o], cb_ref[o:o + 1, :]) for b in seqs]
        ycat = []
        for b in seqs:
            zc, zs = zf[b][0:L], zf[b][L:2 * L]
            a, bm, dm = spec_ref[o, 0], spec_ref[o, 1], spec_ref[o, 2]
            ycat.append(jnp.concatenate([zc * a - zs * bm, zc * bm + zs * dm], axis=0).astype(BF16))
        y = [_dot(g_ref[...], ycat[b]) for b in seqs]
        z = [gate[b] * (y[b] + hbias_ref[o:o + 1, :] * z[b]) for b in seqs]
    wz = wz_ref[...].astype(BF16)
    for b in seqs:
        o_ref[b] = (z[b] * _silu(_dot_nt(h_ref[b], wz))).astype(o_ref.dtype)


def _hyena(h, l, w_in, conv_w, conv_b, spec, hy_bias, dft):
    bsz, L, d = h.shape
    d_hy = hy_bias.shape[-1]
    cb = HY_CB
    ncb = d_hy // cb
    nb = max(1, min(bsz, HY_ROWS // L))
    f, _, g = dft
    once = pl.Buffered(1)

    def w_spec(seg):
        return pl.BlockSpec((None, cb, d), lambda j, i, seg=seg: (l, seg * ncb + j, 0))

    return pl.pallas_call(
        functools.partial(_hyena_kernel, nb=nb, L=L),
        grid=(ncb, bsz // nb),
        in_specs=[pl.BlockSpec((nb, L, d), lambda j, i: (i, 0, 0)),
                  w_spec(0), w_spec(1), w_spec(2), w_spec(3),
                  pl.BlockSpec((3, 3, cb), lambda j, i: (0, 0, j)),
                  pl.BlockSpec((3, cb), lambda j, i: (0, j)),
                  pl.BlockSpec((HY_ORDER, 3, L, cb), lambda j, i: (0, 0, 0, j), pipeline_mode=once),
                  pl.BlockSpec((HY_ORDER, cb), lambda j, i: (0, j)),
                  pl.BlockSpec((2 * L, L), lambda j, i: (0, 0), pipeline_mode=once),
                  pl.BlockSpec((L, 2 * L), lambda j, i: (0, 0), pipeline_mode=once)],
        out_specs=pl.BlockSpec((nb, L, cb), lambda j, i: (i, 0, j)),
        out_shape=jax.ShapeDtypeStruct((bsz, L, d_hy), BF16),
        compiler_params=_params("arbitrary", "arbitrary"),
        name="hyena",
    )(h, w_in, w_in, w_in, w_in,
      conv_w.reshape(3, 3, d_hy).transpose(1, 0, 2), conv_b.reshape(3, d_hy), spec, hy_bias, f, g)


def _mlstm_kernel(*refs, nb, L, has_state, n_aliased):
    (h_ref, wq_ref, wk_ref, wv_ref, wo_ref, wz_ref, wg_ref, gb_ref, cwq_ref, cwk_ref,
     cbq_ref, cbk_ref, ng_ref) = refs[:13]
    refs = refs[13:]
    if has_state:
        c0_ref, n0_ref, m0_ref, yb_ref = refs[:4]
        refs = refs[4:]
    else:
        yb_ref, cn_ref, nn_ref, mn_ref = refs[n_aliased:n_aliased + 4]
        refs = refs[n_aliased + 4:]
    q_s, kt_s, v_s, g2_s, g2t_s, br_s, sc_s, hsum_s, og_s, c_s = refs

    T = ML_CHUNK
    nc = L // T
    R = nb * L
    d = h_ref.shape[-1]
    dh = q_s.shape[-1]
    hb = h_ref[...].reshape(R, d)

    g = _dot_nt(hb, wg_ref[...]) + gb_ref[...]
    lf = _log_sigmoid(g)
    ti = lax.broadcasted_iota(jnp.int32, (T, T), 0)
    si = lax.broadcasted_iota(jnp.int32, (T, T), 1)
    causal = si <= ti
    anti = si >= ti
    lower = causal.astype(F32)
    upper = anti.astype(F32)
    lane = lax.broadcasted_iota(jnp.int32, (T, LANES), 1)
    for j in range(nb * nc):
        rows = slice(j * T, (j + 1) * T)
        pre = _dot_hi(lower, lf[rows])
        suf = _dot_hi(upper, lf[rows])
        g2_s[rows, :] = jnp.where(lane == 1, pre, jnp.where(lane == 3, suf, g[rows]))
        br_s[0, rows, :] = jnp.broadcast_to(pre[:, 1:2], (T, LANES))
        br_s[1, rows, :] = jnp.broadcast_to(suf[:, 3:4], (T, LANES))
    g2t_s[...] = g2_s[...].T[0:8, :]

    zero11 = jnp.zeros((1, 1), F32)
    m_fin = []
    for b in range(nb):
        vals = [[None] * 6 for _ in range(nc)]
        for dr in range(2):
            m = m0_ref[b, dr][:, 0:1] if has_state else zero11
            for ci in range(nc):
                c = ci if dr == 0 else nc - 1 - ci
                cols = slice(b * L + c * T, b * L + (c + 1) * T)
                irow = g2t_s[2 * dr:2 * dr + 1, cols]
                brow = g2t_s[2 * dr + 1:2 * dr + 2, cols]
                bl = brow[:, T - 1:T] if dr == 0 else brow[:, 0:1]
                mw = jnp.max(bl - brow + irow, axis=-1, keepdims=True)
                mn = jnp.maximum(bl + m, mw)
                vals[c][3 * dr:3 * dr + 3] = [m, mn, bl]
                m = mn
            m_fin.append(m)
        for c in range(nc):
            rows8 = [jnp.broadcast_to(v, (1, LANES)) for v in vals[c]] + [jnp.zeros((2, LANES), F32)]
            sc_s[b * nc + c] = jnp.concatenate(rows8, axis=0)

    uq =_dot_nt(hb, wq_ref[...].astype(BF16))
    uk = _dot_nt(hb, wk_ref[...].astype(BF16))
    q_s[...] = _silu(_short_conv3(uq, cwq_ref, cbq_ref[...], L)).astype(BF16)
    uv = _dot_nt(hb, wv_ref[...].astype(BF16))
    k = _silu(_short_conv3(uk, cwk_ref, cbk_ref[...], L)) * (dh ** -0.5)
    kt_s[...] = k.T.astype(BF16)
    uo = _dot_nt(hb, wo_ref[...].astype(BF16))
    v_s[:, 0:dh] = uv.astype(BF16)
    v_s[:, dh:dh + LANES] = jnp.ones((R, LANES), BF16)
    uz = _dot_nt(hb, wz_ref[...].astype(BF16))
    og_s[...] = _sigmoid(uo) * _silu(uz)

    hsum_s[...] = jnp.zeros_like(hsum_s)
    for b in range(nb):
        for dr in range(2):
            if has_state:
                c_s[2 * b + dr, :, 0:dh] = c0_ref[b, dr]
                c_s[2 * b + dr, :, dh:dh + LANES] = jnp.broadcast_to(n0_ref[b, dr], (LANES, dh)).T
            else:
                c_s[2 * b + dr] = jnp.zeros((dh, dh + LANES), F32)

    def step(ci, carry):
        for b in range(nb):
            for dr in range(2):
                c = ci if dr == 0 else nc - 1 - ci
                r0 = pl.multiple_of(b * L + c * T, T)
                qc = q_s[pl.ds(r0, T), :]
                vc = v_s[pl.ds(r0, T), :]
                ktc = kt_s[:, pl.ds(r0, T)]
                b_rep = br_s[dr, pl.ds(r0, T), :]
                g2tc = g2t_s[:, pl.ds(r0, T)]
                irow = g2tc[2 * dr:2 * dr + 1, :]
                brow = g2tc[2 * dr + 1:2 * dr + 2, :]
                sc = sc_s[b * nc + c]
                m_prev = sc[3 * dr:3 * dr + 1, :]
                m_new = sc[3 * dr + 1:3 * dr + 2, :]
                bl = sc[3 * dr + 2:3 * dr + 3, :]
                c_old = c_s[2 * b + dr]

                dlog = jnp.where(causal if dr == 0 else anti, b_rep - brow + irow, NEG)
                inter = b_rep + m_prev
                mt = jnp.maximum(inter, jnp.max(dlog, axis=-1, keepdims=True))
                smat = _dot(qc, ktc) * jnp.exp(dlog - mt)
                iw = jnp.exp(inter - mt)
                ext = (_dot(smat.astype(BF16), vc)
                       + jnp.concatenate([iw, iw, iw], axis=-1) * _dot(qc, c_old.astype(BF16)))
                inv = 1.0 / jnp.maximum(jnp.abs(ext[:, dh:dh + LANES]), jnp.exp(-mt))
                hh = ext[:, 0:dh] * jnp.concatenate([inv, inv], axis=-1)
                hsum_s[pl.ds(r0, T), :] = hsum_s[pl.ds(r0, T), :] + hh

                ws = jnp.exp(bl - brow + irow - m_new)
                dec = jnp.exp(bl + m_prev - m_new)[:, 0:1]
                kw = (ktc.astype(F32) * ws).astype(BF16)
                c_s[2 * b + dr] = dec * c_old + _dot(kw, vc)
        return carry

    lax.fori_loop(0, nc, step, 0, unroll=max(1, min(nc, ML_STREAMS // (2 * nb))))
    if not has_state:
        for b in range(nb):
            for dr in range(2):
                cn_ref[b, dr] = c_s[2 * b + dr, :, 0:dh]
                nn_ref[b, dr] = c_s[2 * b + dr, :, dh:dh + LANES].T[0:1, :]
                mn_ref[b, dr] = jnp.broadcast_to(m_fin[2 * b + dr], (1, LANES))

    hs = hsum_s[...]
    hm = hs * lax.rsqrt(jnp.mean(hs * hs, axis=-1, keepdims=True) + EPS) * ng_ref[...]
    yb_ref[...] = (hm * og_s[...]).astype(yb_ref.dtype).reshape(yb_ref.shape)


def _mlstm(h, l, depth, w_in, wg_heads, gb_heads, conv_w, conv_b, norm_g, states, carried,
           seg_off):
    bsz, L, d = h.shape
    heads = wg_heads.shape[0]
    d_ml = norm_g.shape[-1]
    dh = d_ml // heads
    has_state = states is not None
    nb = max(1, min(bsz, ML_ROWS // L))
    nc = L // ML_CHUNK
    qo, ko, vo, oo, zo = (s // dh for s in seg_off)

    def w_spec(off):
        return pl.BlockSpec((None, dh, d), lambda b, hd, off=off: (l, off + hd, 0))

    def vec_spec(rows, off):
        return pl.BlockSpec((rows, dh), lambda b, hd, off=off: (0, off + hd))

    in_specs = [pl.BlockSpec((nb, L, d), lambda b, hd: (b, 0, 0)),
                w_spec(qo), w_spec(ko), w_spec(vo), w_spec(oo), w_spec(zo),
                pl.BlockSpec((None, LANES, d), lambda b, hd: (hd, 0, 0)),
                pl.BlockSpec((None, 1, LANES), lambda b, hd: (hd, 0, 0)),
                vec_spec(3, 0), vec_spec(3, heads), vec_spec(1, 0), vec_spec(1, heads),
                vec_spec(1, 0)]
    args = [h, w_in, w_in, w_in, w_in, w_in, wg_heads, gb_heads,
            conv_w, conv_w, conv_b.reshape(1, -1), conv_b.reshape(1, -1), norm_g.reshape(1, -1)]
    yb_spec = pl.BlockSpec((nb, L, dh), lambda b, hd: (b, 0, hd))
    yb_shape = jax.ShapeDtypeStruct((bsz, L, d_ml), BF16)
    state_idx = lambda b, hd: (b, l, 0, hd, 0, 0)
    c_spec = pl.BlockSpec((nb, None, 2, None, dh, dh), state_idx)
    n_spec = pl.BlockSpec((nb, None, 2, None, 1, dh), state_idx)
    m_spec = pl.BlockSpec((nb, None, 2, None, 1, LANES), state_idx)
    aliases = {}
    if has_state:
        in_specs += [c_spec, n_spec, m_spec]
        args += list(states)
        out_specs, out_shape = yb_spec, yb_shape
    else:
        if carried is not None:
            aliases = {len(args) + i: 1 + i for i in range(3)}
            in_specs += [pl.BlockSpec(memory_space=pl.ANY)] * 3
            args += list(carried)
        out_specs = (yb_spec, c_spec, n_spec, m_spec)
        out_shape = (yb_shape,
                     jax.ShapeDtypeStruct((bsz, depth, 2, heads, dh, dh), F32),
                     jax.ShapeDtypeStruct((bsz, depth, 2, heads, 1, dh), F32),
                     jax.ShapeDtypeStruct((bsz, depth, 2, heads, 1, LANES), F32))
    rows = nb * L
    scratch = [pltpu.VMEM((rows, dh), BF16),
               pltpu.VMEM((dh, rows), BF16),
               pltpu.VMEM((rows, dh + LANES), BF16),
               pltpu.VMEM((rows, LANES), F32),
               pltpu.VMEM((8, rows), F32),
               pltpu.VMEM((2, rows, LANES), F32),
               pltpu.VMEM((nb * nc, 8, LANES), F32),
               pltpu.VMEM((rows, dh), F32),
               pltpu.VMEM((rows, dh), F32),
               pltpu.VMEM((2 * nb, dh, dh + LANES), F32)]
    return pl.pallas_call(
        functools.partial(_mlstm_kernel, nb=nb, L=L, has_state=has_state, n_aliased=len(aliases)),
        grid=(bsz // nb, heads),
        in_specs=in_specs,
        out_specs=out_specs,
        out_shape=out_shape,
        input_output_aliases=aliases,
        scratch_shapes=scratch,
        compiler_params=_params("arbitrary", "arbitrary"),
        name="mlstm",
    )(*args)


def _out_kernel(x_ref, h_ref, ya_ref, yb_ref, mod_ref, wga_ref, wgb_ref, wpa_ref, wpb_ref,
                wout_ref, g_ref, *rest, last):
    w_s = rest[-1]
    rest = rest[:-1]

    @pl.when((pl.program_id(0) == 0) & (pl.program_id(1) == 0))
    def _():
        w_s[0] = wga_ref[0].astype(BF16)
        w_s[1] = wgb_ref[0].astype(BF16)
        w_s[2] = wpa_ref[...].astype(BF16)
        w_s[3] = wpb_ref[...].astype(BF16)
        w_s[4] = wout_ref[...].astype(BF16)

    hb = h_ref[...]
    ga = _sigmoid(_dot_nt(hb, w_s[0]))
    gb = _sigmoid(_dot_nt(hb, w_s[1]))
    merged = ga * _dot(ya_ref[...], w_s[2]) + gb * _dot(yb_ref[...], w_s[3])
    xn = x_ref[...] + mod_ref[2:3, :] * _dot(merged.astype(BF16), w_s[4])
    normed = xn * lax.rsqrt(jnp.mean(xn * xn, axis=-1, keepdims=True) + EPS) * g_ref[...]
    if last:
        (y_ref,) = rest
        y_ref[...] = normed
    else:
        modn_ref, xo_ref, ho_ref = rest
        xo_ref[...] = xn
        ho_ref[...] = (normed * (1.0 + modn_ref[1:2, :]) + modn_ref[0:1, :]).astype(ho_ref.dtype)


def _out(x, h, ya, yb, mod, l, w_t, off_ga, w_pa, w_pb, w_out, g_next, mod_next):
    bm, t, d = x.shape
    tm = min(t, OUT_TM)
    last = mod_next is None
    tok = lambda b, i: (b, i, 0)
    wl = lambda b, i: (l, 0, 0)
    tok_spec = pl.BlockSpec((None, tm, d), tok)
    mod_spec = pl.BlockSpec((None, 3, d), lambda b, i: (b, 0, 0))
    gate_blk = (pl.Element(1), pl.Element(d), pl.Element(d))
    once = pl.Buffered(1)
    in_specs = [tok_spec, tok_spec, tok_spec, tok_spec, mod_spec,
                pl.BlockSpec(gate_blk, lambda b, i: (l, off_ga, 0), pipeline_mode=once),
                pl.BlockSpec(gate_blk, lambda b, i: (l, off_ga + d, 0), pipeline_mode=once),
                pl.BlockSpec((None, d, d), wl, pipeline_mode=once),
                pl.BlockSpec((None, d, d), wl, pipeline_mode=once),
                pl.BlockSpec((None, d, d), wl, pipeline_mode=once),
                pl.BlockSpec((1, d), lambda b, i: (0, 0))]
    args = [x, h, ya, yb, mod, w_t, w_t, w_pa, w_pb, w_out, g_next.reshape(1, d)]
    if last:
        out_specs, out_shape = tok_spec, jax.ShapeDtypeStruct((bm, t, d), F32)
    else:
        in_specs.append(mod_spec)
        args.append(mod_next)
        out_specs = (tok_spec, tok_spec)
        out_shape = (jax.ShapeDtypeStruct((bm, t, d), F32), jax.ShapeDtypeStruct((bm, t, d), BF16))
    return pl.pallas_call(
        functools.partial(_out_kernel, last=last),
        grid=(bm, t // tm),
        in_specs=in_specs,
        out_specs=out_specs,
        out_shape=out_shape,
        scratch_shapes=[pltpu.VMEM((5, d, d), BF16)],
        compiler_params=_params("arbitrary", "arbitrary"),
        name="merge_out",
    )(*args)


def kernel(x_prompt, x_sample, state_C, state_n, state_m, c, c_ctx, norm_g, w_ada, b_ada, w_in, hy_conv_w, hy_conv_b, hy_w1, hy_b1, hy_w2, hy_b2, hy_w3, hy_b3, hy_freq, hy_decay, hy_bias, ml_conv_w, ml_conv_b, ml_if_b, ml_norm_g, w_pa, w_pb, w_out, final_g):
    depth, d, _ = w_in.shape
    d_hy = hy_bias.shape[-1]
    d_ml = ml_norm_g.shape[-1]
    heads = ml_if_b.shape[-1]
    bp, lp, _ = x_prompt.shape
    bs, ls, _ = x_sample.shape

    off_q = 4 * d_hy
    off_v = off_q + 2 * d_ml
    off_o = off_v + d_ml
    off_z = off_o + d_ml
    off_g = off_z + d_ml
    off_ga = off_g + 4 * heads
    seg_off = (off_q, off_q + d_ml, off_v, off_o, off_z)

    w_t = jnp.transpose(w_in, (0, 2, 1))
    wg = w_t[:, off_g:off_ga, :].reshape(depth, 4, heads, d).transpose(0, 2, 1, 3)
    wg_heads = jnp.zeros((depth, heads, LANES, d), BF16).at[:, :, :4, :].set(wg.astype(BF16))
    gb = ml_if_b.reshape(depth, 4, heads).transpose(0, 2, 1)
    gb_heads = jnp.zeros((depth, heads, 1, LANES), F32).at[:, :, 0, :4].set(gb)

    cvecs = jnp.zeros((8, d), F32).at[0].set(c_ctx).at[1:1 + bs].set(c)
    mods = _mods(cvecs, w_ada, b_ada).reshape(depth, 8, 3, d)

    xp = x_prompt.reshape(1, bp * lp, d)
    xs = x_sample
    cached = (state_C, state_n.reshape(bs, depth, 2, heads, 1, -1),
              jnp.broadcast_to(state_m[..., None, None], (bs, depth, 2, heads, 1, LANES)))
    fin = None
    hp = _norm_mod(xp, norm_g[0], mods[0, 0:1])
    hs = _norm_mod(xs, norm_g[0], mods[0, 1:1 + bs])
    dft = {L: _dft_operands(L) for L in sorted({lp, ls})}
    for l in range(depth):
        spectra = {}
        for L in sorted({lp, ls}):
            spectra[L] = _hyena_spectra(dft[L], l, hy_w1, hy_b1, hy_w2, hy_b2, hy_w3, hy_b3,
                                        hy_freq, hy_decay)

        last = l == depth - 1
        g_next = final_g if last else norm_g[l + 1]

        def layer(x, h, rows, bsz, L, states, carried):
            mod = mods[l, rows]
            hseq = h.reshape(bsz, L, d)
            ya = _hyena(hseq, l, w_t, hy_conv_w[l], hy_conv_b[l], spectra[L], hy_bias[l], dft[L])
            res = _mlstm(hseq, l, depth, w_t, wg_heads[l], gb_heads[l], ml_conv_w[l],
                         ml_conv_b[l], ml_norm_g[l], states, carried, seg_off)
            yb, fin = (res, None) if states is not None else (res[0], res[1:])
            res = _out(x, h, ya.reshape(x.shape), yb.reshape(x.shape), mod, l,
                       w_t, off_ga, w_pa, w_pb, w_out, g_next,
                       None if last else mods[l + 1, rows])
            return (res, None, fin) if last else (res[0], res[1], fin)

        xp, hp, fin = layer(xp, hp, slice(0, 1), bp, lp, None, fin)
        xs, hs, _ = layer(xs, hs, slice(1, 1 + bs), bs, ls, cached, None)

    return (xp.reshape(bp, lp, d), xs, fin[0], fin[1][:, :, :, :, 0, :], fin[2][:, :, :, :, 0, 0])
```

```python
import functools
import math

import numpy as np
import jax
import jax.numpy as jnp
from jax import lax
from jax.experimental import pallas as pl
from jax.experimental.pallas import tpu as pltpu

F32 = jnp.float32
BF16 = jnp.bfloat16
HIGHEST = lax.Precision.HIGHEST

HY_ORDER = 2
HY_BANDS = 16
HY_SHIFT = 0.05
ML_CHUNK = 128
EPS = 1e-6
NEG = -1e30

LANES = 128
SUBLANES = 8
VMEM_LIMIT = 56 * 1024 * 1024
HY_CB = 256
ML_ROWS = 1024
HY_ROWS = 2048
ML_STREAMS = 8
OUT_TM = 512


def _dot(a, b):
    return jnp.dot(a, b, preferred_element_type=F32)


def _dot_nt(a, bt):
    return lax.dot_general(a, bt, (((1,), (1,)), ((), ())), preferred_element_type=F32)


def _dot_hi(a, b):
    return jnp.dot(a, b, preferred_element_type=F32, precision=HIGHEST)


def _sigmoid(x):
    return 1.0 / (1.0 + jnp.exp(-x))


def _silu(x):
    return x * _sigmoid(x)


def _log_sigmoid(x):
    return jnp.minimum(x, 0.0) - jnp.log(1.0 + jnp.exp(-jnp.abs(x)))


def _params(*sem):
    return pltpu.CompilerParams(dimension_semantics=sem, vmem_limit_bytes=VMEM_LIMIT)


@functools.lru_cache(maxsize=None)
def _dft_table(L):
    N = 2 * L
    k = np.arange(L, dtype=np.int64)[:, None]
    n = np.arange(L, dtype=np.int64)[None, :]
    ang = 2.0 * np.pi * ((k * n) % N).astype(np.float64) / N
    c = np.cos(ang)
    s = -np.sin(ang)
    s[0, :] = 1.0 - 2.0 * (np.arange(L) % 2)
    return np.concatenate([c, s], axis=0).astype(np.float32)


def _dft_operands(L):
    f = jnp.asarray(_dft_table(L)).astype(BF16)
    return f, f.T


def _mod_kernel(c_ref, w_ref, b_ref, o_ref):
    o_ref[...] = _dot_hi(_silu(c_ref[...]), w_ref[...]) + b_ref[...]


def _mods(cvecs, w_ada, b_ada):
    depth, d, d3 = w_ada.shape
    r = cvecs.shape[0]
    tn = 512
    return pl.pallas_call(
        _mod_kernel,
        grid=(depth, d3 // tn),
        in_specs=[pl.BlockSpec((r, d), lambda l, j: (0, 0)),
                  pl.BlockSpec((None, d, tn), lambda l, j: (l, 0, j)),
                  pl.BlockSpec((None, 1, tn), lambda l, j: (l, 0, j))],
        out_specs=pl.BlockSpec((None, r, tn), lambda l, j: (l, 0, j)),
        out_shape=jax.ShapeDtypeStruct((depth, r, d3), F32),
        compiler_params=_params("arbitrary", "arbitrary"),
        name="adaln_mod",
    )(cvecs, w_ada, b_ada.reshape(depth, 1, d3))


def _norm_mod_kernel(x_ref, g_ref, mod_ref, o_ref):
    x = x_ref[...]
    y = x * lax.rsqrt(jnp.mean(x * x, axis=-1, keepdims=True) + EPS) * g_ref[...]
    o_ref[...] = (y * (1.0 + mod_ref[1:2, :]) + mod_ref[0:1, :]).astype(o_ref.dtype)


def _norm_mod(x, g, mod):
    bm, t, d = x.shape
    tm = min(t, 1024)
    return pl.pallas_call(
        _norm_mod_kernel,
        grid=(bm, t // tm),
        in_specs=[pl.BlockSpec((None, tm, d), lambda b, i: (b, i, 0)),
                  pl.BlockSpec((1, d), lambda b, i: (0, 0)),
                  pl.BlockSpec((None, 3, d), lambda b, i: (b, 0, 0))],
        out_specs=pl.BlockSpec((None, tm, d), lambda b, i: (b, i, 0)),
        out_shape=jax.ShapeDtypeStruct((bm, t, d), BF16),
        compiler_params=_params("arbitrary", "arbitrary"),
        name="norm_mod",
    )(x, g.reshape(1, d), mod)


def _filter_kernel(w1_ref, b1_ref, w2_ref, b2_ref, freq_ref, w3_00, w3_01, w3_10, w3_11,
                   b3_ref, decay_ref, f_ref, spec_ref, hdn_ref, *, L):
    cb = spec_ref.shape[-1]

    @pl.when(pl.program_id(0) == 0)
    def _():
        row = lax.broadcasted_iota(jnp.int32, (L, LANES), 0).astype(F32)
        lane = lax.broadcasted_iota(jnp.int32, (L, LANES), 1)
        t = row / L
        band = jnp.where(lane <= HY_BANDS, lane, lane - HY_BANDS).astype(F32)
        ang = 2.0 * math.pi * t * band
        feats = jnp.where(lane == 0, t,
                          jnp.where(lane <= HY_BANDS, jnp.cos(ang),
                                    jnp.where(lane <= 2 * HY_BANDS, jnp.sin(ang), 0.0)))
        hdn = jnp.sin(freq_ref[0:1, :] * (_dot_hi(feats, w1_ref[...]) + b1_ref[...]))
        hdn_ref[...] = jnp.sin(freq_ref[1:2, :] * (_dot_hi(hdn, w2_ref[...]) + b2_ref[...]))

    hdn = hdn_ref[...].astype(BF16)
    row_i = lax.broadcasted_iota(jnp.int32, (L, cb), 0)
    t = row_i.astype(F32) / L
    first = row_i == 0
    sign = jnp.where((row_i & 1) == 0, 1.0, -1.0)
    w3 = ((w3_00, w3_01), (w3_10, w3_11))
    for o in range(HY_ORDER):
        hs = []
        for dr in range(2):
            j = 2 * o + dr
            hv = _dot(hdn, w3[o][dr][...].astype(BF16)) + b3_ref[j:j + 1, :]
            win = jnp.exp(-t * jnp.abs(decay_ref[j:j + 1, :])) + HY_SHIFT
            hs.append(hv * win)
        hf = hs[0]
        hb = jnp.where(first, 0.0, hs[1])
        l1 = jnp.sum(jnp.abs(hf), axis=0, keepdims=True) + jnp.sum(jnp.abs(hb), axis=0, keepdims=True)
        inv_l1 = 1.0 / l1
        even = (hf + hb) * inv_l1
        odd = (hf - hb) * inv_l1
        re = _dot(f_ref[0:L, :], even.astype(BF16))
        im = _dot(f_ref[L:2 * L, :], odd.astype(BF16))
        nyq = jnp.sum(even * sign, axis=0, keepdims=True)
        a = re * jnp.where(first, 0.5 / L, 1.0 / L)
        spec_ref[o, 0] = a
        spec_ref[o, 1] = jnp.where(first, 0.0, im * (1.0 / L))
        spec_ref[o, 2] = jnp.where(first, nyq * (0.5 / L), a)


def _hyena_spectra(dft, l, hy_w1, hy_b1, hy_w2, hy_b2, hy_w3, hy_b3, hy_freq, hy_decay):
    emb, ff = hy_w1.shape[1:]
    d_hy = hy_decay.shape[-1]
    cb = HY_CB
    ncb = d_hy // cb
    w1p = jnp.zeros((LANES, ff), F32).at[:emb].set(hy_w1[l])
    f, _ = dft
    L = f.shape[1]
    const = lambda j: (0, 0)

    def w3_spec(o, dr):
        return pl.BlockSpec((None, ff, cb), lambda j, o=o, dr=dr: (l, 0, (2 * o + dr) * ncb + j))

    return pl.pallas_call(
        functools.partial(_filter_kernel, L=L),
        grid=(ncb,),
        in_specs=[pl.BlockSpec((LANES, ff), const),
                  pl.BlockSpec((1, ff), const),
                  pl.BlockSpec((ff, ff), const),
                  pl.BlockSpec((1, ff), const),
                  pl.BlockSpec((2, ff), const),
                  w3_spec(0, 0), w3_spec(0, 1), w3_spec(1, 0), w3_spec(1, 1),
                  pl.BlockSpec((2 * HY_ORDER, cb), lambda j: (0, j)),
                  pl.BlockSpec((2 * HY_ORDER, cb), lambda j: (0, j)),
                  pl.BlockSpec((2 * L, L), const)],
        out_specs=pl.BlockSpec((HY_ORDER, 3, L, cb), lambda j: (0, 0, 0, j)),
        out_shape=jax.ShapeDtypeStruct((HY_ORDER, 3, L, d_hy), F32),
        scratch_shapes=[pltpu.VMEM((L, ff), F32)],
        compiler_params=_params("arbitrary"),
        name="hyena_filter",
    )(w1p, hy_b1[l].reshape(1, ff), hy_w2[l], hy_b2[l].reshape(1, ff), hy_freq[l],
      hy_w3, hy_w3, hy_w3, hy_w3,
      hy_b3[l].reshape(2 * HY_ORDER, d_hy), hy_decay[l].reshape(2 * HY_ORDER, d_hy),
      f)


def _short_conv3(x, w_ref, b, seq=None):
    n, c = x.shape
    seq = n if seq is None else seq
    prev = pltpu.roll(x, 1, axis=0)
    nxt = pltpu.roll(x, n - 1, axis=0)
    row = lax.broadcasted_iota(jnp.int32, (SUBLANES, c), 0)
    prev_parts, nxt_parts = [], []
    for s in range(0, n, seq):
        e = s + seq
        prev_parts += [jnp.where(row == 0, 0.0, prev[s:s + SUBLANES]), prev[s + SUBLANES:e]]
        nxt_parts += [nxt[s:e - SUBLANES], jnp.where(row == SUBLANES - 1, 0.0, nxt[e - SUBLANES:e])]
    prev = jnp.concatenate(prev_parts, axis=0)
    nxt = jnp.concatenate(nxt_parts, axis=0)
    return ((b + prev * w_ref[0:1, :]) + x * w_ref[1:2, :]) + nxt * w_ref[2:3, :]


def _hyena_kernel(h_ref, wx1_ref, wx2_ref, wv_ref, wz_ref, cw_ref, cb_ref, spec_ref, hbias_ref,
                  f_ref, g_ref, o_ref, *, nb, L):
    seqs = range(nb)
    wv = wv_ref[...].astype(BF16)
    z = [_short_conv3(_dot_nt(h_ref[b], wv), cw_ref.at[2], cb_ref[2:3, :]) for b in seqs]
    gate_w = (wx1_ref, wx2_ref)
    for o in range(HY_ORDER):
        zf = [_dot(f_ref[...], z[b].astype(BF16)) for b in seqs]
        wg = gate_w[o][...].astype(BF16)
        gate = [_short_conv3(_dot_nt(h_ref[b], wg), cw_ref.at[o], cb_ref[o:o + 1, :]) for b in seqs]
        ycat = []
        for b in seqs:
            zc, zs = zf[b][0:L], zf[b][L:2 * L]
            a, bm, dm = spec_ref[o, 0], spec_ref[o, 1], spec_ref[o, 2]
            ycat.append(jnp.concatenate([zc * a - zs * bm, zc * bm + zs * dm], axis=0).astype(BF16))
        y = [_dot(g_ref[...], ycat[b]) for b in seqs]
        z = [gate[b] * (y[b] + hbias_ref[o:o + 1, :] * z[b]) for b in seqs]
    wz = wz_ref[...].astype(BF16)
    for b in seqs:
        o_ref[b] = (z[b] * _silu(_dot_nt(h_ref[b], wz))).astype(o_ref.dtype)


def _hyena(h, l, w_in, conv_w, conv_b, spec, hy_bias, dft):
    bsz, L, d = h.shape
    d_hy = hy_bias.shape[-1]
    cb = HY_CB
    ncb = d_hy // cb
    nb = max(1, min(bsz, HY_ROWS // L))
    f, g = dft
    once = pl.Buffered(1)

    def w_spec(seg):
        return pl.BlockSpec((None, cb, d), lambda j, i, seg=seg: (l, seg * ncb + j, 0))

    return pl.pallas_call(
        functools.partial(_hyena_kernel, nb=nb, L=L),
        grid=(ncb, bsz // nb),
        in_specs=[pl.BlockSpec((nb, L, d), lambda j, i: (i, 0, 0)),
                  w_spec(0), w_spec(1), w_spec(2), w_spec(3),
                  pl.BlockSpec((3, 3, cb), lambda j, i: (0, 0, j)),
                  pl.BlockSpec((3, cb), lambda j, i: (0, j)),
                  pl.BlockSpec((HY_ORDER, 3, L, cb), lambda j, i: (0, 0, 0, j), pipeline_mode=once),
                  pl.BlockSpec((HY_ORDER, cb), lambda j, i: (0, j)),
                  pl.BlockSpec((2 * L, L), lambda j, i: (0, 0), pipeline_mode=once),
                  pl.BlockSpec((L, 2 * L), lambda j, i: (0, 0), pipeline_mode=once)],
        out_specs=pl.BlockSpec((nb, L, cb), lambda j, i: (i, 0, j)),
        out_shape=jax.ShapeDtypeStruct((bsz, L, d_hy), BF16),
        compiler_params=_params("arbitrary", "arbitrary"),
        name="hyena",
    )(h, w_in, w_in, w_in, w_in,
      conv_w.reshape(3, 3, d_hy).transpose(1, 0, 2), conv_b.reshape(3, d_hy), spec, hy_bias, f, g)


def _mlstm_kernel(*refs, nb, L, has_state, n_aliased):
    (h_ref, wq_ref, wk_ref, wv_ref, wo_ref, wz_ref, wg_ref, gb_ref, cwq_ref, cwk_ref,
     cbq_ref, cbk_ref, ng_ref) = refs[:13]
    refs = refs[13:]
    if has_state:
        c0_ref, n0_ref, m0_ref, yb_ref = refs[:4]
        refs = refs[4:]
    else:
        yb_ref, cn_ref, nn_ref, mn_ref = refs[n_aliased:n_aliased + 4]
        refs = refs[n_aliased + 4:]
    q_s, kt_s, v_s, g2_s, g2t_s, br_s, sc_s, hsum_s, og_s, c_s = refs

    T = ML_CHUNK
    nc = L // T
    R = nb * L
    d = h_ref.shape[-1]
    dh = q_s.shape[-1]
    hb = h_ref[...].reshape(R, d)

    g = _dot_nt(hb, wg_ref[...]) + gb_ref[...]
    lf = _log_sigmoid(g)
    ti = lax.broadcasted_iota(jnp.int32, (T, T), 0)
    si = lax.broadcasted_iota(jnp.int32, (T, T), 1)
    causal = si <= ti
    anti = si >= ti
    lower = causal.astype(F32)
    upper = anti.astype(F32)
    lane = lax.broadcasted_iota(jnp.int32, (T, LANES), 1)
    for j in range(nb * nc):
        rows = slice(j * T, (j + 1) * T)
        pre = _dot_hi(lower, lf[rows])
        suf = _dot_hi(upper, lf[rows])
        g2_s[rows, :] = jnp.where(lane == 1, pre, jnp.where(lane == 3, suf, g[rows]))
        br_s[0, rows, :] = jnp.broadcast_to(pre[:, 1:2], (T, LANES))
        br_s[1, rows, :] = jnp.broadcast_to(suf[:, 3:4], (T, LANES))
    g2t_s[...] = g2_s[...].T[0:8, :]

    zero11 = jnp.zeros((1, 1), F32)
    m_fin = []
    for b in range(nb):
        vals = [[None] * 6 for _ in range(nc)]
        for dr in range(2):
            m = m0_ref[b, dr][:, 0:1] if has_state else zero11
            for ci in range(nc):
                c = ci if dr == 0 else nc - 1 - ci
                cols = slice(b * L + c * T, b * L + (c + 1) * T)
                irow = g2t_s[2 * dr:2 * dr + 1, cols]
                brow = g2t_s[2 * dr + 1:2 * dr + 2, cols]
                bl = brow[:, T - 1:T] if dr == 0 else brow[:, 0:1]
                mw = jnp.max(bl - brow + irow, axis=-1, keepdims=True)
                mn = jnp.maximum(bl + m, mw)
                vals[c][3 * dr:3 * dr + 3] = [m, mn, bl]
                m = mn
            m_fin.append(m)
        for c in range(nc):
            rows8 = [jnp.broadcast_to(v, (1, LANES)) for v in vals[c]] + [jnp.zeros((2, LANES), F32)]
            sc_s[b * nc + c] = jnp.concatenate(rows8, axis=0)

    uq =_dot_nt(hb, wq_ref[...].astype(BF16))
    uk = _dot_nt(hb, wk_ref[...].astype(BF16))
    q_s[...] = _silu(_short_conv3(uq, cwq_ref, cbq_ref[...], L)).astype(BF16)
    uv = _dot_nt(hb, wv_ref[...].astype(BF16))
    k = _silu(_short_conv3(uk, cwk_ref, cbk_ref[...], L)) * (dh ** -0.5)
    kt_s[...] = k.T.astype(BF16)
    uo = _dot_nt(hb, wo_ref[...].astype(BF16))
    v_s[:, 0:dh] = uv.astype(BF16)
    v_s[:, dh:dh + LANES] = jnp.ones((R, LANES), BF16)
    uz = _dot_nt(hb, wz_ref[...].astype(BF16))
    og_s[...] = _sigmoid(uo) * _silu(uz)

    hsum_s[...] = jnp.zeros_like(hsum_s)
    for b in range(nb):
        for dr in range(2):
            if has_state:
                c_s[2 * b + dr, :, 0:dh] = c0_ref[b, dr]
                c_s[2 * b + dr, :, dh:dh + LANES] = jnp.broadcast_to(n0_ref[b, dr], (LANES, dh)).T
            else:
                c_s[2 * b + dr] = jnp.zeros((dh, dh + LANES), F32)

    def step(ci, carry):
        for b in range(nb):
            for dr in range(2):
                c = ci if dr == 0 else nc - 1 - ci
                r0 = pl.multiple_of(b * L + c * T, T)
                qc = q_s[pl.ds(r0, T), :]
                vc = v_s[pl.ds(r0, T), :]
                ktc = kt_s[:, pl.ds(r0, T)]
                b_rep = br_s[dr, pl.ds(r0, T), :]
                g2tc = g2t_s[:, pl.ds(r0, T)]
                irow = g2tc[2 * dr:2 * dr + 1, :]
                brow = g2tc[2 * dr + 1:2 * dr + 2, :]
                sc = sc_s[b * nc + c]
                m_prev = sc[3 * dr:3 * dr + 1, :]
                m_new = sc[3 * dr + 1:3 * dr + 2, :]
                bl = sc[3 * dr + 2:3 * dr + 3, :]
                c_old = c_s[2 * b + dr]

                dlog = jnp.where(causal if dr == 0 else anti, b_rep - brow + irow, NEG)
                inter = b_rep + m_prev
                mt = jnp.maximum(inter, jnp.max(dlog, axis=-1, keepdims=True))
                smat = _dot(qc, ktc) * jnp.exp(dlog - mt)
                iw = jnp.exp(inter - mt)
                ext = (_dot(smat.astype(BF16), vc)
                       + jnp.concatenate([iw, iw, iw], axis=-1) * _dot(qc, c_old.astype(BF16)))
                inv = 1.0 / jnp.maximum(jnp.abs(ext[:, dh:dh + LANES]), jnp.exp(-mt))
                hh = ext[:, 0:dh] * jnp.concatenate([inv, inv], axis=-1)
                hsum_s[pl.ds(r0, T), :] = hsum_s[pl.ds(r0, T), :] + hh

                ws = jnp.exp(bl - brow + irow - m_new)
                dec = jnp.exp(bl + m_prev - m_new)[:, 0:1]
                kw = (ktc.astype(F32) * ws).astype(BF16)
                c_s[2 * b + dr] = dec * c_old + _dot(kw, vc)
        return carry

    lax.fori_loop(0, nc, step, 0, unroll=max(1, min(nc, ML_STREAMS // (2 * nb))))
    if not has_state:
        for b in range(nb):
            for dr in range(2):
                cn_ref[b, dr] = c_s[2 * b + dr, :, 0:dh]
                nn_ref[b, dr] = c_s[2 * b + dr, :, dh:dh + LANES].T[0:1, :]
                mn_ref[b, dr] = jnp.broadcast_to(m_fin[2 * b + dr], (1, LANES))

    hs = hsum_s[...]
    hm = hs * lax.rsqrt(jnp.mean(hs * hs, axis=-1, keepdims=True) + EPS) * ng_ref[...]
    yb_ref[...] = (hm * og_s[...]).astype(yb_ref.dtype).reshape(yb_ref.shape)


def _mlstm(h, l, depth, w_in, wg_heads, gb_heads, conv_w, conv_b, norm_g, states, carried,
           seg_off):
    bsz, L, d = h.shape
    heads = wg_heads.shape[0]
    d_ml = norm_g.shape[-1]
    dh = d_ml // heads
    has_state = states is not None
    nb = max(1, min(bsz, ML_ROWS // L))
    nc = L // ML_CHUNK
    qo, ko, vo, oo, zo = (s // dh for s in seg_off)

    def w_spec(off):
        return pl.BlockSpec((None, dh, d), lambda b, hd, off=off: (l, off + hd, 0))

    def vec_spec(rows, off):
        return pl.BlockSpec((rows, dh), lambda b, hd, off=off: (0, off + hd))

    in_specs = [pl.BlockSpec((nb, L, d), lambda b, hd: (b, 0, 0)),
                w_spec(qo), w_spec(ko), w_spec(vo), w_spec(oo), w_spec(zo),
                pl.BlockSpec((None, LANES, d), lambda b, hd: (hd, 0, 0)),
                pl.BlockSpec((None, 1, LANES), lambda b, hd: (hd, 0, 0)),
                vec_spec(3, 0), vec_spec(3, heads), vec_spec(1, 0), vec_spec(1, heads),
                vec_spec(1, 0)]
    args = [h, w_in, w_in, w_in, w_in, w_in, wg_heads, gb_heads,
            conv_w, conv_w, conv_b.reshape(1, -1), conv_b.reshape(1, -1), norm_g.reshape(1, -1)]
    yb_spec = pl.BlockSpec((nb, L, dh), lambda b, hd: (b, 0, hd))
    yb_shape = jax.ShapeDtypeStruct((bsz, L, d_ml), BF16)
    state_idx = lambda b, hd: (b, l, 0, hd, 0, 0)
    c_spec = pl.BlockSpec((nb, None, 2, None, dh, dh), state_idx)
    n_spec = pl.BlockSpec((nb, None, 2, None, 1, dh), state_idx)
    m_spec = pl.BlockSpec((nb, None, 2, None, 1, LANES), state_idx)
    aliases = {}
    if has_state:
        in_specs += [c_spec, n_spec, m_spec]
        args += list(states)
        out_specs, out_shape = yb_spec, yb_shape
    else:
        if carried is not None:
            aliases = {len(args) + i: 1 + i for i in range(3)}
            in_specs += [pl.BlockSpec(memory_space=pl.ANY)] * 3
            args += list(carried)
        out_specs = (yb_spec, c_spec, n_spec, m_spec)
        out_shape = (yb_shape,
                     jax.ShapeDtypeStruct((bsz, depth, 2, heads, dh, dh), F32),
                     jax.ShapeDtypeStruct((bsz, depth, 2, heads, 1, dh), F32),
                     jax.ShapeDtypeStruct((bsz, depth, 2, heads, 1, LANES), F32))
    rows = nb * L
    scratch = [pltpu.VMEM((rows, dh), BF16),
               pltpu.VMEM((dh, rows), BF16),
               pltpu.VMEM((rows, dh + LANES), BF16),
               pltpu.VMEM((rows, LANES), F32),
               pltpu.VMEM((8, rows), F32),
               pltpu.VMEM((2, rows, LANES), F32),
               pltpu.VMEM((nb * nc, 8, LANES), F32),
               pltpu.VMEM((rows, dh), F32),
               pltpu.VMEM((rows, dh), F32),
               pltpu.VMEM((2 * nb, dh, dh + LANES), F32)]
    return pl.pallas_call(
        functools.partial(_mlstm_kernel, nb=nb, L=L, has_state=has_state, n_aliased=len(aliases)),
        grid=(bsz // nb, heads),
        in_specs=in_specs,
        out_specs=out_specs,
        out_shape=out_shape,
        input_output_aliases=aliases,
        scratch_shapes=scratch,
        compiler_params=_params("arbitrary", "arbitrary"),
        name="mlstm",
    )(*args)


def _out_kernel(x_ref, h_ref, ya_ref, yb_ref, mod_ref, wga_ref, wgb_ref, wpa_ref, wpb_ref,
                wout_ref, g_ref, *rest, last):
    w_s = rest[-1]
    rest = rest[:-1]

    @pl.when((pl.program_id(0) == 0) & (pl.program_id(1) == 0))
    def _():
        w_s[0] = wga_ref[0].astype(BF16)
        w_s[1] = wgb_ref[0].astype(BF16)
        w_s[2] = wpa_ref[...].astype(BF16)
        w_s[3] = wpb_ref[...].astype(BF16)
        w_s[4] = wout_ref[...].astype(BF16)

    hb = h_ref[...]
    ga = _sigmoid(_dot_nt(hb, w_s[0]))
    gb = _sigmoid(_dot_nt(hb, w_s[1]))
    merged = ga * _dot(ya_ref[...], w_s[2]) + gb * _dot(yb_ref[...], w_s[3])
    xn = x_ref[...] + mod_ref[2:3, :] * _dot(merged.astype(BF16), w_s[4])
    normed = xn * lax.rsqrt(jnp.mean(xn * xn, axis=-1, keepdims=True) + EPS) * g_ref[...]
    if last:
        (y_ref,) = rest
        y_ref[...] = normed
    else:
        modn_ref, xo_ref, ho_ref = rest
        xo_ref[...] = xn
        ho_ref[...] = (normed * (1.0 + modn_ref[1:2, :]) + modn_ref[0:1, :]).astype(ho_ref.dtype)


def _out(x, h, ya, yb, mod, l, w_t, off_ga, w_pa, w_pb, w_out, g_next, mod_next):
    bm, t, d = x.shape
    tm = min(t, OUT_TM)
    last = mod_next is None
    tok = lambda b, i: (b, i, 0)
    wl = lambda b, i: (l, 0, 0)
    tok_spec = pl.BlockSpec((None, tm, d), tok)
    mod_spec = pl.BlockSpec((None, 3, d), lambda b, i: (b, 0, 0))
    gate_blk = (pl.Element(1), pl.Element(d), pl.Element(d))
    once = pl.Buffered(1)
    in_specs = [tok_spec, tok_spec, tok_spec, tok_spec, mod_spec,
                pl.BlockSpec(gate_blk, lambda b, i: (l, off_ga, 0), pipeline_mode=once),
                pl.BlockSpec(gate_blk, lambda b, i: (l, off_ga + d, 0), pipeline_mode=once),
                pl.BlockSpec((None, d, d), wl, pipeline_mode=once),
                pl.BlockSpec((None, d, d), wl, pipeline_mode=once),
                pl.BlockSpec((None, d, d), wl, pipeline_mode=once),
                pl.BlockSpec((1, d), lambda b, i: (0, 0))]
    args = [x, h, ya, yb, mod, w_t, w_t, w_pa, w_pb, w_out, g_next.reshape(1, d)]
    if last:
        out_specs, out_shape = tok_spec, jax.ShapeDtypeStruct((bm, t, d), F32)
    else:
        in_specs.append(mod_spec)
        args.append(mod_next)
        out_specs = (tok_spec, tok_spec)
        out_shape = (jax.ShapeDtypeStruct((bm, t, d), F32), jax.ShapeDtypeStruct((bm, t, d), BF16))
    return pl.pallas_call(
        functools.partial(_out_kernel, last=last),
        grid=(bm, t // tm),
        in_specs=in_specs,
        out_specs=out_specs,
        out_shape=out_shape,
        scratch_shapes=[pltpu.VMEM((5, d, d), BF16)],
        compiler_params=_params("arbitrary", "arbitrary"),
        name="merge_out",
    )(*args)


def kernel(x_prompt, x_sample, state_C, state_n, state_m, c, c_ctx, norm_g, w_ada, b_ada, w_in, hy_conv_w, hy_conv_b, hy_w1, hy_b1, hy_w2, hy_b2, hy_w3, hy_b3, hy_freq, hy_decay, hy_bias, ml_conv_w, ml_conv_b, ml_if_b, ml_norm_g, w_pa, w_pb, w_out, final_g):
    depth, d, _ = w_in.shape
    d_hy = hy_bias.shape[-1]
    d_ml = ml_norm_g.shape[-1]
    heads = ml_if_b.shape[-1]
    bp, lp, _ = x_prompt.shape
    bs, ls, _ = x_sample.shape

    off_q = 4 * d_hy
    off_v = off_q + 2 * d_ml
    off_o = off_v + d_ml
    off_z = off_o + d_ml
    off_g = off_z + d_ml
    off_ga = off_g + 4 * heads
    seg_off = (off_q, off_q + d_ml, off_v, off_o, off_z)

    w_t = jnp.transpose(w_in, (0, 2, 1))
    wg = w_t[:, off_g:off_ga, :].reshape(depth, 4, heads, d).transpose(0, 2, 1, 3)
    wg_heads = jnp.zeros((depth, heads, LANES, d), BF16).at[:, :, :4, :].set(wg.astype(BF16))
    gb = ml_if_b.reshape(depth, 4, heads).transpose(0, 2, 1)
    gb_heads = jnp.zeros((depth, heads, 1, LANES), F32).at[:, :, 0, :4].set(gb)

    cvecs = jnp.zeros((8, d), F32).at[0].set(c_ctx).at[1:1 + bs].set(c)
    mods = _mods(cvecs, w_ada, b_ada).reshape(depth, 8, 3, d)

    xp = x_prompt.reshape(1, bp * lp, d)
    xs = x_sample
    cached = (state_C, state_n.reshape(bs, depth, 2, heads, 1, -1),
              jnp.broadcast_to(state_m[..., None, None], (bs, depth, 2, heads, 1, LANES)))
    fin = None
    hp = _norm_mod(xp, norm_g[0], mods[0, 0:1])
    hs = _norm_mod(xs, norm_g[0], mods[0, 1:1 + bs])
    dft = {L: _dft_operands(L) for L in sorted({lp, ls})}
    for l in range(depth):
        spectra = {}
        for L in sorted({lp, ls}):
            spectra[L] = _hyena_spectra(dft[L], l, hy_w1, hy_b1, hy_w2, hy_b2, hy_w3, hy_b3,
                                        hy_freq, hy_decay)

        last = l == depth - 1
        g_next = final_g if last else norm_g[l + 1]

        def layer(x, h, rows, bsz, L, states, carried):
            mod = mods[l, rows]
            hseq = h.reshape(bsz, L, d)
            ya = _hyena(hseq, l, w_t, hy_conv_w[l], hy_conv_b[l], spectra[L], hy_bias[l], dft[L])
            res = _mlstm(hseq, l, depth, w_t, wg_heads[l], gb_heads[l], ml_conv_w[l],
                         ml_conv_b[l], ml_norm_g[l], states, carried, seg_off)
            yb, fin = (res, None) if states is not None else (res[0], res[1:])
            res = _out(x, h, ya.reshape(x.shape), yb.reshape(x.shape), mod, l,
                       w_t, off_ga, w_pa, w_pb, w_out, g_next,
                       None if last else mods[l + 1, rows])
            return (res, None, fin) if last else (res[0], res[1], fin)

        xp, hp, fin = layer(xp, hp, slice(0, 1), bp, lp, None, fin)
        xs, hs, _ = layer(xs, hs, slice(1, 1 + bs), bs, ls, cached, None)

    return (xp.reshape(bp, lp, d), xs, fin[0], fin[1][:, :, :, :, 0, :], fin[2][:, :, :, :, 0, 0])
```

```python
import functools
import math

import numpy as np
import jax
import jax.numpy as jnp
from jax import lax
from jax.experimental import pallas as pl
from jax.experimental.pallas import tpu as pltpu

F32 = jnp.float32
BF16 = jnp.bfloat16
HIGHEST = lax.Precision.HIGHEST

HY_ORDER = 2
HY_BANDS = 16
HY_SHIFT = 0.05
ML_CHUNK = 128
EPS = 1e-6
NEG = -1e30

LANES = 128
SUBLANES = 8
VMEM_LIMIT = 56 * 1024 * 1024
HY_CB = 256
ML_ROWS = 2048
HY_ROWS = 2048
HY_ROWS_SHORT = 4096
HY_SHORT_SEQ = 512
ML_STREAMS = 8
OUT_TM = 512


def _dot(a, b):
    return jnp.dot(a, b, preferred_element_type=F32)


def _dot_nt(a, bt):
    return lax.dot_general(a, bt, (((1,), (1,)), ((), ())), preferred_element_type=F32)


def _dot_hi(a, b):
    return jnp.dot(a, b, preferred_element_type=F32, precision=HIGHEST)


def _sigmoid(x):
    return 1.0 / (1.0 + jnp.exp(-x))


def _silu(x):
    return x * _sigmoid(x)


def _log_sigmoid(x):
    return jnp.minimum(x, 0.0) - jnp.log(1.0 + jnp.exp(-jnp.abs(x)))


def _params(*sem):
    return pltpu.CompilerParams(dimension_semantics=sem, vmem_limit_bytes=VMEM_LIMIT)


@functools.lru_cache(maxsize=None)
def _dft_table(L):
    N = 2 * L
    k = np.arange(L, dtype=np.int64)[:, None]
    n = np.arange(L, dtype=np.int64)[None, :]
    ang = 2.0 * np.pi * ((k * n) % N).astype(np.float64) / N
    c = np.cos(ang)
    s = -np.sin(ang)
    s[0, :] = 1.0 - 2.0 * (np.arange(L) % 2)
    return np.concatenate([c, s], axis=0).astype(np.float32)


def _dft_operands(L):
    f = jnp.asarray(_dft_table(L)).astype(BF16)
    return f, f.T


def _split_bf16(x):
    hi = x.astype(BF16)
    return hi, (x - hi.astype(F32)).astype(BF16)


def _mod_kernel(c_ref, w_ref, b_ref, o_ref):
    a_hi, a_lo = _split_bf16(_silu(c_ref[...]))
    w_hi, w_lo = _split_bf16(w_ref[...])
    o_ref[...] = (_dot(a_hi, w_hi) + (_dot(a_lo, w_hi) + _dot(a_hi, w_lo))) + b_ref[...]


def _mods(cvecs, w_ada, b_ada):
    depth, d, d3 = w_ada.shape
    r = cvecs.shape[0]
    tn = 1024
    return pl.pallas_call(
        _mod_kernel,
        grid=(depth, d3 // tn),
        in_specs=[pl.BlockSpec((r, d), lambda l, j: (0, 0)),
                  pl.BlockSpec((None, d, tn), lambda l, j: (l, 0, j)),
                  pl.BlockSpec((None, 1, tn), lambda l, j: (l, 0, j))],
        out_specs=pl.BlockSpec((None, r, tn), lambda l, j: (l, 0, j)),
        out_shape=jax.ShapeDtypeStruct((depth, r, d3), F32),
        compiler_params=_params("arbitrary", "arbitrary"),
        name="adaln_mod",
    )(cvecs, w_ada, b_ada.reshape(depth, 1, d3))


def _norm_mod_kernel(x_ref, g_ref, mod_ref, o_ref):
    x = x_ref[...]
    y = x * lax.rsqrt(jnp.mean(x * x, axis=-1, keepdims=True) + EPS) * g_ref[...]
    o_ref[...] = (y * (1.0 + mod_ref[1:2, :]) + mod_ref[0:1, :]).astype(o_ref.dtype)


def _norm_mod(x, g, mod):
    bm, t, d = x.shape
    tm = min(t, 2048)
    return pl.pallas_call(
        _norm_mod_kernel,
        grid=(bm, t // tm),
        in_specs=[pl.BlockSpec((None, tm, d), lambda b, i: (b, i, 0)),
                  pl.BlockSpec((1, d), lambda b, i: (0, 0)),
                  pl.BlockSpec((None, 3, d), lambda b, i: (b, 0, 0))],
        out_specs=pl.BlockSpec((None, tm, d), lambda b, i: (b, i, 0)),
        out_shape=jax.ShapeDtypeStruct((bm, t, d), BF16),
        compiler_params=_params("arbitrary", "arbitrary"),
        name="norm_mod",
    )(x, g.reshape(1, d), mod)


def _filter_kernel(w1_ref, b1_ref, w2_ref, b2_ref, freq_ref, w3_00, w3_01, w3_10, w3_11,
                   b3_ref, decay_ref, f_ref, spec_ref, hdn_ref, *, L):
    cb = spec_ref.shape[-1]

    @pl.when(pl.program_id(0) == 0)
    def _():
        row = lax.broadcasted_iota(jnp.int32, (L, LANES), 0).astype(F32)
        lane = lax.broadcasted_iota(jnp.int32, (L, LANES), 1)
        t = row / L
        band = jnp.where(lane <= HY_BANDS, lane, lane - HY_BANDS).astype(F32)
        ang = 2.0 * math.pi * t * band
        feats = jnp.where(lane == 0, t,
                          jnp.where(lane <= HY_BANDS, jnp.cos(ang),
                                    jnp.where(lane <= 2 * HY_BANDS, jnp.sin(ang), 0.0)))
        hdn = jnp.sin(freq_ref[0:1, :] * (_dot_hi(feats, w1_ref[...]) + b1_ref[...]))
        hdn_ref[...] = jnp.sin(freq_ref[1:2, :] * (_dot_hi(hdn, w2_ref[...]) + b2_ref[...]))

    hdn = hdn_ref[...].astype(BF16)
    row_i = lax.broadcasted_iota(jnp.int32, (L, cb), 0)
    t = row_i.astype(F32) / L
    first = row_i == 0
    sign = jnp.where((row_i & 1) == 0, 1.0, -1.0)
    w3 = ((w3_00, w3_01), (w3_10, w3_11))
    for o in range(HY_ORDER):
        hs = []
        for dr in range(2):
            j = 2 * o + dr
            hv = _dot(hdn, w3[o][dr][...].astype(BF16)) + b3_ref[j:j + 1, :]
            win = jnp.exp(-t * jnp.abs(decay_ref[j:j + 1, :])) + HY_SHIFT
            hs.append(hv * win)
        hf = hs[0]
        hb = jnp.where(first, 0.0, hs[1])
        l1 = jnp.sum(jnp.abs(hf), axis=0, keepdims=True) + jnp.sum(jnp.abs(hb), axis=0, keepdims=True)
        inv_l1 = 1.0 / l1
        even = (hf + hb) * inv_l1
        odd = (hf - hb) * inv_l1
        re = _dot(f_ref[0:L, :], even.astype(BF16))
        im = _dot(f_ref[L:2 * L, :], odd.astype(BF16))
        nyq = jnp.sum(even * sign, axis=0, keepdims=True)
        a = re * jnp.where(first, 0.5 / L, 1.0 / L)
        spec_ref[o, 0] = a
        spec_ref[o, 1] = jnp.where(first, 0.0, im * (1.0 / L))
        spec_ref[o, 2] = jnp.where(first, nyq * (0.5 / L), a)


def _hyena_spectra(dft, l, hy_w1, hy_b1, hy_w2, hy_b2, hy_w3, hy_b3, hy_freq, hy_decay):
    emb, ff = hy_w1.shape[1:]
    d_hy = hy_decay.shape[-1]
    cb = HY_CB
    ncb = d_hy // cb
    w1p = jnp.zeros((LANES, ff), F32).at[:emb].set(hy_w1[l])
    f, _ = dft
    L = f.shape[1]
    const = lambda j: (0, 0)

    def w3_spec(o, dr):
        return pl.BlockSpec((None, ff, cb), lambda j, o=o, dr=dr: (l, 0, (2 * o + dr) * ncb + j))

    return pl.pallas_call(
        functools.partial(_filter_kernel, L=L),
        grid=(ncb,),
        in_specs=[pl.BlockSpec((LANES, ff), const),
                  pl.BlockSpec((1, ff), const),
                  pl.BlockSpec((ff, ff), const),
                  pl.BlockSpec((1, ff), const),
                  pl.BlockSpec((2, ff), const),
                  w3_spec(0, 0), w3_spec(0, 1), w3_spec(1, 0), w3_spec(1, 1),
                  pl.BlockSpec((2 * HY_ORDER, cb), lambda j: (0, j)),
                  pl.BlockSpec((2 * HY_ORDER, cb), lambda j: (0, j)),
                  pl.BlockSpec((2 * L, L), const)],
        out_specs=pl.BlockSpec((HY_ORDER, 3, L, cb), lambda j: (0, 0, 0, j)),
        out_shape=jax.ShapeDtypeStruct((HY_ORDER, 3, L, d_hy), F32),
        scratch_shapes=[pltpu.VMEM((L, ff), F32)],
        compiler_params=_params("arbitrary"),
        name="hyena_filter",
    )(w1p, hy_b1[l].reshape(1, ff), hy_w2[l], hy_b2[l].reshape(1, ff), hy_freq[l],
      hy_w3, hy_w3, hy_w3, hy_w3,
      hy_b3[l].reshape(2 * HY_ORDER, d_hy), hy_decay[l].reshape(2 * HY_ORDER, d_hy),
      f)


def _short_conv3(x, w_ref, b, seq=None):
    n, c = x.shape
    seq = n if seq is None else seq
    prev = pltpu.roll(x, 1, axis=0)
    nxt = pltpu.roll(x, n - 1, axis=0)
    row = lax.broadcasted_iota(jnp.int32, (SUBLANES, c), 0)
    prev_parts, nxt_parts = [], []
    for s in range(0, n, seq):
        e = s + seq
        prev_parts += [jnp.where(row == 0, 0.0, prev[s:s + SUBLANES]), prev[s + SUBLANES:e]]
        nxt_parts += [nxt[s:e - SUBLANES], jnp.where(row == SUBLANES - 1, 0.0, nxt[e - SUBLANES:e])]
    prev = jnp.concatenate(prev_parts, axis=0)
    nxt = jnp.concatenate(nxt_parts, axis=0)
    return ((b + prev * w_ref[0:1, :]) + x * w_ref[1:2, :]) + nxt * w_ref[2:3, :]


def _hyena_kernel(h_ref, wx1_ref, wx2_ref, wv_ref, wz_ref, cw_ref, cb_ref, spec_ref, hbias_ref,
                  f_ref, g_ref, o_ref, *, nb, L):
    seqs = range(nb)
    wv = wv_ref[...].astype(BF16)
    z = [_short_conv3(_dot_nt(h_ref[b], wv), cw_ref.at[2], cb_ref[2:3, :]) for b in seqs]
    gate_w = (wx1_ref, wx2_ref)
    for o in range(HY_ORDER):
        zf = [_dot(f_ref[...], z[b].astype(BF16)) for b in seqs]
        wg = gate_w[o][...].astype(BF16)
        gate = [_short_conv3(_dot_nt(h_ref[b], wg), cw_ref.at[o], cb_ref[o:o + 1, :]) for b in seqs]
        ycat = []
        for b in seqs:
            zc, zs = zf[b][0:L], zf[b][L:2 * L]
            a, bm, dm = spec_ref[o, 0], spec_ref[o, 1], spec_ref[o, 2]
            ycat.append(jnp.concatenate([zc * a - zs * bm, zc * bm + zs * dm], axis=0).astype(BF16))
        y = [_dot(g_ref[...], ycat[b]) for b in seqs]
        z = [gate[b] * (y[b] + hbias_ref[o:o + 1, :] * z[b]) for b in seqs]
    wz = wz_ref[...].astype(BF16)
    for b in seqs:
        o_ref[b] = (z[b] * _silu(_dot_nt(h_ref[b], wz))).astype(o_ref.dtype)


def _hyena(h, l, w_in, conv_w, conv_b, spec, hy_bias, dft):
    bsz, L, d = h.shape
    d_hy = hy_bias.shape[-1]
    cb = HY_CB
    ncb = d_hy // cb
    nb = max(1, min(bsz, (HY_ROWS_SHORT if L <= HY_SHORT_SEQ else HY_ROWS) // L))
    f, g = dft
    once = pl.Buffered(1)

    def w_spec(seg):
        return pl.BlockSpec((None, cb, d), lambda j, i, seg=seg: (l, seg * ncb + j, 0))

    return pl.pallas_call(
        functools.partial(_hyena_kernel, nb=nb, L=L),
        grid=(ncb, bsz // nb),
        in_specs=[pl.BlockSpec((nb, L, d), lambda j, i: (i, 0, 0)),
                  w_spec(0), w_spec(1), w_spec(2), w_spec(3),
                  pl.BlockSpec((3, 3, cb), lambda j, i: (0, 0, j)),
                  pl.BlockSpec((3, cb), lambda j, i: (0, j)),
                  pl.BlockSpec((HY_ORDER, 3, L, cb), lambda j, i: (0, 0, 0, j), pipeline_mode=once),
                  pl.BlockSpec((HY_ORDER, cb), lambda j, i: (0, j)),
                  pl.BlockSpec((2 * L, L), lambda j, i: (0, 0), pipeline_mode=once),
                  pl.BlockSpec((L, 2 * L), lambda j, i: (0, 0), pipeline_mode=once)],
        out_specs=pl.BlockSpec((nb, L, cb), lambda j, i: (i, 0, j)),
        out_shape=jax.ShapeDtypeStruct((bsz, L, d_hy), BF16),
        compiler_params=_params("arbitrary", "arbitrary"),
        name="hyena",
    )(h, w_in, w_in, w_in, w_in,
      conv_w.reshape(3, 3, d_hy).transpose(1, 0, 2), conv_b.reshape(3, d_hy), spec, hy_bias, f, g)


def _mlstm_kernel(*refs, nb, L, has_state, n_aliased):
    (h_ref, wq_ref, wk_ref, wv_ref, wo_ref, wz_ref, wg_ref, gb_ref, cwq_ref, cwk_ref,
     cbq_ref, cbk_ref, ng_ref) = refs[:13]
    refs = refs[13:]
    if has_state:
        c0_ref, n0_ref, m0_ref, yb_ref = refs[:4]
        refs = refs[4:]
    else:
        yb_ref, cn_ref, nn_ref, mn_ref = refs[n_aliased:n_aliased + 4]
        refs = refs[n_aliased + 4:]
    q_s, kt_s, v_s, g2_s, g2t_s, br_s, sc_s, hsum_s, og_s, c_s = refs

    T = ML_CHUNK
    nc = L // T
    R = nb * L
    d = h_ref.shape[-1]
    dh = q_s.shape[-1]
    hb = h_ref[...].reshape(R, d)

    g = _dot_nt(hb, wg_ref[...]) + gb_ref[...]
    lf = _log_sigmoid(g)
    ti = lax.broadcasted_iota(jnp.int32, (T, T), 0)
    si = lax.broadcasted_iota(jnp.int32, (T, T), 1)
    causal = si <= ti
    anti = si >= ti
    lower = causal.astype(F32)
    upper = anti.astype(F32)
    lane = lax.broadcasted_iota(jnp.int32, (T, LANES), 1)
    for j in range(nb * nc):
        rows = slice(j * T, (j + 1) * T)
        pre = _dot_hi(lower, lf[rows])
        suf = _dot_hi(upper, lf[rows])
        g2_s[rows, :] = jnp.where(lane == 1, pre, jnp.where(lane == 3, suf, g[rows]))
        br_s[0, rows, :] = jnp.broadcast_to(pre[:, 1:2], (T, LANES))
        br_s[1, rows, :] = jnp.broadcast_to(suf[:, 3:4], (T, LANES))
    g2t_s[...] = g2_s[...].T[0:8, :]

    zero11 = jnp.zeros((1, 1), F32)
    m_fin = []
    for b in range(nb):
        vals = [[None] * 6 for _ in range(nc)]
        for dr in range(2):
            m = m0_ref[b, dr][:, 0:1] if has_state else zero11
            for ci in range(nc):
                c = ci if dr == 0 else nc - 1 - ci
                cols = slice(b * L + c * T, b * L + (c + 1) * T)
                irow = g2t_s[2 * dr:2 * dr + 1, cols]
                brow = g2t_s[2 * dr + 1:2 * dr + 2, cols]
                bl = brow[:, T - 1:T] if dr == 0 else brow[:, 0:1]
                mw = jnp.max(bl - brow + irow, axis=-1, keepdims=True)
                mn = jnp.maximum(bl + m, mw)
                vals[c][3 * dr:3 * dr + 3] = [m, mn, bl]
                m = mn
            m_fin.append(m)
        for c in range(nc):
            rows8 = [jnp.broadcast_to(v, (1, LANES)) for v in vals[c]] + [jnp.zeros((2, LANES), F32)]
            sc_s[b * nc + c] = jnp.concatenate(rows8, axis=0)

    uq =_dot_nt(hb, wq_ref[...].astype(BF16))
    uk = _dot_nt(hb, wk_ref[...].astype(BF16))
    q_s[...] = _silu(_short_conv3(uq, cwq_ref, cbq_ref[...], L)).astype(BF16)
    uv = _dot_nt(hb, wv_ref[...].astype(BF16))
    k = _silu(_short_conv3(uk, cwk_ref, cbk_ref[...], L)) * (dh ** -0.5)
    kt_s[...] = k.T.astype(BF16)
    uo = _dot_nt(hb, wo_ref[...].astype(BF16))
    v_s[:, 0:dh] = uv.astype(BF16)
    v_s[:, dh:dh + LANES] = jnp.ones((R, LANES), BF16)
    uz = _dot_nt(hb, wz_ref[...].astype(BF16))
    og_s[...] = _sigmoid(uo) * _silu(uz)

    hsum_s[...] = jnp.zeros_like(hsum_s)
    for b in range(nb):
        for dr in range(2):
            if has_state:
                c_s[2 * b + dr, :, 0:dh] = c0_ref[b, dr]
                c_s[2 * b + dr, :, dh:dh + LANES] = jnp.broadcast_to(n0_ref[b, dr], (LANES, dh)).T
            else:
                c_s[2 * b + dr] = jnp.zeros((dh, dh + LANES), F32)

    def step(ci, carry):
        for b in range(nb):
            for dr in range(2):
                c = ci if dr == 0 else nc - 1 - ci
                r0 = pl.multiple_of(b * L + c * T, T)
                qc = q_s[pl.ds(r0, T), :]
                vc = v_s[pl.ds(r0, T), :]
                ktc = kt_s[:, pl.ds(r0, T)]
                b_rep = br_s[dr, pl.ds(r0, T), :]
                g2tc = g2t_s[:, pl.ds(r0, T)]
                irow = g2tc[2 * dr:2 * dr + 1, :]
                brow = g2tc[2 * dr + 1:2 * dr + 2, :]
                sc = sc_s[b * nc + c]
                m_prev = sc[3 * dr:3 * dr + 1, :]
                m_new = sc[3 * dr + 1:3 * dr + 2, :]
                bl = sc[3 * dr + 2:3 * dr + 3, :]
                c_old = c_s[2 * b + dr]

                dlog = jnp.where(causal if dr == 0 else anti, b_rep - brow + irow, NEG)
                inter = b_rep + m_prev
                mt = jnp.maximum(inter, jnp.max(dlog, axis=-1, keepdims=True))
                smat = _dot(qc, ktc) * jnp.exp(dlog - mt)
                iw = jnp.exp(inter - mt)
                ext = (_dot(smat.astype(BF16), vc)
                       + jnp.concatenate([iw, iw, iw], axis=-1) * _dot(qc, c_old.astype(BF16)))
                inv = 1.0 / jnp.maximum(jnp.abs(ext[:, dh:dh + LANES]), jnp.exp(-mt))
                hh = ext[:, 0:dh] * jnp.concatenate([inv, inv], axis=-1)
                hsum_s[pl.ds(r0, T), :] = hsum_s[pl.ds(r0, T), :] + hh

                ws = jnp.exp(bl - brow + irow - m_new)
                dec = jnp.exp(bl + m_prev - m_new)[:, 0:1]
                kw = (ktc.astype(F32) * ws).astype(BF16)
                c_s[2 * b + dr] = dec * c_old + _dot(kw, vc)
        return carry

    lax.fori_loop(0, nc, step, 0, unroll=max(1, min(nc, ML_STREAMS // (2 * nb))))
    if not has_state:
        for b in range(nb):
            for dr in range(2):
                cn_ref[b, dr] = c_s[2 * b + dr, :, 0:dh]
                nn_ref[b, dr] = c_s[2 * b + dr, :, dh:dh + LANES].T[0:1, :]
                mn_ref[b, dr] = jnp.broadcast_to(m_fin[2 * b + dr], (1, LANES))

    hs = hsum_s[...]
    hm = hs * lax.rsqrt(jnp.mean(hs * hs, axis=-1, keepdims=True) + EPS) * ng_ref[...]
    yb_ref[...] = (hm * og_s[...]).astype(yb_ref.dtype).reshape(yb_ref.shape)


def _mlstm(h, l, depth, w_in, wg_heads, gb_heads, conv_w, conv_b, norm_g, states, carried,
           seg_off):
    bsz, L, d = h.shape
    heads = wg_heads.shape[0]
    d_ml = norm_g.shape[-1]
    dh = d_ml // heads
    has_state = states is not None
    nb = max(1, min(bsz, ML_ROWS // L))
    nc = L // ML_CHUNK
    qo, ko, vo, oo, zo = (s // dh for s in seg_off)

    def w_spec(off):
        return pl.BlockSpec((None, dh, d), lambda b, hd, off=off: (l, off + hd, 0))

    def vec_spec(rows, off):
        return pl.BlockSpec((rows, dh), lambda b, hd, off=off: (0, off + hd))

    in_specs = [pl.BlockSpec((nb, L, d), lambda b, hd: (b, 0, 0)),
                w_spec(qo), w_spec(ko), w_spec(vo), w_spec(oo), w_spec(zo),
                pl.BlockSpec((None, LANES, d), lambda b, hd: (hd, 0, 0)),
                pl.BlockSpec((None, 1, LANES), lambda b, hd: (hd, 0, 0)),
                vec_spec(3, 0), vec_spec(3, heads), vec_spec(1, 0), vec_spec(1, heads),
                vec_spec(1, 0)]
    args = [h, w_in, w_in, w_in, w_in, w_in, wg_heads, gb_heads,
            conv_w, conv_w, conv_b.reshape(1, -1), conv_b.reshape(1, -1), norm_g.reshape(1, -1)]
    yb_spec = pl.BlockSpec((nb, L, dh), lambda b, hd: (b, 0, hd))
    yb_shape = jax.ShapeDtypeStruct((bsz, L, d_ml), BF16)
    state_idx = lambda b, hd: (b, l, 0, hd, 0, 0)
    c_spec = pl.BlockSpec((nb, None, 2, None, dh, dh), state_idx)
    n_spec = pl.BlockSpec((nb, None, 2, None, 1, dh), state_idx)
    m_spec = pl.BlockSpec((nb, None, 2, None, 1, LANES), state_idx)
    aliases = {}
    if has_state:
        in_specs += [c_spec, n_spec, m_spec]
        args += list(states)
        out_specs, out_shape = yb_spec, yb_shape
    else:
        if carried is not None:
            aliases = {len(args) + i: 1 + i for i in range(3)}
            in_specs += [pl.BlockSpec(memory_space=pl.ANY)] * 3
            args += list(carried)
        out_specs = (yb_spec, c_spec, n_spec, m_spec)
        out_shape = (yb_shape,
                     jax.ShapeDtypeStruct((bsz, depth, 2, heads, dh, dh), F32),
                     jax.ShapeDtypeStruct((bsz, depth, 2, heads, 1, dh), F32),
                     jax.ShapeDtypeStruct((bsz, depth, 2, heads, 1, LANES), F32))
    rows = nb * L
    scratch = [pltpu.VMEM((rows, dh), BF16),
               pltpu.VMEM((dh, rows), BF16),
               pltpu.VMEM((rows, dh + LANES), BF16),
               pltpu.VMEM((rows, LANES), F32),
               pltpu.VMEM((8, rows), F32),
               pltpu.VMEM((2, rows, LANES), F32),
               pltpu.VMEM((nb * nc, 8, LANES), F32),
               pltpu.VMEM((rows, dh), F32),
               pltpu.VMEM((rows, dh), F32),
               pltpu.VMEM((2 * nb, dh, dh + LANES), F32)]
    return pl.pallas_call(
        functools.partial(_mlstm_kernel, nb=nb, L=L, has_state=has_state, n_aliased=len(aliases)),
        grid=(bsz // nb, heads),
        in_specs=in_specs,
        out_specs=out_specs,
        out_shape=out_shape,
        input_output_aliases=aliases,
        scratch_shapes=scratch,
        compiler_params=_params("arbitrary", "arbitrary"),
        name="mlstm",
    )(*args)


def _out_kernel(x_ref, h_ref, ya_ref, yb_ref, mod_ref, wga_ref, wgb_ref, wpa_ref, wpb_ref,
                wout_ref, g_ref, *rest, last):
    w_s = rest[-1]
    rest = rest[:-1]

    @pl.when((pl.program_id(0) == 0) & (pl.program_id(1) == 0))
    def _():
        w_s[0] = wga_ref[0].astype(BF16)
        w_s[1] = wgb_ref[0].astype(BF16)
        w_s[2] = wpa_ref[...].astype(BF16)
        w_s[3] = wpb_ref[...].astype(BF16)
        w_s[4] = wout_ref[...].astype(BF16)

    hb = h_ref[...]
    ga = _sigmoid(_dot_nt(hb, w_s[0]))
    gb = _sigmoid(_dot_nt(hb, w_s[1]))
    merged = ga * _dot(ya_ref[...], w_s[2]) + gb * _dot(yb_ref[...], w_s[3])
    xn = x_ref[...] + mod_ref[2:3, :] * _dot(merged.astype(BF16), w_s[4])
    normed = xn * lax.rsqrt(jnp.mean(xn * xn, axis=-1, keepdims=True) + EPS) * g_ref[...]
    if last:
        (y_ref,) = rest
        y_ref[...] = normed
    else:
        modn_ref, xo_ref, ho_ref = rest
        xo_ref[...] = xn
        ho_ref[...] = (normed * (1.0 + modn_ref[1:2, :]) + modn_ref[0:1, :]).astype(ho_ref.dtype)


def _out(x, h, ya, yb, mod, l, w_t, off_ga, w_pa, w_pb, w_out, g_next, mod_next):
    bm, t, d = x.shape
    tm = min(t, OUT_TM)
    last = mod_next is None
    tok = lambda b, i: (b, i, 0)
    wl = lambda b, i: (l, 0, 0)
    tok_spec = pl.BlockSpec((None, tm, d), tok)
    mod_spec = pl.BlockSpec((None, 3, d), lambda b, i: (b, 0, 0))
    gate_blk = (pl.Element(1), pl.Element(d), pl.Element(d))
    once = pl.Buffered(1)
    in_specs = [tok_spec, tok_spec, tok_spec, tok_spec, mod_spec,
                pl.BlockSpec(gate_blk, lambda b, i: (l, off_ga, 0), pipeline_mode=once),
                pl.BlockSpec(gate_blk, lambda b, i: (l, off_ga + d, 0), pipeline_mode=once),
                pl.BlockSpec((None, d, d), wl, pipeline_mode=once),
                pl.BlockSpec((None, d, d), wl, pipeline_mode=once),
                pl.BlockSpec((None, d, d), wl, pipeline_mode=once),
                pl.BlockSpec((1, d), lambda b, i: (0, 0))]
    args = [x, h, ya, yb, mod, w_t, w_t, w_pa, w_pb, w_out, g_next.reshape(1, d)]
    if last:
        out_specs, out_shape = tok_spec, jax.ShapeDtypeStruct((bm, t, d), F32)
    else:
        in_specs.append(mod_spec)
        args.append(mod_next)
        out_specs = (tok_spec, tok_spec)
        out_shape = (jax.ShapeDtypeStruct((bm, t, d), F32), jax.ShapeDtypeStruct((bm, t, d), BF16))
    return pl.pallas_call(
        functools.partial(_out_kernel, last=last),
        grid=(bm, t // tm),
        in_specs=in_specs,
        out_specs=out_specs,
        out_shape=out_shape,
        scratch_shapes=[pltpu.VMEM((5, d, d), BF16)],
        compiler_params=_params("arbitrary", "arbitrary"),
        name="merge_out",
    )(*args)


def kernel(x_prompt, x_sample, state_C, state_n, state_m, c, c_ctx, norm_g, w_ada, b_ada, w_in, hy_conv_w, hy_conv_b, hy_w1, hy_b1, hy_w2, hy_b2, hy_w3, hy_b3, hy_freq, hy_decay, hy_bias, ml_conv_w, ml_conv_b, ml_if_b, ml_norm_g, w_pa, w_pb, w_out, final_g):
    depth, d, _ = w_in.shape
    d_hy = hy_bias.shape[-1]
    d_ml = ml_norm_g.shape[-1]
    heads = ml_if_b.shape[-1]
    bp, lp, _ = x_prompt.shape
    bs, ls, _ = x_sample.shape

    off_q = 4 * d_hy
    off_v = off_q + 2 * d_ml
    off_o = off_v + d_ml
    off_z = off_o + d_ml
    off_g = off_z + d_ml
    off_ga = off_g + 4 * heads
    seg_off = (off_q, off_q + d_ml, off_v, off_o, off_z)

    w_t = jnp.transpose(w_in, (0, 2, 1))
    wg = w_t[:, off_g:off_ga, :].reshape(depth, 4, heads, d).transpose(0, 2, 1, 3)
    wg_heads = jnp.zeros((depth, heads, LANES, d), BF16).at[:, :, :4, :].set(wg.astype(BF16))
    gb = ml_if_b.reshape(depth, 4, heads).transpose(0, 2, 1)
    gb_heads = jnp.zeros((depth, heads, 1, LANES), F32).at[:, :, 0, :4].set(gb)

    cvecs = jnp.zeros((8, d), F32).at[0].set(c_ctx).at[1:1 + bs].set(c)
    mods = _mods(cvecs, w_ada, b_ada).reshape(depth, 8, 3, d)

    xp = x_prompt.reshape(1, bp * lp, d)
    xs = x_sample
    cached = (state_C, state_n.reshape(bs, depth, 2, heads, 1, -1),
              jnp.broadcast_to(state_m[..., None, None], (bs, depth, 2, heads, 1, LANES)))
    fin = None
    hp = _norm_mod(xp, norm_g[0], mods[0, 0:1])
    hs = _norm_mod(xs, norm_g[0], mods[0, 1:1 + bs])
    dft = {L: _dft_operands(L) for L in sorted({lp, ls})}
    for l in range(depth):
        spectra = {}
        for L in sorted({lp, ls}):
            spectra[L] = _hyena_spectra(dft[L], l, hy_w1, hy_b1, hy_w2, hy_b2, hy_w3, hy_b3,
                                        hy_freq, hy_decay)

        last = l == depth - 1
        g_next = final_g if last else norm_g[l + 1]

        def layer(x, h, rows, bsz, L, states, carried):
            mod = mods[l, rows]
            hseq = h.reshape(bsz, L, d)
            ya = _hyena(hseq, l, w_t, hy_conv_w[l], hy_conv_b[l], spectra[L], hy_bias[l], dft[L])
            res = _mlstm(hseq, l, depth, w_t, wg_heads[l], gb_heads[l], ml_conv_w[l],
                         ml_conv_b[l], ml_norm_g[l], states, carried, seg_off)
            yb, fin = (res, None) if states is not None else (res[0], res[1:])
            res = _out(x, h, ya.reshape(x.shape), yb.reshape(x.shape), mod, l,
                       w_t, off_ga, w_pa, w_pb, w_out, g_next,
                       None if last else mods[l + 1, rows])
            return (res, None, fin) if last else (res[0], res[1], fin)

        xp, hp, fin = layer(xp, hp, slice(0, 1), bp, lp, None, fin)
        xs, hs, _ = layer(xs, hs, slice(1, 1 + bs), bs, ls, cached, None)

    return (xp.reshape(bp, lp, d), xs, fin[0], fin[1][:, :, :, :, 0, :], fin[2][:, :, :, :, 0, 0])
```

```python
import functools
import math

import numpy as np
import jax
import jax.numpy as jnp
from jax import lax
from jax.experimental import pallas as pl
from jax.experimental.pallas import tpu as pltpu

F32 = jnp.float32
BF16 = jnp.bfloat16
HIGHEST = lax.Precision.HIGHEST

HY_ORDER = 2
HY_BANDS = 16
HY_SHIFT = 0.05
ML_CHUNK = 128
EPS = 1e-6
NEG = -1e30

LANES = 128
SUBLANES = 8
VMEM_LIMIT = 56 * 1024 * 1024
HY_CB = 256
ML_ROWS = 2048
HY_ROWS = 2048
ML_STREAMS = 8
OUT_TM = 512


def _dot(a, b):
    return jnp.dot(a, b, preferred_element_type=F32)


def _dot_nt(a, bt):
    return lax.dot_general(a, bt, (((1,), (1,)), ((), ())), preferred_element_type=F32)


def _dot_hi(a, b):
    return jnp.dot(a, b, preferred_element_type=F32, precision=HIGHEST)


def _sigmoid(x):
    return 1.0 / (1.0 + jnp.exp(-x))


def _silu(x):
    return x * _sigmoid(x)


def _log_sigmoid(x):
    return jnp.minimum(x, 0.0) - jnp.log(1.0 + jnp.exp(-jnp.abs(x)))


def _params(*sem):
    return pltpu.CompilerParams(dimension_semantics=sem, vmem_limit_bytes=VMEM_LIMIT)


@functools.lru_cache(maxsize=None)
def _dft_table(L):
    N = 2 * L
    k = np.arange(L, dtype=np.int64)[:, None]
    n = np.arange(L, dtype=np.int64)[None, :]
    ang = 2.0 * np.pi * ((k * n) % N).astype(np.float64) / N
    c = np.cos(ang)
    s = -np.sin(ang)
    s[0, :] = 1.0 - 2.0 * (np.arange(L) % 2)
    return np.concatenate([c, s], axis=0).astype(np.float32)


def _dft_operands(L):
    f = jnp.asarray(_dft_table(L)).astype(BF16)
    return f, f.T


def _split_bf16(x):
    hi = x.astype(BF16)
    return hi, (x - hi.astype(F32)).astype(BF16)


def _mod_kernel(c_ref, w_ref, b_ref, o_ref):
    a_hi, a_lo = _split_bf16(_silu(c_ref[...]))
    w_hi, w_lo = _split_bf16(w_ref[...])
    o_ref[...] = (_dot(a_hi, w_hi) + (_dot(a_lo, w_hi) + _dot(a_hi, w_lo))) + b_ref[...]


def _mods(cvecs, w_ada, b_ada):
    depth, d, d3 = w_ada.shape
    r = cvecs.shape[0]
    tn = 1024
    return pl.pallas_call(
        _mod_kernel,
        grid=(depth, d3 // tn),
        in_specs=[pl.BlockSpec((r, d), lambda l, j: (0, 0)),
                  pl.BlockSpec((None, d, tn), lambda l, j: (l, 0, j)),
                  pl.BlockSpec((None, 1, tn), lambda l, j: (l, 0, j))],
        out_specs=pl.BlockSpec((None, r, tn), lambda l, j: (l, 0, j)),
        out_shape=jax.ShapeDtypeStruct((depth, r, d3), F32),
        compiler_params=_params("arbitrary", "arbitrary"),
        name="adaln_mod",
    )(cvecs, w_ada, b_ada.reshape(depth, 1, d3))


def _norm_mod_kernel(x_ref, g_ref, mod_ref, o_ref):
    x = x_ref[...]
    y = x * lax.rsqrt(jnp.mean(x * x, axis=-1, keepdims=True) + EPS) * g_ref[...]
    o_ref[...] = (y * (1.0 + mod_ref[1:2, :]) + mod_ref[0:1, :]).astype(o_ref.dtype)


def _norm_mod(x, g, mod):
    bm, t, d = x.shape
    tm = min(t, 2048)
    return pl.pallas_call(
        _norm_mod_kernel,
        grid=(bm, t // tm),
        in_specs=[pl.BlockSpec((None, tm, d), lambda b, i: (b, i, 0)),
                  pl.BlockSpec((1, d), lambda b, i: (0, 0)),
                  pl.BlockSpec((None, 3, d), lambda b, i: (b, 0, 0))],
        out_specs=pl.BlockSpec((None, tm, d), lambda b, i: (b, i, 0)),
        out_shape=jax.ShapeDtypeStruct((bm, t, d), BF16),
        compiler_params=_params("arbitrary", "arbitrary"),
        name="norm_mod",
    )(x, g.reshape(1, d), mod)


def _filter_kernel(w1_ref, b1_ref, w2_ref, b2_ref, freq_ref, w3_00, w3_01, w3_10, w3_11,
                   b3_ref, decay_ref, f_ref, spec_ref, hdn_ref, *, L):
    cb = spec_ref.shape[-1]

    @pl.when(pl.program_id(0) == 0)
    def _():
        row = lax.broadcasted_iota(jnp.int32, (L, LANES), 0).astype(F32)
        lane = lax.broadcasted_iota(jnp.int32, (L, LANES), 1)
        t = row / L
        band = jnp.where(lane <= HY_BANDS, lane, lane - HY_BANDS).astype(F32)
        ang = 2.0 * math.pi * t * band
        feats = jnp.where(lane == 0, t,
                          jnp.where(lane <= HY_BANDS, jnp.cos(ang),
                                    jnp.where(lane <= 2 * HY_BANDS, jnp.sin(ang), 0.0)))
        hdn = jnp.sin(freq_ref[0:1, :] * (_dot_hi(feats, w1_ref[...]) + b1_ref[...]))
        hdn_ref[...] = jnp.sin(freq_ref[1:2, :] * (_dot_hi(hdn, w2_ref[...]) + b2_ref[...]))

    hdn = hdn_ref[...].astype(BF16)
    row_i = lax.broadcasted_iota(jnp.int32, (L, cb), 0)
    t = row_i.astype(F32) / L
    first = row_i == 0
    sign = jnp.where((row_i & 1) == 0, 1.0, -1.0)
    w3 = ((w3_00, w3_01), (w3_10, w3_11))
    for o in range(HY_ORDER):
        hs = []
        for dr in range(2):
            j = 2 * o + dr
            hv = _dot(hdn, w3[o][dr][...].astype(BF16)) + b3_ref[j:j + 1, :]
            win = jnp.exp(-t * jnp.abs(decay_ref[j:j + 1, :])) + HY_SHIFT
            hs.append(hv * win)
        hf = hs[0]
        hb = jnp.where(first, 0.0, hs[1])
        l1 = jnp.sum(jnp.abs(hf), axis=0, keepdims=True) + jnp.sum(jnp.abs(hb), axis=0, keepdims=True)
        inv_l1 = 1.0 / l1
        even = (hf + hb) * inv_l1
        odd = (hf - hb) * inv_l1
        re = _dot(f_ref[0:L, :], even.astype(BF16))
        im = _dot(f_ref[L:2 * L, :], odd.astype(BF16))
        nyq = jnp.sum(even * sign, axis=0, keepdims=True)
        a = re * jnp.where(first, 0.5 / L, 1.0 / L)
        spec_ref[o, 0] = a
        spec_ref[o, 1] = jnp.where(first, 0.0, im * (1.0 / L))
        spec_ref[o, 2] = jnp.where(first, nyq * (0.5 / L), a)


def _hyena_spectra(dft, l, hy_w1, hy_b1, hy_w2, hy_b2, hy_w3, hy_b3, hy_freq, hy_decay):
    emb, ff = hy_w1.shape[1:]
    d_hy = hy_decay.shape[-1]
    cb = HY_CB
    ncb = d_hy // cb
    w1p = jnp.zeros((LANES, ff), F32).at[:emb].set(hy_w1[l])
    f, _ = dft
    L = f.shape[1]
    const = lambda j: (0, 0)

    def w3_spec(o, dr):
        return pl.BlockSpec((None, ff, cb), lambda j, o=o, dr=dr: (l, 0, (2 * o + dr) * ncb + j))

    return pl.pallas_call(
        functools.partial(_filter_kernel, L=L),
        grid=(ncb,),
        in_specs=[pl.BlockSpec((LANES, ff), const),
                  pl.BlockSpec((1, ff), const),
                  pl.BlockSpec((ff, ff), const),
                  pl.BlockSpec((1, ff), const),
                  pl.BlockSpec((2, ff), const),
                  w3_spec(0, 0), w3_spec(0, 1), w3_spec(1, 0), w3_spec(1, 1),
                  pl.BlockSpec((2 * HY_ORDER, cb), lambda j: (0, j)),
                  pl.BlockSpec((2 * HY_ORDER, cb), lambda j: (0, j)),
                  pl.BlockSpec((2 * L, L), const)],
        out_specs=pl.BlockSpec((HY_ORDER, 3, L, cb), lambda j: (0, 0, 0, j)),
        out_shape=jax.ShapeDtypeStruct((HY_ORDER, 3, L, d_hy), F32),
        scratch_shapes=[pltpu.VMEM((L, ff), F32)],
        compiler_params=_params("arbitrary"),
        name="hyena_filter",
    )(w1p, hy_b1[l].reshape(1, ff), hy_w2[l], hy_b2[l].reshape(1, ff), hy_freq[l],
      hy_w3, hy_w3, hy_w3, hy_w3,
      hy_b3[l].reshape(2 * HY_ORDER, d_hy), hy_decay[l].reshape(2 * HY_ORDER, d_hy),
      f)


def _short_conv3(x, w_ref, b, seq=None):
    n, c = x.shape
    seq = n if seq is None else seq
    prev = pltpu.roll(x, 1, axis=0)
    nxt = pltpu.roll(x, n - 1, axis=0)
    row = lax.broadcasted_iota(jnp.int32, (SUBLANES, c), 0)
    prev_parts, nxt_parts = [], []
    for s in range(0, n, seq):
        e = s + seq
        prev_parts += [jnp.where(row == 0, 0.0, prev[s:s + SUBLANES]), prev[s + SUBLANES:e]]
        nxt_parts += [nxt[s:e - SUBLANES], jnp.where(row == SUBLANES - 1, 0.0, nxt[e - SUBLANES:e])]
    prev = jnp.concatenate(prev_parts, axis=0)
    nxt = jnp.concatenate(nxt_parts, axis=0)
    return ((b + prev * w_ref[0:1, :]) + x * w_ref[1:2, :]) + nxt * w_ref[2:3, :]


def _hyena_kernel(h_ref, wx1_ref, wx2_ref, wv_ref, wz_ref, cw_ref, cb_ref, spec_ref, hbias_ref,
                  f_ref, g_ref, o_ref, *, nb, L):
    seqs = range(nb)
    wv = wv_ref[...].astype(BF16)
    z = [_short_conv3(_dot_nt(h_ref[b], wv), cw_ref.at[2], cb_ref[2:3, :]) for b in seqs]
    gate_w = (wx1_ref, wx2_ref)
    for o in range(HY_ORDER):
        zf = [_dot(f_ref[...], z[b].astype(BF16)) for b in seqs]
        wg = gate_w[o][...].astype(BF16)
        gate = [_short_conv3(_dot_nt(h_ref[b], wg), cw_ref.at[o], cb_ref[o:o + 1, :]) for b in seqs]
        ycat = []
        for b in seqs:
            zc, zs = zf[b][0:L], zf[b][L:2 * L]
            a, bm, dm = spec_ref[o, 0], spec_ref[o, 1], spec_ref[o, 2]
            ycat.append(jnp.concatenate([zc * a - zs * bm, zc * bm + zs * dm], axis=0).astype(BF16))
        y = [_dot(g_ref[...], ycat[b]) for b in seqs]
        z = [gate[b] * (y[b] + hbias_ref[o:o + 1, :] * z[b]) for b in seqs]
    wz = wz_ref[...].astype(BF16)
    for b in seqs:
        o_ref[b] = (z[b] * _silu(_dot_nt(h_ref[b], wz))).astype(o_ref.dtype)


def _hyena(h, l, w_in, conv_w, conv_b, spec, hy_bias, dft):
    bsz, L, d = h.shape
    d_hy = hy_bias.shape[-1]
    cb = HY_CB
    ncb = d_hy // cb
    nb = max(1, min(bsz, HY_ROWS // L))
    f, g = dft
    once = pl.Buffered(1)

    def w_spec(seg):
        return pl.BlockSpec((None, cb, d), lambda j, i, seg=seg: (l, seg * ncb + j, 0))

    return pl.pallas_call(
        functools.partial(_hyena_kernel, nb=nb, L=L),
        grid=(ncb, bsz // nb),
        in_specs=[pl.BlockSpec((nb, L, d), lambda j, i: (i, 0, 0)),
                  w_spec(0), w_spec(1), w_spec(2), w_spec(3),
                  pl.BlockSpec((3, 3, cb), lambda j, i: (0, 0, j)),
                  pl.BlockSpec((3, cb), lambda j, i: (0, j)),
                  pl.BlockSpec((HY_ORDER, 3, L, cb), lambda j, i: (0, 0, 0, j), pipeline_mode=once),
                  pl.BlockSpec((HY_ORDER, cb), lambda j, i: (0, j)),
                  pl.BlockSpec((2 * L, L), lambda j, i: (0, 0), pipeline_mode=once),
                  pl.BlockSpec((L, 2 * L), lambda j, i: (0, 0), pipeline_mode=once)],
        out_specs=pl.BlockSpec((nb, L, cb), lambda j, i: (i, 0, j)),
        out_shape=jax.ShapeDtypeStruct((bsz, L, d_hy), BF16),
        compiler_params=_params("arbitrary", "arbitrary"),
        name="hyena",
    )(h, w_in, w_in, w_in, w_in,
      conv_w.reshape(3, 3, d_hy).transpose(1, 0, 2), conv_b.reshape(3, d_hy), spec, hy_bias, f, g)


def _mlstm_kernel(*refs, nb, L, has_state, n_aliased):
    (h_ref, wq_ref, wk_ref, wv_ref, wo_ref, wz_ref, wg_ref, gb_ref, cwq_ref, cwk_ref,
     cbq_ref, cbk_ref, ng_ref) = refs[:13]
    refs = refs[13:]
    if has_state:
        c0_ref, n0_ref, m0_ref, yb_ref = refs[:4]
        refs = refs[4:]
    else:
        yb_ref, cn_ref, nn_ref, mn_ref = refs[n_aliased:n_aliased + 4]
        refs = refs[n_aliased + 4:]
    q_s, kt_s, v_s, g2_s, g2t_s, br_s, sc_s, hsum_s, og_s, c_s = refs

    T = ML_CHUNK
    nc = L // T
    R = nb * L
    d = h_ref.shape[-1]
    dh = q_s.shape[-1]
    hb = h_ref[...].reshape(R, d)

    g = _dot_nt(hb, wg_ref[...]) + gb_ref[...]
    lf = _log_sigmoid(g)
    ti = lax.broadcasted_iota(jnp.int32, (T, T), 0)
    si = lax.broadcasted_iota(jnp.int32, (T, T), 1)
    causal = si <= ti
    anti = si >= ti
    lower = causal.astype(F32)
    upper = anti.astype(F32)
    lane = lax.broadcasted_iota(jnp.int32, (T, LANES), 1)
    for j in range(nb * nc):
        rows = slice(j * T, (j + 1) * T)
        pre = _dot_hi(lower, lf[rows])
        suf = _dot_hi(upper, lf[rows])
        g2_s[rows, :] = jnp.where(lane == 1, pre, jnp.where(lane == 3, suf, g[rows]))
        br_s[0, rows, :] = jnp.broadcast_to(pre[:, 1:2], (T, LANES))
        br_s[1, rows, :] = jnp.broadcast_to(suf[:, 3:4], (T, LANES))
    g2t_s[...] = g2_s[...].T[0:8, :]

    zero11 = jnp.zeros((1, 1), F32)
    m_fin = []
    for b in range(nb):
        vals = [[None] * 6 for _ in range(nc)]
        for dr in range(2):
            m = m0_ref[b, dr][:, 0:1] if has_state else zero11
            for ci in range(nc):
                c = ci if dr == 0 else nc - 1 - ci
                cols = slice(b * L + c * T, b * L + (c + 1) * T)
                irow = g2t_s[2 * dr:2 * dr + 1, cols]
                brow = g2t_s[2 * dr + 1:2 * dr + 2, cols]
                bl = brow[:, T - 1:T] if dr == 0 else brow[:, 0:1]
                mw = jnp.max(bl - brow + irow, axis=-1, keepdims=True)
                mn = jnp.maximum(bl + m, mw)
                vals[c][3 * dr:3 * dr + 3] = [m, mn, bl]
                m = mn
            m_fin.append(m)
        for c in range(nc):
            rows8 = [jnp.broadcast_to(v, (1, LANES)) for v in vals[c]] + [jnp.zeros((2, LANES), F32)]
            sc_s[b * nc + c] = jnp.concatenate(rows8, axis=0)

    uq =_dot_nt(hb, wq_ref[...].astype(BF16))
    uk = _dot_nt(hb, wk_ref[...].astype(BF16))
    q_s[...] = _silu(_short_conv3(uq, cwq_ref, cbq_ref[...], L)).astype(BF16)
    uv = _dot_nt(hb, wv_ref[...].astype(BF16))
    k = _silu(_short_conv3(uk, cwk_ref, cbk_ref[...], L)) * (dh ** -0.5)
    kt_s[...] = k.T.astype(BF16)
    uo = _dot_nt(hb, wo_ref[...].astype(BF16))
    v_s[:, 0:dh] = uv.astype(BF16)
    v_s[:, dh:dh + LANES] = jnp.ones((R, LANES), BF16)
    uz = _dot_nt(hb, wz_ref[...].astype(BF16))
    og_s[...] = _sigmoid(uo) * _silu(uz)

    hsum_s[...] = jnp.zeros_like(hsum_s)
    for b in range(nb):
        for dr in range(2):
            if has_state:
                c_s[2 * b + dr, :, 0:dh] = c0_ref[b, dr]
                c_s[2 * b + dr, :, dh:dh + LANES] = jnp.broadcast_to(n0_ref[b, dr], (LANES, dh)).T
            else:
                c_s[2 * b + dr] = jnp.zeros((dh, dh + LANES), F32)

    def step(ci, carry):
        for b in range(nb):
            for dr in range(2):
                c = ci if dr == 0 else nc - 1 - ci
                r0 = pl.multiple_of(b * L + c * T, T)
                qc = q_s[pl.ds(r0, T), :]
                vc = v_s[pl.ds(r0, T), :]
                ktc = kt_s[:, pl.ds(r0, T)]
                b_rep = br_s[dr, pl.ds(r0, T), :]
                g2tc = g2t_s[:, pl.ds(r0, T)]
                irow = g2tc[2 * dr:2 * dr + 1, :]
                brow = g2tc[2 * dr + 1:2 * dr + 2, :]
                sc = sc_s[b * nc + c]
                m_prev = sc[3 * dr:3 * dr + 1, :]
                m_new = sc[3 * dr + 1:3 * dr + 2, :]
                bl = sc[3 * dr + 2:3 * dr + 3, :]
                c_old = c_s[2 * b + dr]

                dlog = jnp.where(causal if dr == 0 else anti, b_rep - brow + irow, NEG)
                inter = b_rep + m_prev
                mt = jnp.maximum(inter, jnp.max(dlog, axis=-1, keepdims=True))
                smat = _dot(qc, ktc) * jnp.exp(dlog - mt)
                iw = jnp.exp(inter - mt)
                ext = (_dot(smat.astype(BF16), vc)
                       + jnp.concatenate([iw, iw, iw], axis=-1) * _dot(qc, c_old.astype(BF16)))
                inv = 1.0 / jnp.maximum(jnp.abs(ext[:, dh:dh + LANES]), jnp.exp(-mt))
                hh = ext[:, 0:dh] * jnp.concatenate([inv, inv], axis=-1)
                hsum_s[pl.ds(r0, T), :] = hsum_s[pl.ds(r0, T), :] + hh

                ws = jnp.exp(bl - brow + irow - m_new)
                dec = jnp.exp(bl + m_prev - m_new)[:, 0:1]
                kw = (ktc.astype(F32) * ws).astype(BF16)
                c_s[2 * b + dr] = dec * c_old + _dot(kw, vc)
        return carry

    lax.fori_loop(0, nc, step, 0, unroll=max(1, min(nc, ML_STREAMS // (2 * nb))))
    if not has_state:
        for b in range(nb):
            for dr in range(2):
                cn_ref[b, dr] = c_s[2 * b + dr, :, 0:dh]
                nn_ref[b, dr] = c_s[2 * b + dr, :, dh:dh + LANES].T[0:1, :]
                mn_ref[b, dr] = jnp.broadcast_to(m_fin[2 * b + dr], (1, LANES))

    hs = hsum_s[...]
    hm = hs * lax.rsqrt(jnp.mean(hs * hs, axis=-1, keepdims=True) + EPS) * ng_ref[...]
    yb_ref[...] = (hm * og_s[...]).astype(yb_ref.dtype).reshape(yb_ref.shape)


def _mlstm(h, l, depth, w_in, wg_heads, gb_heads, conv_w, conv_b, norm_g, states, carried,
           seg_off):
    bsz, L, d = h.shape
    heads = wg_heads.shape[0]
    d_ml = norm_g.shape[-1]
    dh = d_ml // heads
    has_state = states is not None
    nb = max(1, min(bsz, ML_ROWS // L))
    nc = L // ML_CHUNK
    qo, ko, vo, oo, zo = (s // dh for s in seg_off)

    def w_spec(off):
        return pl.BlockSpec((None, dh, d), lambda b, hd, off=off: (l, off + hd, 0))

    def vec_spec(rows, off):
        return pl.BlockSpec((rows, dh), lambda b, hd, off=off: (0, off + hd))

    in_specs = [pl.BlockSpec((nb, L, d), lambda b, hd: (b, 0, 0)),
                w_spec(qo), w_spec(ko), w_spec(vo), w_spec(oo), w_spec(zo),
                pl.BlockSpec((None, LANES, d), lambda b, hd: (hd, 0, 0)),
                pl.BlockSpec((None, 1, LANES), lambda b, hd: (hd, 0, 0)),
                vec_spec(3, 0), vec_spec(3, heads), vec_spec(1, 0), vec_spec(1, heads),
                vec_spec(1, 0)]
    args = [h, w_in, w_in, w_in, w_in, w_in, wg_heads, gb_heads,
            conv_w, conv_w, conv_b.reshape(1, -1), conv_b.reshape(1, -1), norm_g.reshape(1, -1)]
    yb_spec = pl.BlockSpec((nb, L, dh), lambda b, hd: (b, 0, hd))
    yb_shape = jax.ShapeDtypeStruct((bsz, L, d_ml), BF16)
    state_idx = lambda b, hd: (b, l, 0, hd, 0, 0)
    c_spec = pl.BlockSpec((nb, None, 2, None, dh, dh), state_idx)
    n_spec = pl.BlockSpec((nb, None, 2, None, 1, dh), state_idx)
    m_spec = pl.BlockSpec((nb, None, 2, None, 1, LANES), state_idx)
    aliases = {}
    if has_state:
        in_specs += [c_spec, n_spec, m_spec]
        args += list(states)
        out_specs, out_shape = yb_spec, yb_shape
    else:
        if carried is not None:
            aliases = {len(args) + i: 1 + i for i in range(3)}
            in_specs += [pl.BlockSpec(memory_space=pl.ANY)] * 3
            args += list(carried)
        out_specs = (yb_spec, c_spec, n_spec, m_spec)
        out_shape = (yb_shape,
                     jax.ShapeDtypeStruct((bsz, depth, 2, heads, dh, dh), F32),
                     jax.ShapeDtypeStruct((bsz, depth, 2, heads, 1, dh), F32),
                     jax.ShapeDtypeStruct((bsz, depth, 2, heads, 1, LANES), F32))
    rows = nb * L
    scratch = [pltpu.VMEM((rows, dh), BF16),
               pltpu.VMEM((dh, rows), BF16),
               pltpu.VMEM((rows, dh + LANES), BF16),
               pltpu.VMEM((rows, LANES), F32),
               pltpu.VMEM((8, rows), F32),
               pltpu.VMEM((2, rows, LANES), F32),
               pltpu.VMEM((nb * nc, 8, LANES), F32),
               pltpu.VMEM((rows, dh), F32),
               pltpu.VMEM((rows, dh), F32),
               pltpu.VMEM((2 * nb, dh, dh + LANES), F32)]
    return pl.pallas_call(
        functools.partial(_mlstm_kernel, nb=nb, L=L, has_state=has_state, n_aliased=len(aliases)),
        grid=(bsz // nb, heads),
        in_specs=in_specs,
        out_specs=out_specs,
        out_shape=out_shape,
        input_output_aliases=aliases,
        scratch_shapes=scratch,
        compiler_params=_params("arbitrary", "arbitrary"),
        name="mlstm",
    )(*args)


OUT_WEIGHTS = 5
OUT_STAGES = 2


def _out_kernel(x_ref, h_ref, ya_ref, yb_ref, mod_ref, wt_hbm, wpa_hbm, wpb_hbm, wout_hbm,
                g_ref, *rest, last, l, off_ga):
    w_s, stage, sem = rest[-3:]
    rest = rest[:-3]
    d = w_s.shape[-1]
    srcs = (wt_hbm.at[l, pl.ds(off_ga, d), :], wt_hbm.at[l, pl.ds(off_ga + d, d), :],
            wpa_hbm.at[l], wpb_hbm.at[l], wout_hbm.at[l])

    def copy(i):
        slot = i % OUT_STAGES
        return pltpu.make_async_copy(srcs[i], stage.at[slot], sem.at[slot])

    def fetched(i):
        copy(i).wait()
        w_s[i] = stage[i % OUT_STAGES].astype(BF16)
        if i + OUT_STAGES < OUT_WEIGHTS:
            copy(i + OUT_STAGES).start()
        return w_s[i]

    def body(weight):
        hb = h_ref[...]
        ga = _sigmoid(_dot_nt(hb, weight(0)))
        gb = _sigmoid(_dot_nt(hb, weight(1)))
        merged = ga * _dot(ya_ref[...], weight(2)) + gb * _dot(yb_ref[...], weight(3))
        xn = x_ref[...] + mod_ref[2:3, :] * _dot(merged.astype(BF16), weight(4))
        normed = xn * lax.rsqrt(jnp.mean(xn * xn, axis=-1, keepdims=True) + EPS) * g_ref[...]
        if last:
            (y_ref,) = rest
            y_ref[...] = normed
        else:
            modn_ref, xo_ref, ho_ref = rest
            xo_ref[...] = xn
            ho_ref[...] = (normed * (1.0 + modn_ref[1:2, :]) + modn_ref[0:1, :]).astype(ho_ref.dtype)

    first = (pl.program_id(0) == 0) & (pl.program_id(1) == 0)

    @pl.when(first)
    def _():
        for i in range(OUT_STAGES):
            copy(i).start()
        body(fetched)

    @pl.when(jnp.logical_not(first))
    def _():
        body(lambda i: w_s[i])


def _out(x, h, ya, yb, mod, l, w_t, off_ga, w_pa, w_pb, w_out, g_next, mod_next):
    bm, t, d = x.shape
    tm = min(t, OUT_TM)
    last = mod_next is None
    tok = lambda b, i: (b, i, 0)
    tok_spec = pl.BlockSpec((None, tm, d), tok)
    mod_spec = pl.BlockSpec((None, 3, d), lambda b, i: (b, 0, 0))
    hbm = pl.BlockSpec(memory_space=pl.ANY)
    in_specs = [tok_spec, tok_spec, tok_spec, tok_spec, mod_spec, hbm, hbm, hbm, hbm,
                pl.BlockSpec((1, d), lambda b, i: (0, 0))]
    args = [x, h, ya, yb, mod, w_t, w_pa, w_pb, w_out, g_next.reshape(1, d)]
    if last:
        out_specs, out_shape = tok_spec, jax.ShapeDtypeStruct((bm, t, d), F32)
    else:
        in_specs.append(mod_spec)
        args.append(mod_next)
        out_specs = (tok_spec, tok_spec)
        out_shape = (jax.ShapeDtypeStruct((bm, t, d), F32), jax.ShapeDtypeStruct((bm, t, d), BF16))
    return pl.pallas_call(
        functools.partial(_out_kernel, last=last, l=l, off_ga=off_ga),
        grid=(bm, t // tm),
        in_specs=in_specs,
        out_specs=out_specs,
        out_shape=out_shape,
        scratch_shapes=[pltpu.VMEM((OUT_WEIGHTS, d, d), BF16),
                        pltpu.VMEM((OUT_STAGES, d, d), F32),
                        pltpu.SemaphoreType.DMA((OUT_STAGES,))],
        compiler_params=_params("arbitrary", "arbitrary"),
        name="merge_out",
    )(*args)


def kernel(x_prompt, x_sample, state_C, state_n, state_m, c, c_ctx, norm_g, w_ada, b_ada, w_in, hy_conv_w, hy_conv_b, hy_w1, hy_b1, hy_w2, hy_b2, hy_w3, hy_b3, hy_freq, hy_decay, hy_bias, ml_conv_w, ml_conv_b, ml_if_b, ml_norm_g, w_pa, w_pb, w_out, final_g):
    depth, d, _ = w_in.shape
    d_hy = hy_bias.shape[-1]
    d_ml = ml_norm_g.shape[-1]
    heads = ml_if_b.shape[-1]
    bp, lp, _ = x_prompt.shape
    bs, ls, _ = x_sample.shape

    off_q = 4 * d_hy
    off_v = off_q + 2 * d_ml
    off_o = off_v + d_ml
    off_z = off_o + d_ml
    off_g = off_z + d_ml
    off_ga = off_g + 4 * heads
    seg_off = (off_q, off_q + d_ml, off_v, off_o, off_z)

    w_t = jnp.transpose(w_in, (0, 2, 1))
    wg = w_t[:, off_g:off_ga, :].reshape(depth, 4, heads, d).transpose(0, 2, 1, 3)
    wg_heads = jnp.zeros((depth, heads, LANES, d), BF16).at[:, :, :4, :].set(wg.astype(BF16))
    gb = ml_if_b.reshape(depth, 4, heads).transpose(0, 2, 1)
    gb_heads = jnp.zeros((depth, heads, 1, LANES), F32).at[:, :, 0, :4].set(gb)

    cvecs = jnp.zeros((8, d), F32).at[0].set(c_ctx).at[1:1 + bs].set(c)
    mods = _mods(cvecs, w_ada, b_ada).reshape(depth, 8, 3, d)

    xp = x_prompt.reshape(1, bp * lp, d)
    xs = x_sample
    cached = (state_C, state_n.reshape(bs, depth, 2, heads, 1, -1),
              jnp.broadcast_to(state_m[..., None, None], (bs, depth, 2, heads, 1, LANES)))
    fin = None
    hp = _norm_mod(xp, norm_g[0], mods[0, 0:1])
    hs = _norm_mod(xs, norm_g[0], mods[0, 1:1 + bs])
    dft = {L: _dft_operands(L) for L in sorted({lp, ls})}
    for l in range(depth):
        spectra = {}
        for L in sorted({lp, ls}):
            spectra[L] = _hyena_spectra(dft[L], l, hy_w1, hy_b1, hy_w2, hy_b2, hy_w3, hy_b3,
                                        hy_freq, hy_decay)

        last = l == depth - 1
        g_next = final_g if last else norm_g[l + 1]

        def layer(x, h, rows, bsz, L, states, carried):
            mod = mods[l, rows]
            hseq = h.reshape(bsz, L, d)
            ya = _hyena(hseq, l, w_t, hy_conv_w[l], hy_conv_b[l], spectra[L], hy_bias[l], dft[L])
            res = _mlstm(hseq, l, depth, w_t, wg_heads[l], gb_heads[l], ml_conv_w[l],
                         ml_conv_b[l], ml_norm_g[l], states, carried, seg_off)
            yb, fin = (res, None) if states is not None else (res[0], res[1:])
            res = _out(x, h, ya.reshape(x.shape), yb.reshape(x.shape), mod, l,
                       w_t, off_ga, w_pa, w_pb, w_out, g_next,
                       None if last else mods[l + 1, rows])
            return (res, None, fin) if last else (res[0], res[1], fin)

        xp, hp, fin = layer(xp, hp, slice(0, 1), bp, lp, None, fin)
        xs, hs, _ = layer(xs, hs, slice(1, 1 + bs), bs, ls, cached, None)

    return (xp.reshape(bp, lp, d), xs, fin[0], fin[1][:, :, :, :, 0, :], fin[2][:, :, :, :, 0, 0])
```

```python
import functools
import math

import numpy as np
import jax
import jax.numpy as jnp
from jax import lax
from jax.experimental import pallas as pl
from jax.experimental.pallas import tpu as pltpu

F32 = jnp.float32
BF16 = jnp.bfloat16
HIGHEST = lax.Precision.HIGHEST

HY_ORDER = 2
HY_BANDS = 16
HY_SHIFT = 0.05
ML_CHUNK = 128
EPS = 1e-6
NEG = -1e30

LANES = 128
SUBLANES = 8
VMEM_LIMIT = 56 * 1024 * 1024
HY_CB = 256
ML_ROWS = 2048
HY_ROWS = 2048
ML_STREAMS = 8
OUT_TM = 512


def _dot(a, b):
    return jnp.dot(a, b, preferred_element_type=F32)


def _dot_nt(a, bt):
    return lax.dot_general(a, bt, (((1,), (1,)), ((), ())), preferred_element_type=F32)


def _dot_hi(a, b):
    return jnp.dot(a, b, preferred_element_type=F32, precision=HIGHEST)


def _sigmoid(x):
    return 1.0 / (1.0 + jnp.exp(-x))


def _silu(x):
    return x * _sigmoid(x)


def _log_sigmoid(x):
    return jnp.minimum(x, 0.0) - jnp.log(1.0 + jnp.exp(-jnp.abs(x)))


def _params(*sem):
    return pltpu.CompilerParams(dimension_semantics=sem, vmem_limit_bytes=VMEM_LIMIT)


@functools.lru_cache(maxsize=None)
def _dft_table(L):
    N = 2 * L
    k = np.arange(L, dtype=np.int64)[:, None]
    n = np.arange(L, dtype=np.int64)[None, :]
    ang = 2.0 * np.pi * ((k * n) % N).astype(np.float64) / N
    c = np.cos(ang)
    s = -np.sin(ang)
    s[0, :] = 1.0 - 2.0 * (np.arange(L) % 2)
    return np.concatenate([c, s], axis=0).astype(np.float32)


def _dft_operands(L):
    f = jnp.asarray(_dft_table(L)).astype(BF16)
    return f, f.T


def _split_bf16(x):
    hi = x.astype(BF16)
    return hi, (x - hi.astype(F32)).astype(BF16)


def _mod_kernel(c_ref, w_ref, b_ref, o_ref):
    a_hi, a_lo = _split_bf16(_silu(c_ref[...]))
    w_hi, w_lo = _split_bf16(w_ref[...])
    o_ref[...] = (_dot(a_hi, w_hi) + (_dot(a_lo, w_hi) + _dot(a_hi, w_lo))) + b_ref[...]


def _mods(cvecs, w_ada, b_ada):
    depth, d, d3 = w_ada.shape
    r = cvecs.shape[0]
    tn = 1024
    return pl.pallas_call(
        _mod_kernel,
        grid=(depth, d3 // tn),
        in_specs=[pl.BlockSpec((r, d), lambda l, j: (0, 0)),
                  pl.BlockSpec((None, d, tn), lambda l, j: (l, 0, j)),
                  pl.BlockSpec((None, 1, tn), lambda l, j: (l, 0, j))],
        out_specs=pl.BlockSpec((None, r, tn), lambda l, j: (l, 0, j)),
        out_shape=jax.ShapeDtypeStruct((depth, r, d3), F32),
        compiler_params=_params("arbitrary", "arbitrary"),
        name="adaln_mod",
    )(cvecs, w_ada, b_ada.reshape(depth, 1, d3))


def _norm_mod_kernel(x_ref, g_ref, mod_ref, o_ref):
    x = x_ref[...]
    y = x * lax.rsqrt(jnp.mean(x * x, axis=-1, keepdims=True) + EPS) * g_ref[...]
    o_ref[...] = (y * (1.0 + mod_ref[1:2, :]) + mod_ref[0:1, :]).astype(o_ref.dtype)


def _norm_mod(x, g, mod):
    bm, t, d = x.shape
    tm = min(t, 2048)
    return pl.pallas_call(
        _norm_mod_kernel,
        grid=(bm, t // tm),
        in_specs=[pl.BlockSpec((None, tm, d), lambda b, i: (b, i, 0)),
                  pl.BlockSpec((1, d), lambda b, i: (0, 0)),
                  pl.BlockSpec((None, 3, d), lambda b, i: (b, 0, 0))],
        out_specs=pl.BlockSpec((None, tm, d), lambda b, i: (b, i, 0)),
        out_shape=jax.ShapeDtypeStruct((bm, t, d), BF16),
        compiler_params=_params("arbitrary", "arbitrary"),
        name="norm_mod",
    )(x, g.reshape(1, d), mod)


def _filter_kernel(w1_ref, b1_ref, w2_ref, b2_ref, freq_ref, w3_00, w3_01, w3_10, w3_11,
                   b3_ref, decay_ref, f_ref, spec_ref, hdn_ref, *, L):
    cb = spec_ref.shape[-1]

    @pl.when(pl.program_id(0) == 0)
    def _():
        fr = -(-(1 + 2 * HY_BANDS) // SUBLANES) * SUBLANES
        r = lax.broadcasted_iota(jnp.int32, (fr, L), 0)
        t = lax.broadcasted_iota(jnp.int32, (fr, L), 1).astype(F32) / L
        band = jnp.where(r <= HY_BANDS, r, r - HY_BANDS).astype(F32)
        ang = 2.0 * math.pi * t * band
        feats = jnp.where(r == 0, t,
                          jnp.where(r <= HY_BANDS, jnp.cos(ang),
                                    jnp.where(r <= 2 * HY_BANDS, jnp.sin(ang), 0.0)))
        feats = jnp.concatenate([feats, jnp.zeros((LANES - fr, L), F32)], axis=0)
        hdn = jnp.sin(freq_ref[:, 0:1] * (_dot_hi(w1_ref[...], feats) + b1_ref[...]))
        hdn = jnp.sin(freq_ref[:, 1:2] * (_dot_hi(w2_ref[...], hdn) + b2_ref[...]))
        hdn_ref[...] = hdn.T

    hdn = hdn_ref[...].astype(BF16)
    row_i = lax.broadcasted_iota(jnp.int32, (L, cb), 0)
    t = row_i.astype(F32) / L
    first = row_i == 0
    sign = jnp.where((row_i & 1) == 0, 1.0, -1.0)
    w3 = ((w3_00, w3_01), (w3_10, w3_11))
    for o in range(HY_ORDER):
        hs = []
        for dr in range(2):
            j = 2 * o + dr
            hv = _dot(hdn, w3[o][dr][...].astype(BF16)) + b3_ref[j:j + 1, :]
            win = jnp.exp(-t * jnp.abs(decay_ref[j:j + 1, :])) + HY_SHIFT
            hs.append(hv * win)
        hf = hs[0]
        hb = jnp.where(first, 0.0, hs[1])
        l1 = jnp.sum(jnp.abs(hf), axis=0, keepdims=True) + jnp.sum(jnp.abs(hb), axis=0, keepdims=True)
        inv_l1 = 1.0 / l1
        even = (hf + hb) * inv_l1
        odd = (hf - hb) * inv_l1
        re = _dot(f_ref[0:L, :], even.astype(BF16))
        im = _dot(f_ref[L:2 * L, :], odd.astype(BF16))
        nyq = jnp.sum(even * sign, axis=0, keepdims=True)
        a = re * jnp.where(first, 0.5 / L, 1.0 / L)
        spec_ref[o, 0] = a.astype(spec_ref.dtype)
        spec_ref[o, 1] = jnp.where(first, 0.0, im * (1.0 / L)).astype(spec_ref.dtype)
        spec_ref[o, 2] = jnp.where(first, nyq * (0.5 / L), a).astype(spec_ref.dtype)


def _hyena_spectra(dft, l, hy_w1, hy_b1, hy_w2, hy_b2, hy_w3, hy_b3, hy_freq, hy_decay):
    emb, ff = hy_w1.shape[1:]
    d_hy = hy_decay.shape[-1]
    cb = HY_CB
    ncb = d_hy // cb
    w1t = jnp.zeros((ff, LANES), F32).at[:, :emb].set(hy_w1[l].T)
    f, _ = dft
    L = f.shape[1]
    const = lambda j: (0, 0)

    def w3_spec(o, dr):
        return pl.BlockSpec((None, ff, cb), lambda j, o=o, dr=dr: (l, 0, (2 * o + dr) * ncb + j))

    return pl.pallas_call(
        functools.partial(_filter_kernel, L=L),
        grid=(ncb,),
        in_specs=[pl.BlockSpec((ff, LANES), const),
                  pl.BlockSpec((ff, 1), const),
                  pl.BlockSpec((ff, ff), const),
                  pl.BlockSpec((ff, 1), const),
                  pl.BlockSpec((ff, 2), const),
                  w3_spec(0, 0), w3_spec(0, 1), w3_spec(1, 0), w3_spec(1, 1),
                  pl.BlockSpec((2 * HY_ORDER, cb), lambda j: (0, j)),
                  pl.BlockSpec((2 * HY_ORDER, cb), lambda j: (0, j)),
                  pl.BlockSpec((2 * L, L), const)],
        out_specs=pl.BlockSpec((HY_ORDER, 3, L, cb), lambda j: (0, 0, 0, j)),
        out_shape=jax.ShapeDtypeStruct((HY_ORDER, 3, L, d_hy), BF16),
        scratch_shapes=[pltpu.VMEM((L, ff), F32)],
        compiler_params=_params("arbitrary"),
        name="hyena_filter",
    )(w1t, hy_b1[l].reshape(ff, 1), hy_w2[l].T, hy_b2[l].reshape(ff, 1), hy_freq[l].T,
      hy_w3, hy_w3, hy_w3, hy_w3,
      hy_b3[l].reshape(2 * HY_ORDER, d_hy), hy_decay[l].reshape(2 * HY_ORDER, d_hy),
      f)


def _short_conv3(x, w_ref, b, seq=None):
    n, c = x.shape
    seq = n if seq is None else seq
    prev = pltpu.roll(x, 1, axis=0)
    nxt = pltpu.roll(x, n - 1, axis=0)
    row = lax.broadcasted_iota(jnp.int32, (SUBLANES, c), 0)
    prev_parts, nxt_parts = [], []
    for s in range(0, n, seq):
        e = s + seq
        prev_parts += [jnp.where(row == 0, 0.0, prev[s:s + SUBLANES]), prev[s + SUBLANES:e]]
        nxt_parts += [nxt[s:e - SUBLANES], jnp.where(row == SUBLANES - 1, 0.0, nxt[e - SUBLANES:e])]
    prev = jnp.concatenate(prev_parts, axis=0)
    nxt = jnp.concatenate(nxt_parts, axis=0)
    return ((b + prev * w_ref[0:1, :]) + x * w_ref[1:2, :]) + nxt * w_ref[2:3, :]


def _hyena_kernel(h_ref, wx1_ref, wx2_ref, wv_ref, wz_ref, cw_ref, cb_ref, spec_ref, hbias_ref,
                  f_ref, g_ref, o_ref, *, nb, L):
    seqs = range(nb)
    wv = wv_ref[...].astype(BF16)
    z = [_short_conv3(_dot_nt(h_ref[b], wv), cw_ref.at[2], cb_ref[2:3, :]) for b in seqs]
    gate_w = (wx1_ref, wx2_ref)
    for o in range(HY_ORDER):
        zf = [_dot(f_ref[...], z[b].astype(BF16)) for b in seqs]
        wg = gate_w[o][...].astype(BF16)
        gate = [_short_conv3(_dot_nt(h_ref[b], wg), cw_ref.at[o], cb_ref[o:o + 1, :]) for b in seqs]
        a, bm, dm = (spec_ref[o, i].astype(F32) for i in range(3))
        ycat = []
        for b in seqs:
            zc, zs = zf[b][0:L], zf[b][L:2 * L]
            ycat.append(jnp.concatenate([zc * a - zs * bm, zc * bm + zs * dm], axis=0).astype(BF16))
        y = [_dot(g_ref[...], ycat[b]) for b in seqs]
        z = [gate[b] * (y[b] + hbias_ref[o:o + 1, :] * z[b]) for b in seqs]
    wz = wz_ref[...].astype(BF16)
    for b in seqs:
        o_ref[b] = (z[b] * _silu(_dot_nt(h_ref[b], wz))).astype(o_ref.dtype)


def _hyena(h, l, w_in, conv_w, conv_b, spec, hy_bias, dft):
    bsz, L, d = h.shape
    d_hy = hy_bias.shape[-1]
    cb = HY_CB
    ncb = d_hy // cb
    nb = max(1, min(bsz, HY_ROWS // L))
    f, g = dft
    once = pl.Buffered(1)

    def w_spec(seg):
        return pl.BlockSpec((None, cb, d), lambda j, i, seg=seg: (l, seg * ncb + j, 0))

    return pl.pallas_call(
        functools.partial(_hyena_kernel, nb=nb, L=L),
        grid=(ncb, bsz // nb),
        in_specs=[pl.BlockSpec((nb, L, d), lambda j, i: (i, 0, 0)),
                  w_spec(0), w_spec(1), w_spec(2), w_spec(3),
                  pl.BlockSpec((3, 3, cb), lambda j, i: (0, 0, j)),
                  pl.BlockSpec((3, cb), lambda j, i: (0, j)),
                  pl.BlockSpec((HY_ORDER, 3, L, cb), lambda j, i: (0, 0, 0, j)),
                  pl.BlockSpec((HY_ORDER, cb), lambda j, i: (0, j)),
                  pl.BlockSpec((2 * L, L), lambda j, i: (0, 0), pipeline_mode=once),
                  pl.BlockSpec((L, 2 * L), lambda j, i: (0, 0), pipeline_mode=once)],
        out_specs=pl.BlockSpec((nb, L, cb), lambda j, i: (i, 0, j)),
        out_shape=jax.ShapeDtypeStruct((bsz, L, d_hy), BF16),
        compiler_params=_params("arbitrary", "arbitrary"),
        name="hyena",
    )(h, w_in, w_in, w_in, w_in,
      conv_w.reshape(3, 3, d_hy).transpose(1, 0, 2), conv_b.reshape(3, d_hy), spec, hy_bias, f, g)


def _mlstm_kernel(*refs, nb, L, has_state, n_aliased):
    (h_ref, wq_ref, wk_ref, wv_ref, wo_ref, wz_ref, wg_ref, gb_ref, cwq_ref, cwk_ref,
     cbq_ref, cbk_ref, ng_ref) = refs[:13]
    refs = refs[13:]
    if has_state:
        c0_ref, n0_ref, m0_ref, yb_ref = refs[:4]
        refs = refs[4:]
    else:
        yb_ref, cn_ref, nn_ref, mn_ref = refs[n_aliased:n_aliased + 4]
        refs = refs[n_aliased + 4:]
    q_s, kt_s, v_s, g2_s, g2t_s, br_s, sc_s, hsum_s, og_s, c_s = refs

    T = ML_CHUNK
    nc = L // T
    R = nb * L
    d = h_ref.shape[-1]
    dh = q_s.shape[-1]
    hb = h_ref[...].reshape(R, d)

    g = _dot_nt(hb, wg_ref[...]) + gb_ref[...]
    lf = _log_sigmoid(g)
    ti = lax.broadcasted_iota(jnp.int32, (T, T), 0)
    si = lax.broadcasted_iota(jnp.int32, (T, T), 1)
    causal = si <= ti
    anti = si >= ti
    lower = causal.astype(F32)
    upper = anti.astype(F32)
    lane = lax.broadcasted_iota(jnp.int32, (T, LANES), 1)
    for j in range(nb * nc):
        rows = slice(j * T, (j + 1) * T)
        pre = _dot_hi(lower, lf[rows])
        suf = _dot_hi(upper, lf[rows])
        g2_s[rows, :] = jnp.where(lane == 1, pre, jnp.where(lane == 3, suf, g[rows]))
        br_s[0, rows, :] = jnp.broadcast_to(pre[:, 1:2], (T, LANES))
        br_s[1, rows, :] = jnp.broadcast_to(suf[:, 3:4], (T, LANES))
    g2t_s[...] = g2_s[...].T[0:8, :]

    zero11 = jnp.zeros((1, 1), F32)
    m_fin = []
    for b in range(nb):
        vals = [[None] * 6 for _ in range(nc)]
        for dr in range(2):
            m = m0_ref[b, dr][:, 0:1] if has_state else zero11
            for ci in range(nc):
                c = ci if dr == 0 else nc - 1 - ci
                cols = slice(b * L + c * T, b * L + (c + 1) * T)
                irow = g2t_s[2 * dr:2 * dr + 1, cols]
                brow = g2t_s[2 * dr + 1:2 * dr + 2, cols]
                bl = brow[:, T - 1:T] if dr == 0 else brow[:, 0:1]
                mw = jnp.max(bl - brow + irow, axis=-1, keepdims=True)
                mn = jnp.maximum(bl + m, mw)
                vals[c][3 * dr:3 * dr + 3] = [m, mn, bl]
                m = mn
            m_fin.append(m)
        for c in range(nc):
            rows8 = [jnp.broadcast_to(v, (1, LANES)) for v in vals[c]] + [jnp.zeros((2, LANES), F32)]
            sc_s[b * nc + c] = jnp.concatenate(rows8, axis=0)

    uq =_dot_nt(hb, wq_ref[...].astype(BF16))
    uk = _dot_nt(hb, wk_ref[...].astype(BF16))
    q_s[...] = _silu(_short_conv3(uq, cwq_ref, cbq_ref[...], L)).astype(BF16)
    uv = _dot_nt(hb, wv_ref[...].astype(BF16))
    k = _silu(_short_conv3(uk, cwk_ref, cbk_ref[...], L)) * (dh ** -0.5)
    kt_s[...] = k.T.astype(BF16)
    uo = _dot_nt(hb, wo_ref[...].astype(BF16))
    v_s[:, 0:dh] = uv.astype(BF16)
    v_s[:, dh:dh + LANES] = jnp.ones((R, LANES), BF16)
    uz = _dot_nt(hb, wz_ref[...].astype(BF16))
    og_s[...] = _sigmoid(uo) * _silu(uz)

    hsum_s[...] = jnp.zeros_like(hsum_s)
    for b in range(nb):
        for dr in range(2):
            if has_state:
                c_s[2 * b + dr, :, 0:dh] = c0_ref[b, dr]
                c_s[2 * b + dr, :, dh:dh + LANES] = jnp.broadcast_to(n0_ref[b, dr], (LANES, dh)).T
            else:
                c_s[2 * b + dr] = jnp.zeros((dh, dh + LANES), F32)

    def step(ci, carry):
        for b in range(nb):
            for dr in range(2):
                c = ci if dr == 0 else nc - 1 - ci
                r0 = pl.multiple_of(b * L + c * T, T)
                qc = q_s[pl.ds(r0, T), :]
                vc = v_s[pl.ds(r0, T), :]
                ktc = kt_s[:, pl.ds(r0, T)]
                b_rep = br_s[dr, pl.ds(r0, T), :]
                g2tc = g2t_s[:, pl.ds(r0, T)]
                irow = g2tc[2 * dr:2 * dr + 1, :]
                brow = g2tc[2 * dr + 1:2 * dr + 2, :]
                sc = sc_s[b * nc + c]
                m_prev = sc[3 * dr:3 * dr + 1, :]
                m_new = sc[3 * dr + 1:3 * dr + 2, :]
                bl = sc[3 * dr + 2:3 * dr + 3, :]
                c_old = c_s[2 * b + dr]

                dlog = jnp.where(causal if dr == 0 else anti, b_rep - brow + irow, NEG)
                inter = b_rep + m_prev
                mt = jnp.maximum(inter, jnp.max(dlog, axis=-1, keepdims=True))
                smat = _dot(qc, ktc) * jnp.exp(dlog - mt)
                iw = jnp.exp(inter - mt)
                ext = (_dot(smat.astype(BF16), vc)
                       + jnp.concatenate([iw, iw, iw], axis=-1) * _dot(qc, c_old.astype(BF16)))
                inv = 1.0 / jnp.maximum(jnp.abs(ext[:, dh:dh + LANES]), jnp.exp(-mt))
                hh = ext[:, 0:dh] * jnp.concatenate([inv, inv], axis=-1)
                hsum_s[pl.ds(r0, T), :] = hsum_s[pl.ds(r0, T), :] + hh

                ws = jnp.exp(bl - brow + irow - m_new)
                dec = jnp.exp(bl + m_prev - m_new)[:, 0:1]
                kw = (ktc.astype(F32) * ws).astype(BF16)
                c_s[2 * b + dr] = dec * c_old + _dot(kw, vc)
        return carry

    lax.fori_loop(0, nc, step, 0, unroll=max(1, min(nc, ML_STREAMS // (2 * nb))))
    if not has_state:
        for b in range(nb):
            for dr in range(2):
                cn_ref[b, dr] = c_s[2 * b + dr, :, 0:dh]
                nn_ref[b, dr] = c_s[2 * b + dr, :, dh:dh + LANES].T[0:1, :]
                mn_ref[b, dr] = jnp.broadcast_to(m_fin[2 * b + dr], (1, LANES))

    hs = hsum_s[...]
    hm = hs * lax.rsqrt(jnp.mean(hs * hs, axis=-1, keepdims=True) + EPS) * ng_ref[...]
    yb_ref[...] = (hm * og_s[...]).astype(yb_ref.dtype).reshape(yb_ref.shape)


def _mlstm(h, l, depth, w_in, wg_heads, gb_heads, conv_w, conv_b, norm_g, states, carried,
           seg_off):
    bsz, L, d = h.shape
    heads = wg_heads.shape[0]
    d_ml = norm_g.shape[-1]
    dh = d_ml // heads
    has_state = states is not None
    nb = max(1, min(bsz, ML_ROWS // L))
    nc = L // ML_CHUNK
    qo, ko, vo, oo, zo = (s // dh for s in seg_off)

    def w_spec(off):
        return pl.BlockSpec((None, dh, d), lambda b, hd, off=off: (l, off + hd, 0))

    def vec_spec(rows, off):
        return pl.BlockSpec((rows, dh), lambda b, hd, off=off: (0, off + hd))

    in_specs = [pl.BlockSpec((nb, L, d), lambda b, hd: (b, 0, 0)),
                w_spec(qo), w_spec(ko), w_spec(vo), w_spec(oo), w_spec(zo),
                pl.BlockSpec((None, LANES, d), lambda b, hd: (hd, 0, 0)),
                pl.BlockSpec((None, 1, LANES), lambda b, hd: (hd, 0, 0)),
                vec_spec(3, 0), vec_spec(3, heads), vec_spec(1, 0), vec_spec(1, heads),
                vec_spec(1, 0)]
    args = [h, w_in, w_in, w_in, w_in, w_in, wg_heads, gb_heads,
            conv_w, conv_w, conv_b.reshape(1, -1), conv_b.reshape(1, -1), norm_g.reshape(1, -1)]
    yb_spec = pl.BlockSpec((nb, L, dh), lambda b, hd: (b, 0, hd))
    yb_shape = jax.ShapeDtypeStruct((bsz, L, d_ml), BF16)
    state_idx = lambda b, hd: (b, l, 0, hd, 0, 0)
    c_spec = pl.BlockSpec((nb, None, 2, None, dh, dh), state_idx)
    n_spec = pl.BlockSpec((nb, None, 2, None, 1, dh), state_idx)
    m_spec = pl.BlockSpec((nb, None, 2, None, 1, LANES), state_idx)
    aliases = {}
    if has_state:
        in_specs += [c_spec, n_spec, m_spec]
        args += list(states)
        out_specs, out_shape = yb_spec, yb_shape
    else:
        if carried is not None:
            aliases = {len(args) + i: 1 + i for i in range(3)}
            in_specs += [pl.BlockSpec(memory_space=pl.ANY)] * 3
            args += list(carried)
        out_specs = (yb_spec, c_spec, n_spec, m_spec)
        out_shape = (yb_shape,
                     jax.ShapeDtypeStruct((bsz, depth, 2, heads, dh, dh), F32),
                     jax.ShapeDtypeStruct((bsz, depth, 2, heads, 1, dh), F32),
                     jax.ShapeDtypeStruct((bsz, depth, 2, heads, 1, LANES), F32))
    rows = nb * L
    scratch = [pltpu.VMEM((rows, dh), BF16),
               pltpu.VMEM((dh, rows), BF16),
               pltpu.VMEM((rows, dh + LANES), BF16),
               pltpu.VMEM((rows, LANES), F32),
               pltpu.VMEM((8, rows), F32),
               pltpu.VMEM((2, rows, LANES), F32),
               pltpu.VMEM((nb * nc, 8, LANES), F32),
               pltpu.VMEM((rows, dh), F32),
               pltpu.VMEM((rows, dh), F32),
               pltpu.VMEM((2 * nb, dh, dh + LANES), F32)]
    return pl.pallas_call(
        functools.partial(_mlstm_kernel, nb=nb, L=L, has_state=has_state, n_aliased=len(aliases)),
        grid=(bsz // nb, heads),
        in_specs=in_specs,
        out_specs=out_specs,
        out_shape=out_shape,
        input_output_aliases=aliases,
        scratch_shapes=scratch,
        compiler_params=_params("arbitrary", "arbitrary"),
        name="mlstm",
    )(*args)


OUT_WEIGHTS = 5
OUT_STAGES = 2


def _out_kernel(x_ref, h_ref, ya_ref, yb_ref, mod_ref, wt_hbm, wpa_hbm, wpb_hbm, wout_hbm,
                g_ref, *rest, last, l, off_ga):
    w_s, stage, sem = rest[-3:]
    rest = rest[:-3]
    d = w_s.shape[-1]
    srcs = (wt_hbm.at[l, pl.ds(off_ga, d), :], wt_hbm.at[l, pl.ds(off_ga + d, d), :],
            wpa_hbm.at[l], wpb_hbm.at[l], wout_hbm.at[l])

    def copy(i):
        slot = i % OUT_STAGES
        return pltpu.make_async_copy(srcs[i], stage.at[slot], sem.at[slot])

    def fetched(i):
        copy(i).wait()
        w_s[i] = stage[i % OUT_STAGES].astype(BF16)
        if i + OUT_STAGES < OUT_WEIGHTS:
            copy(i + OUT_STAGES).start()
        return w_s[i]

    def body(weight):
        hb = h_ref[...]
        ga = _sigmoid(_dot_nt(hb, weight(0)))
        gb = _sigmoid(_dot_nt(hb, weight(1)))
        merged = ga * _dot(ya_ref[...], weight(2)) + gb * _dot(yb_ref[...], weight(3))
        xn = x_ref[...] + mod_ref[2:3, :] * _dot(merged.astype(BF16), weight(4))
        normed = xn * lax.rsqrt(jnp.mean(xn * xn, axis=-1, keepdims=True) + EPS) * g_ref[...]
        if last:
            (y_ref,) = rest
            y_ref[...] = normed
        else:
            modn_ref, xo_ref, ho_ref = rest
            xo_ref[...] = xn
            ho_ref[...] = (normed * (1.0 + modn_ref[1:2, :]) + modn_ref[0:1, :]).astype(ho_ref.dtype)

    first = (pl.program_id(0) == 0) & (pl.program_id(1) == 0)

    @pl.when(first)
    def _():
        for i in range(OUT_STAGES):
            copy(i).start()
        body(fetched)

    @pl.when(jnp.logical_not(first))
    def _():
        body(lambda i: w_s[i])


def _out(x, h, ya, yb, mod, l, w_t, off_ga, w_pa, w_pb, w_out, g_next, mod_next):
    bm, t, d = x.shape
    tm = min(t, OUT_TM)
    last = mod_next is None
    tok = lambda b, i: (b, i, 0)
    tok_spec = pl.BlockSpec((None, tm, d), tok)
    mod_spec = pl.BlockSpec((None, 3, d), lambda b, i: (b, 0, 0))
    hbm = pl.BlockSpec(memory_space=pl.ANY)
    in_specs = [tok_spec, tok_spec, tok_spec, tok_spec, mod_spec, hbm, hbm, hbm, hbm,
                pl.BlockSpec((1, d), lambda b, i: (0, 0))]
    args = [x, h, ya, yb, mod, w_t, w_pa, w_pb, w_out, g_next.reshape(1, d)]
    if last:
        out_specs, out_shape = tok_spec, jax.ShapeDtypeStruct((bm, t, d), F32)
    else:
        in_specs.append(mod_spec)
        args.append(mod_next)
        out_specs = (tok_spec, tok_spec)
        out_shape = (jax.ShapeDtypeStruct((bm, t, d), F32), jax.ShapeDtypeStruct((bm, t, d), BF16))
    return pl.pallas_call(
        functools.partial(_out_kernel, last=last, l=l, off_ga=off_ga),
        grid=(bm, t // tm),
        in_specs=in_specs,
        out_specs=out_specs,
        out_shape=out_shape,
        scratch_shapes=[pltpu.VMEM((OUT_WEIGHTS, d, d), BF16),
                        pltpu.VMEM((OUT_STAGES, d, d), F32),
                        pltpu.SemaphoreType.DMA((OUT_STAGES,))],
        compiler_params=_params("arbitrary", "arbitrary"),
        name="merge_out",
    )(*args)


def kernel(x_prompt, x_sample, state_C, state_n, state_m, c, c_ctx, norm_g, w_ada, b_ada, w_in, hy_conv_w, hy_conv_b, hy_w1, hy_b1, hy_w2, hy_b2, hy_w3, hy_b3, hy_freq, hy_decay, hy_bias, ml_conv_w, ml_conv_b, ml_if_b, ml_norm_g, w_pa, w_pb, w_out, final_g):
    depth, d, _ = w_in.shape
    d_hy = hy_bias.shape[-1]
    d_ml = ml_norm_g.shape[-1]
    heads = ml_if_b.shape[-1]
    bp, lp, _ = x_prompt.shape
    bs, ls, _ = x_sample.shape

    off_q = 4 * d_hy
    off_v = off_q + 2 * d_ml
    off_o = off_v + d_ml
    off_z = off_o + d_ml
    off_g = off_z + d_ml
    off_ga = off_g + 4 * heads
    seg_off = (off_q, off_q + d_ml, off_v, off_o, off_z)

    w_t = jnp.transpose(w_in, (0, 2, 1))
    wg = w_t[:, off_g:off_ga, :].reshape(depth, 4, heads, d).transpose(0, 2, 1, 3)
    wg_heads = jnp.zeros((depth, heads, LANES, d), BF16).at[:, :, :4, :].set(wg.astype(BF16))
    gb = ml_if_b.reshape(depth, 4, heads).transpose(0, 2, 1)
    gb_heads = jnp.zeros((depth, heads, 1, LANES), F32).at[:, :, 0, :4].set(gb)

    cvecs = jnp.zeros((8, d), F32).at[0].set(c_ctx).at[1:1 + bs].set(c)
    mods = _mods(cvecs, w_ada, b_ada).reshape(depth, 8, 3, d)

    xp = x_prompt.reshape(1, bp * lp, d)
    xs = x_sample
    cached = (state_C, state_n.reshape(bs, depth, 2, heads, 1, -1),
              jnp.broadcast_to(state_m[..., None, None], (bs, depth, 2, heads, 1, LANES)))
    fin = None
    hp = _norm_mod(xp, norm_g[0], mods[0, 0:1])
    hs = _norm_mod(xs, norm_g[0], mods[0, 1:1 + bs])
    dft = {L: _dft_operands(L) for L in sorted({lp, ls})}
    for l in range(depth):
        spectra = {}
        for L in sorted({lp, ls}):
            spectra[L] = _hyena_spectra(dft[L], l, hy_w1, hy_b1, hy_w2, hy_b2, hy_w3, hy_b3,
                                        hy_freq, hy_decay)

        last = l == depth - 1
        g_next = final_g if last else norm_g[l + 1]

        def layer(x, h, rows, bsz, L, states, carried):
            mod = mods[l, rows]
            hseq = h.reshape(bsz, L, d)
            ya = _hyena(hseq, l, w_t, hy_conv_w[l], hy_conv_b[l], spectra[L], hy_bias[l], dft[L])
            res = _mlstm(hseq, l, depth, w_t, wg_heads[l], gb_heads[l], ml_conv_w[l],
                         ml_conv_b[l], ml_norm_g[l], states, carried, seg_off)
            yb, fin = (res, None) if states is not None else (res[0], res[1:])
            res = _out(x, h, ya.reshape(x.shape), yb.reshape(x.shape), mod, l,
                       w_t, off_ga, w_pa, w_pb, w_out, g_next,
                       None if last else mods[l + 1, rows])
            return (res, None, fin) if last else (res[0], res[1], fin)

        xp, hp, fin = layer(xp, hp, slice(0, 1), bp, lp, None, fin)
        xs, hs, _ = layer(xs, hs, slice(1, 1 + bs), bs, ls, cached, None)

    return (xp.reshape(bp, lp, d), xs, fin[0], fin[1][:, :, :, :, 0, :], fin[2][:, :, :, :, 0, 0])
```

```python
import functools
import math

import numpy as np
import jax
import jax.numpy as jnp
from jax import lax
from jax.experimental import pallas as pl
from jax.experimental.pallas import tpu as pltpu

F32 = jnp.float32
BF16 = jnp.bfloat16
HIGHEST = lax.Precision.HIGHEST

HY_ORDER = 2
HY_BANDS = 16
HY_SHIFT = 0.05
ML_CHUNK = 128
EPS = 1e-6
NEG = -1e30

LANES = 128
SUBLANES = 8
VMEM_LIMIT = 58 * 1024 * 1024
HY_CB = 256
ML_ROWS = 2048
HY_ROWS = 2048
ML_STREAMS = 8
OUT_TM = 512


def _dot(a, b):
    return jnp.dot(a, b, preferred_element_type=F32)


def _dot_nt(a, bt):
    return lax.dot_general(a, bt, (((1,), (1,)), ((), ())), preferred_element_type=F32)


def _dot_hi(a, b):
    return jnp.dot(a, b, preferred_element_type=F32, precision=HIGHEST)


def _sigmoid(x):
    return 1.0 / (1.0 + jnp.exp(-x))


def _silu(x):
    return x * _sigmoid(x)


def _log_sigmoid(x):
    return jnp.minimum(x, 0.0) - jnp.log(1.0 + jnp.exp(-jnp.abs(x)))


def _params(*sem):
    return pltpu.CompilerParams(dimension_semantics=sem, vmem_limit_bytes=VMEM_LIMIT)


@functools.lru_cache(maxsize=None)
def _dft_table(L):
    N = 2 * L
    k = np.arange(L, dtype=np.int64)[:, None]
    n = np.arange(L, dtype=np.int64)[None, :]
    ang = 2.0 * np.pi * ((k * n) % N).astype(np.float64) / N
    c = np.cos(ang)
    s = -np.sin(ang)
    s[0, :] = 1.0 - 2.0 * (np.arange(L) % 2)
    return np.concatenate([c, s], axis=0).astype(np.float32)


def _dft_operands(L):
    f = jnp.asarray(_dft_table(L)).astype(BF16)
    return f, f.T


def _split_bf16(x):
    hi = x.astype(BF16)
    return hi, (x - hi.astype(F32)).astype(BF16)


def _mod_kernel(c_ref, w_ref, b_ref, o_ref):
    a_hi, a_lo = _split_bf16(_silu(c_ref[...]))
    w_hi, w_lo = _split_bf16(w_ref[...])
    o_ref[...] = (_dot(a_hi, w_hi) + (_dot(a_lo, w_hi) + _dot(a_hi, w_lo))) + b_ref[...]


def _mods(cvecs, w_ada, b_ada):
    depth, d, d3 = w_ada.shape
    r = cvecs.shape[0]
    tn = 1024
    return pl.pallas_call(
        _mod_kernel,
        grid=(depth, d3 // tn),
        in_specs=[pl.BlockSpec((r, d), lambda l, j: (0, 0)),
                  pl.BlockSpec((None, d, tn), lambda l, j: (l, 0, j)),
                  pl.BlockSpec((None, 1, tn), lambda l, j: (l, 0, j))],
        out_specs=pl.BlockSpec((None, r, tn), lambda l, j: (l, 0, j)),
        out_shape=jax.ShapeDtypeStruct((depth, r, d3), F32),
        compiler_params=_params("arbitrary", "arbitrary"),
        name="adaln_mod",
    )(cvecs, w_ada, b_ada.reshape(depth, 1, d3))


def _norm_mod_kernel(x_ref, g_ref, mod_ref, o_ref):
    x = x_ref[...]
    y = x * lax.rsqrt(jnp.mean(x * x, axis=-1, keepdims=True) + EPS) * g_ref[...]
    o_ref[...] = (y * (1.0 + mod_ref[1:2, :]) + mod_ref[0:1, :]).astype(o_ref.dtype)


def _mod_spec(d, l, row0):
    return pl.BlockSpec((None, None, 3, d), lambda b, i: (l, row0 + b, 0, 0))


def _norm_mod(x, norm_g, mods, l, row0):
    bm, t, d = x.shape
    tm = min(t, 2048)
    return pl.pallas_call(
        _norm_mod_kernel,
        grid=(bm, t // tm),
        in_specs=[pl.BlockSpec((None, tm, d), lambda b, i: (b, i, 0)),
                  pl.BlockSpec((None, 1, d), lambda b, i: (l, 0, 0)),
                  _mod_spec(d, l, row0)],
        out_specs=pl.BlockSpec((None, tm, d), lambda b, i: (b, i, 0)),
        out_shape=jax.ShapeDtypeStruct((bm, t, d), BF16),
        compiler_params=_params("arbitrary", "arbitrary"),
        name="norm_mod",
    )(x, norm_g, mods)


def _filter_kernel(w1_ref, b1_ref, w2_ref, b2_ref, freq_ref, w3_00, w3_01, w3_10, w3_11,
                   b3_ref, decay_ref, f_ref, spec_ref, hdn_ref, *, L):
    cb = spec_ref.shape[-1]

    @pl.when(pl.program_id(0) == 0)
    def _():
        fr = -(-(1 + 2 * HY_BANDS) // SUBLANES) * SUBLANES
        r = lax.broadcasted_iota(jnp.int32, (fr, L), 0)
        t = lax.broadcasted_iota(jnp.int32, (fr, L), 1).astype(F32) / L
        band = jnp.where(r <= HY_BANDS, r, r - HY_BANDS).astype(F32)
        ang = 2.0 * math.pi * t * band
        feats = jnp.where(r == 0, t,
                          jnp.where(r <= HY_BANDS, jnp.cos(ang),
                                    jnp.where(r <= 2 * HY_BANDS, jnp.sin(ang), 0.0)))
        feats = jnp.concatenate([feats, jnp.zeros((LANES - fr, L), F32)], axis=0)
        hdn = jnp.sin(freq_ref[:, 0:1] * (_dot_hi(w1_ref[...], feats) + b1_ref[...]))
        hdn = jnp.sin(freq_ref[:, 1:2] * (_dot_hi(w2_ref[...], hdn) + b2_ref[...]))
        hdn_ref[...] = hdn.T

    hdn = hdn_ref[...].astype(BF16)
    row_i = lax.broadcasted_iota(jnp.int32, (L, cb), 0)
    t = row_i.astype(F32) / L
    first = row_i == 0
    sign = jnp.where((row_i & 1) == 0, 1.0, -1.0)
    w3 = ((w3_00, w3_01), (w3_10, w3_11))
    for o in range(HY_ORDER):
        hs = []
        for dr in range(2):
            j = 2 * o + dr
            hv = _dot(hdn, w3[o][dr][...].astype(BF16)) + b3_ref[j:j + 1, :]
            win = jnp.exp(-t * jnp.abs(decay_ref[j:j + 1, :])) + HY_SHIFT
            hs.append(hv * win)
        hf = hs[0]
        hb = jnp.where(first, 0.0, hs[1])
        l1 = jnp.sum(jnp.abs(hf), axis=0, keepdims=True) + jnp.sum(jnp.abs(hb), axis=0, keepdims=True)
        inv_l1 = 1.0 / l1
        even = (hf + hb) * inv_l1
        odd = (hf - hb) * inv_l1
        re = _dot(f_ref[0:L, :], even.astype(BF16))
        im = _dot(f_ref[L:2 * L, :], odd.astype(BF16))
        nyq = jnp.sum(even * sign, axis=0, keepdims=True)
        a = re * jnp.where(first, 0.5 / L, 1.0 / L)
        spec_ref[o, 0] = a.astype(spec_ref.dtype)
        spec_ref[o, 1] = jnp.where(first, 0.0, im * (1.0 / L)).astype(spec_ref.dtype)
        spec_ref[o, 2] = jnp.where(first, nyq * (0.5 / L), a).astype(spec_ref.dtype)


def _filter_params(hy_w1, hy_b1, hy_w2, hy_b2, hy_w3, hy_b3, hy_freq, hy_decay):
    depth, emb, ff = hy_w1.shape
    d_hy = hy_decay.shape[-1]
    w1t = jnp.zeros((depth, ff, LANES), F32).at[:, :, :emb].set(hy_w1.transpose(0, 2, 1))
    return (w1t, hy_b1.reshape(depth, ff, 1), hy_w2.transpose(0, 2, 1), hy_b2.reshape(depth, ff, 1),
            hy_freq.transpose(0, 2, 1), hy_w3, hy_b3.reshape(depth, 2 * HY_ORDER, d_hy),
            hy_decay.reshape(depth, 2 * HY_ORDER, d_hy))


def _hyena_spectra(dft, l, filter_params):
    w1t, b1, w2t, b2, freq, w3, b3, decay = filter_params
    ff = w1t.shape[1]
    d_hy = decay.shape[-1]
    cb = HY_CB
    ncb = d_hy // cb
    f, _ = dft
    L = f.shape[1]
    layer = lambda j: (l, 0, 0)

    def w3_spec(o, dr):
        return pl.BlockSpec((None, ff, cb), lambda j, o=o, dr=dr: (l, 0, (2 * o + dr) * ncb + j))

    return pl.pallas_call(
        functools.partial(_filter_kernel, L=L),
        grid=(ncb,),
        in_specs=[pl.BlockSpec((None, ff, LANES), layer),
                  pl.BlockSpec((None, ff, 1), layer),
                  pl.BlockSpec((None, ff, ff), layer),
                  pl.BlockSpec((None, ff, 1), layer),
                  pl.BlockSpec((None, ff, 2), layer),
                  w3_spec(0, 0), w3_spec(0, 1), w3_spec(1, 0), w3_spec(1, 1),
                  pl.BlockSpec((None, 2 * HY_ORDER, cb), lambda j: (l, 0, j)),
                  pl.BlockSpec((None, 2 * HY_ORDER, cb), lambda j: (l, 0, j)),
                  pl.BlockSpec((2 * L, L), lambda j: (0, 0))],
        out_specs=pl.BlockSpec((HY_ORDER, 3, L, cb), lambda j: (0, 0, 0, j)),
        out_shape=jax.ShapeDtypeStruct((HY_ORDER, 3, L, d_hy), BF16),
        scratch_shapes=[pltpu.VMEM((L, ff), F32)],
        compiler_params=_params("arbitrary"),
        name="hyena_filter",
    )(w1t, b1, w2t, b2, freq, w3, w3, w3, w3, b3, decay, f)


def _short_conv3(x, w_ref, b, seq=None):
    n, c = x.shape
    seq = n if seq is None else seq
    prev = pltpu.roll(x, 1, axis=0)
    nxt = pltpu.roll(x, n - 1, axis=0)
    row = lax.broadcasted_iota(jnp.int32, (SUBLANES, c), 0)
    prev_parts, nxt_parts = [], []
    for s in range(0, n, seq):
        e = s + seq
        prev_parts += [jnp.where(row == 0, 0.0, prev[s:s + SUBLANES]), prev[s + SUBLANES:e]]
        nxt_parts += [nxt[s:e - SUBLANES], jnp.where(row == SUBLANES - 1, 0.0, nxt[e - SUBLANES:e])]
    prev = jnp.concatenate(prev_parts, axis=0)
    nxt = jnp.concatenate(nxt_parts, axis=0)
    return ((b + prev * w_ref[0:1, :]) + x * w_ref[1:2, :]) + nxt * w_ref[2:3, :]


def _hyena_kernel(h_ref, wx1_ref, wx2_ref, wv_ref, wz_ref, cw_ref, cb_ref, spec_ref, hbias_ref,
                  f_ref, g_ref, o_ref, *, nb, L):
    seqs = range(nb)
    wv = wv_ref[...].astype(BF16)
    z = [_short_conv3(_dot_nt(h_ref[b], wv), cw_ref.at[2], cb_ref[2:3, :]) for b in seqs]
    gate_w = (wx1_ref, wx2_ref)
    for o in range(HY_ORDER):
        zf = [_dot(f_ref[...], z[b].astype(BF16)) for b in seqs]
        wg = gate_w[o][...].astype(BF16)
        gate = [_short_conv3(_dot_nt(h_ref[b], wg), cw_ref.at[o], cb_ref[o:o + 1, :]) for b in seqs]
        a, bm, dm = (spec_ref[o, i].astype(F32) for i in range(3))
        ycat = []
        for b in seqs:
            zc, zs = zf[b][0:L], zf[b][L:2 * L]
            ycat.append(jnp.concatenate([zc * a - zs * bm, zc * bm + zs * dm], axis=0).astype(BF16))
        y = [_dot(g_ref[...], ycat[b]) for b in seqs]
        z = [gate[b] * (y[b] + hbias_ref[o:o + 1, :] * z[b]) for b in seqs]
    wz = wz_ref[...].astype(BF16)
    for b in seqs:
        o_ref[b] = (z[b] * _silu(_dot_nt(h_ref[b], wz))).astype(o_ref.dtype)


def _hyena(h, l, w_t, conv_w, conv_b, spec, hy_bias, dft):
    bsz, L, d = h.shape
    d_hy = hy_bias.shape[-1]
    cb = HY_CB
    ncb = d_hy // cb
    nb = max(1, min(bsz, HY_ROWS // L))
    f, g = dft
    once = pl.Buffered(1)

    def w_spec(seg):
        return pl.BlockSpec((None, cb, d), lambda j, i, seg=seg: (l, seg * ncb + j, 0))

    return pl.pallas_call(
        functools.partial(_hyena_kernel, nb=nb, L=L),
        grid=(ncb, bsz // nb),
        in_specs=[pl.BlockSpec((nb, L, d), lambda j, i: (i, 0, 0)),
                  w_spec(0), w_spec(1), w_spec(2), w_spec(3),
                  pl.BlockSpec((None, 3, 3, cb), lambda j, i: (l, 0, 0, j)),
                  pl.BlockSpec((None, 3, cb), lambda j, i: (l, 0, j)),
                  pl.BlockSpec((HY_ORDER, 3, L, cb), lambda j, i: (0, 0, 0, j)),
                  pl.BlockSpec((None, HY_ORDER, cb), lambda j, i: (l, 0, j)),
                  pl.BlockSpec((2 * L, L), lambda j, i: (0, 0), pipeline_mode=once),
                  pl.BlockSpec((L, 2 * L), lambda j, i: (0, 0), pipeline_mode=once)],
        out_specs=pl.BlockSpec((nb, L, cb), lambda j, i: (i, 0, j)),
        out_shape=jax.ShapeDtypeStruct((bsz, L, d_hy), BF16),
        compiler_params=_params("arbitrary", "arbitrary"),
        name="hyena",
    )(h, w_t, w_t, w_t, w_t, conv_w, conv_b, spec, hy_bias, f, g)


def _mlstm_kernel(*refs, nb, L, has_state, n_aliased):
    (h_ref, wq_ref, wk_ref, wv_ref, wo_ref, wz_ref, wg_ref, gb_ref, cwq_ref, cwk_ref,
     cbq_ref, cbk_ref, ng_ref) = refs[:13]
    refs = refs[13:]
    if has_state:
        c0_ref, n0_ref, m0_ref, yb_ref = refs[:4]
        refs = refs[4:]
    else:
        yb_ref, cn_ref, nn_ref, mn_ref = refs[n_aliased:n_aliased + 4]
        refs = refs[n_aliased + 4:]
    q_s, kt_s, v_s, g2_s, g2t_s, sc_s, hdir_s, og_s, c_s = refs

    T = ML_CHUNK
    nc = L // T
    R = nb * L
    d = h_ref.shape[-1]
    dh = q_s.shape[-1]
    hb = h_ref[...].reshape(R, d)

    g = _dot_nt(hb, wg_ref[...]) + gb_ref[...]
    lf = _log_sigmoid(g)
    ti = lax.broadcasted_iota(jnp.int32, (T, T), 0)
    si = lax.broadcasted_iota(jnp.int32, (T, T), 1)
    causal = si <= ti
    anti = si >= ti
    lower = causal.astype(F32)
    upper = anti.astype(F32)
    lane = lax.broadcasted_iota(jnp.int32, (T, LANES), 1)
    for j in range(nb * nc):
        rows = slice(j * T, (j + 1) * T)
        pre = _dot_hi(lower, lf[rows])
        suf = _dot_hi(upper, lf[rows])
        g2_s[rows, :] = jnp.where(lane == 1, pre, jnp.where(lane == 3, suf, g[rows]))
    g2t_s[...] = g2_s[...].T[0:8, :]

    zero11 = jnp.zeros((1, 1), F32)
    m_fin = []
    for b in range(nb):
        vals = [[None] * 6 for _ in range(nc)]
        for dr in range(2):
            m = m0_ref[b, dr][:, 0:1] if has_state else zero11
            for ci in range(nc):
                c = ci if dr == 0 else nc - 1 - ci
                cols = slice(b * L + c * T, b * L + (c + 1) * T)
                irow = g2t_s[2 * dr:2 * dr + 1, cols]
                brow = g2t_s[2 * dr + 1:2 * dr + 2, cols]
                bl = brow[:, T - 1:T] if dr == 0 else brow[:, 0:1]
                mw = jnp.max(bl - brow + irow, axis=-1, keepdims=True)
                mn = jnp.maximum(bl + m, mw)
                vals[c][3 * dr:3 * dr + 3] = [m, mn, bl]
                m = mn
            m_fin.append(m)
        for c in range(nc):
            rows8 = [jnp.broadcast_to(v, (1, LANES)) for v in vals[c]] + [jnp.zeros((2, LANES), F32)]
            sc_s[b * nc + c] = jnp.concatenate(rows8, axis=0)

    uq =_dot_nt(hb, wq_ref[...].astype(BF16))
    uk = _dot_nt(hb, wk_ref[...].astype(BF16))
    q_s[...] = _silu(_short_conv3(uq, cwq_ref, cbq_ref[...], L)).astype(BF16)
    uv = _dot_nt(hb, wv_ref[...].astype(BF16))
    k = _silu(_short_conv3(uk, cwk_ref, cbk_ref[...], L)) * (dh ** -0.5)
    kt_s[...] = k.T.astype(BF16)
    uo = _dot_nt(hb, wo_ref[...].astype(BF16))
    v_s[:, 0:dh] = uv.astype(BF16)
    v_s[:, dh:dh + LANES] = jnp.ones((R, LANES), BF16)
    uz = _dot_nt(hb, wz_ref[...].astype(BF16))
    og_s[...] = _sigmoid(uo) * _silu(uz)

    for b in range(nb):
        for dr in range(2):
            if has_state:
                c_s[2 * b + dr, :, 0:dh] = c0_ref[b, dr]
                c_s[2 * b + dr, :, dh:dh + LANES] = jnp.broadcast_to(n0_ref[b, dr], (LANES, dh)).T
            else:
                c_s[2 * b + dr] = jnp.zeros((dh, dh + LANES), F32)

    def step(ci, carry):
        for b in range(nb):
            for dr in range(2):
                c = ci if dr == 0 else nc - 1 - ci
                r0 = pl.multiple_of(b * L + c * T, T)
                qc = q_s[pl.ds(r0, T), :]
                vc = v_s[pl.ds(r0, T), :]
                ktc = kt_s[:, pl.ds(r0, T)]
                b_rep = jnp.broadcast_to(g2_s[pl.ds(r0, T), :][:, 2 * dr + 1:2 * dr + 2], (T, LANES))
                g2tc = g2t_s[:, pl.ds(r0, T)]
                irow = g2tc[2 * dr:2 * dr + 1, :]
                brow = g2tc[2 * dr + 1:2 * dr + 2, :]
                sc = sc_s[b * nc + c]
                m_prev = sc[3 * dr:3 * dr + 1, :]
                m_new = sc[3 * dr + 1:3 * dr + 2, :]
                bl = sc[3 * dr + 2:3 * dr + 3, :]
                c_old = c_s[2 * b + dr]

                dlog = jnp.where(causal if dr == 0 else anti, b_rep - brow + irow, NEG)
                inter = b_rep + m_prev
                mt = jnp.maximum(inter, jnp.max(dlog, axis=-1, keepdims=True))
                smat = _dot(qc, ktc) * jnp.exp(dlog - mt)
                iw = jnp.exp(inter - mt)
                ext = (_dot(smat.astype(BF16), vc)
                       + jnp.concatenate([iw, iw, iw], axis=-1) * _dot(qc, c_old.astype(BF16)))
                inv = 1.0 / jnp.maximum(jnp.abs(ext[:, dh:dh + LANES]), jnp.exp(-mt))
                hh = ext[:, 0:dh] * jnp.concatenate([inv, inv], axis=-1)
                hdir_s[dr, pl.ds(r0, T), :] = hh

                ws = jnp.exp(bl - brow + irow - m_new)
                dec = jnp.exp(bl + m_prev - m_new)[:, 0:1]
                kw = (ktc.astype(F32) * ws).astype(BF16)
                c_s[2 * b + dr] = dec * c_old + _dot(kw, vc)
        return carry

    lax.fori_loop(0, nc, step, 0, unroll=max(1, min(nc, ML_STREAMS // (2 * nb))))
    if not has_state:
        for b in range(nb):
            for dr in range(2):
                cn_ref[b, dr] = c_s[2 * b + dr, :, 0:dh]
                nn_ref[b, dr] = c_s[2 * b + dr, :, dh:dh + LANES].T[0:1, :]
                mn_ref[b, dr] = jnp.broadcast_to(m_fin[2 * b + dr], (1, LANES))

    hs = hdir_s[0] + hdir_s[1]
    hm = hs * lax.rsqrt(jnp.mean(hs * hs, axis=-1, keepdims=True) + EPS) * ng_ref[...]
    yb_ref[...] = (hm * og_s[...]).astype(yb_ref.dtype).reshape(yb_ref.shape)


def _mlstm(h, l, depth, w_in, wg_heads, gb_heads, conv_w, conv_b, norm_g, states, carried,
           seg_off):
    bsz, L, d = h.shape
    heads = wg_heads.shape[1]
    d_ml = norm_g.shape[-1]
    dh = d_ml // heads
    has_state = states is not None
    nb = max(1, min(bsz, ML_ROWS // L))
    nc = L // ML_CHUNK
    qo, ko, vo, oo, zo = (s // dh for s in seg_off)

    def w_spec(off):
        return pl.BlockSpec((None, dh, d), lambda hd, b, off=off: (l, off + hd, 0))

    def vec_spec(rows, off):
        return pl.BlockSpec((None, rows, dh), lambda hd, b, off=off: (l, 0, off + hd))

    in_specs = [pl.BlockSpec((nb, L, d), lambda hd, b: (b, 0, 0)),
                w_spec(qo), w_spec(ko), w_spec(vo), w_spec(oo), w_spec(zo),
                pl.BlockSpec((None, None, LANES, d), lambda hd, b: (l, hd, 0, 0)),
                pl.BlockSpec((None, None, 1, LANES), lambda hd, b: (l, hd, 0, 0)),
                vec_spec(3, 0), vec_spec(3, heads), vec_spec(1, 0), vec_spec(1, heads),
                vec_spec(1, 0)]
    args = [h, w_in, w_in, w_in, w_in, w_in, wg_heads, gb_heads,
            conv_w, conv_w, conv_b, conv_b, norm_g]
    yb_spec = pl.BlockSpec((nb, L, dh), lambda hd, b: (b, 0, hd))
    yb_shape = jax.ShapeDtypeStruct((bsz, L, d_ml), BF16)
    state_idx = lambda hd, b: (b, l, 0, hd, 0, 0)
    c_spec = pl.BlockSpec((nb, None, 2, None, dh, dh), state_idx)
    n_spec = pl.BlockSpec((nb, None, 2, None, 1, dh), state_idx)
    m_spec = pl.BlockSpec((nb, None, 2, None, 1, LANES), state_idx)
    aliases = {}
    if has_state:
        in_specs += [c_spec, n_spec, m_spec]
        args += list(states)
        out_specs, out_shape = yb_spec, yb_shape
    else:
        if carried is not None:
            aliases = {len(args) + i: 1 + i for i in range(3)}
            in_specs += [pl.BlockSpec(memory_space=pl.ANY)] * 3
            args += list(carried)
        out_specs = (yb_spec, c_spec, n_spec, m_spec)
        out_shape = (yb_shape,
                     jax.ShapeDtypeStruct((bsz, depth, 2, heads, dh, dh), F32),
                     jax.ShapeDtypeStruct((bsz, depth, 2, heads, 1, dh), F32),
                     jax.ShapeDtypeStruct((bsz, depth, 2, heads, 1, LANES), F32))
    rows = nb * L
    scratch = [pltpu.VMEM((rows, dh), BF16),
               pltpu.VMEM((dh, rows), BF16),
               pltpu.VMEM((rows, dh + LANES), BF16),
               pltpu.VMEM((rows, LANES), F32),
               pltpu.VMEM((8, rows), F32),
               pltpu.VMEM((nb * nc, 8, LANES), F32),
               pltpu.VMEM((2, rows, dh), F32),
               pltpu.VMEM((rows, dh), F32),
               pltpu.VMEM((2 * nb, dh, dh + LANES), F32)]
    return pl.pallas_call(
        functools.partial(_mlstm_kernel, nb=nb, L=L, has_state=has_state, n_aliased=len(aliases)),
        grid=(heads, bsz // nb),
        in_specs=in_specs,
        out_specs=out_specs,
        out_shape=out_shape,
        input_output_aliases=aliases,
        scratch_shapes=scratch,
        compiler_params=_params("arbitrary", "arbitrary"),
        name="mlstm",
    )(*args)


OUT_WEIGHTS = 5
OUT_STAGES = 2


def _out_kernel(x_ref, h_ref, ya_ref, yb_ref, mod_ref, wt_hbm, wpa_hbm, wpb_hbm, wout_hbm,
                g_ref, *rest, last, l, off_ga):
    w_s, stage, sem = rest[-3:]
    rest = rest[:-3]
    d = w_s.shape[-1]
    srcs = (wt_hbm.at[l, pl.ds(off_ga, d), :], wt_hbm.at[l, pl.ds(off_ga + d, d), :],
            wpa_hbm.at[l], wpb_hbm.at[l], wout_hbm.at[l])

    def copy(i):
        slot = i % OUT_STAGES
        return pltpu.make_async_copy(srcs[i], stage.at[slot], sem.at[slot])

    def fetched(i):
        copy(i).wait()
        w_s[i] = stage[i % OUT_STAGES].astype(BF16)
        if i + OUT_STAGES < OUT_WEIGHTS:
            copy(i + OUT_STAGES).start()
        return w_s[i]

    def body(weight):
        hb = h_ref[...]
        ga = _sigmoid(_dot_nt(hb, weight(0)))
        gb = _sigmoid(_dot_nt(hb, weight(1)))
        merged = ga * _dot(ya_ref[...], weight(2)) + gb * _dot(yb_ref[...], weight(3))
        xn = x_ref[...] + mod_ref[2:3, :] * _dot(merged.astype(BF16), weight(4))
        normed = xn * lax.rsqrt(jnp.mean(xn * xn, axis=-1, keepdims=True) + EPS) * g_ref[...]
        if last:
            (y_ref,) = rest
            y_ref[...] = normed
        else:
            modn_ref, xo_ref, ho_ref = rest
            xo_ref[...] = xn
            ho_ref[...] = (normed * (1.0 + modn_ref[1:2, :]) + modn_ref[0:1, :]).astype(ho_ref.dtype)

    first = (pl.program_id(0) == 0) & (pl.program_id(1) == 0)

    @pl.when(first)
    def _():
        for i in range(OUT_STAGES):
            copy(i).start()
        body(fetched)

    @pl.when(jnp.logical_not(first))
    def _():
        body(lambda i: w_s[i])


def _out(x, h, ya, yb, mods, row0, l, w_t, off_ga, w_pa, w_pb, w_out, norm_g, final_g):
    bm, t, d = x.shape
    tm = min(t, OUT_TM)
    last = l == mods.shape[0] - 1
    tok = lambda b, i: (b, i, 0)
    tok_spec = pl.BlockSpec((None, tm, d), tok)
    hbm = pl.BlockSpec(memory_space=pl.ANY)
    g_spec = pl.BlockSpec((None, 1, d), lambda b, i: (0 if last else l + 1, 0, 0))
    in_specs = [tok_spec, tok_spec, tok_spec, tok_spec, _mod_spec(d, l, row0), hbm, hbm, hbm, hbm,
                g_spec]
    args = [x, h, ya, yb, mods, w_t, w_pa, w_pb, w_out, final_g.reshape(1, 1, d) if last else norm_g]
    if last:
        out_specs, out_shape = tok_spec, jax.ShapeDtypeStruct((bm, t, d), F32)
    else:
        in_specs.append(_mod_spec(d, l + 1, row0))
        args.append(mods)
        out_specs = (tok_spec, tok_spec)
        out_shape = (jax.ShapeDtypeStruct((bm, t, d), F32), jax.ShapeDtypeStruct((bm, t, d), BF16))
    return pl.pallas_call(
        functools.partial(_out_kernel, last=last, l=l, off_ga=off_ga),
        grid=(bm, t // tm),
        in_specs=in_specs,
        out_specs=out_specs,
        out_shape=out_shape,
        scratch_shapes=[pltpu.VMEM((OUT_WEIGHTS, d, d), BF16),
                        pltpu.VMEM((OUT_STAGES, d, d), F32),
                        pltpu.SemaphoreType.DMA((OUT_STAGES,))],
        compiler_params=_params("arbitrary", "arbitrary"),
        name="merge_out",
    )(*args)


def kernel(x_prompt, x_sample, state_C, state_n, state_m, c, c_ctx, norm_g, w_ada, b_ada, w_in, hy_conv_w, hy_conv_b, hy_w1, hy_b1, hy_w2, hy_b2, hy_w3, hy_b3, hy_freq, hy_decay, hy_bias, ml_conv_w, ml_conv_b, ml_if_b, ml_norm_g, w_pa, w_pb, w_out, final_g):
    depth, d, _ = w_in.shape
    d_hy = hy_bias.shape[-1]
    d_ml = ml_norm_g.shape[-1]
    heads = ml_if_b.shape[-1]
    bp, lp, _ = x_prompt.shape
    bs, ls, _ = x_sample.shape

    off_q = 4 * d_hy
    off_v = off_q + 2 * d_ml
    off_o = off_v + d_ml
    off_z = off_o + d_ml
    off_g = off_z + d_ml
    off_ga = off_g + 4 * heads
    seg_off = (off_q, off_q + d_ml, off_v, off_o, off_z)

    w_t = jnp.transpose(w_in, (0, 2, 1))
    wg = w_t[:, off_g:off_ga, :].reshape(depth, 4, heads, d).transpose(0, 2, 1, 3)
    wg_heads = jnp.zeros((depth, heads, LANES, d), BF16).at[:, :, :4, :].set(wg.astype(BF16))
    gb = ml_if_b.reshape(depth, 4, heads).transpose(0, 2, 1)
    gb_heads = jnp.zeros((depth, heads, 1, LANES), F32).at[:, :, 0, :4].set(gb)

    cvecs = jnp.concatenate([c_ctx[None], c, jnp.zeros((-(1 + bs) % SUBLANES, d), F32)], axis=0)
    mods = _mods(cvecs, w_ada, b_ada).reshape(depth, cvecs.shape[0], 3, d)

    norm_g3 = norm_g.reshape(depth, 1, d)
    hy_cw = hy_conv_w.reshape(depth, 3, 3, d_hy).transpose(0, 2, 1, 3)
    hy_cb = hy_conv_b.reshape(depth, 3, d_hy)
    ml_cb = ml_conv_b.reshape(depth, 1, 2 * d_ml)
    ml_ng = ml_norm_g.reshape(depth, 1, d_ml)
    filter_params = _filter_params(hy_w1, hy_b1, hy_w2, hy_b2, hy_w3, hy_b3, hy_freq, hy_decay)

    xp = x_prompt.reshape(1, bp * lp, d)
    xs = x_sample
    cached = (state_C, state_n.reshape(bs, depth, 2, heads, 1, -1),
              jnp.broadcast_to(state_m[..., None, None], (bs, depth, 2, heads, 1, LANES)))
    fin = None
    hp = _norm_mod(xp, norm_g3, mods, 0, 0)
    hs = _norm_mod(xs, norm_g3, mods, 0, 1)
    dft = {L: _dft_operands(L) for L in sorted({lp, ls})}
    for l in range(depth):
        spectra = {L: _hyena_spectra(dft[L], l, filter_params) for L in sorted({lp, ls})}
        last = l == depth - 1

        def layer(x, h, row0, bsz, L, states, carried):
            hseq = h.reshape(bsz, L, d)
            ya = _hyena(hseq, l, w_t, hy_cw, hy_cb, spectra[L], hy_bias, dft[L])
            res = _mlstm(hseq, l, depth, w_t, wg_heads, gb_heads, ml_conv_w, ml_cb, ml_ng,
                         states, carried, seg_off)
            yb, fin = (res, None) if states is not None else (res[0], res[1:])
            res = _out(x, h, ya.reshape(x.shape), yb.reshape(x.shape), mods, row0, l,
                       w_t, off_ga, w_pa, w_pb, w_out, norm_g3, final_g)
            return (res, None, fin) if last else (res[0], res[1], fin)

        xp, hp, fin = layer(xp, hp, 0, bp, lp, None, fin)
        xs, hs, _ = layer(xs, hs, 1, bs, ls, cached, None)

    return (xp.reshape(bp, lp, d), xs, fin[0], fin[1][:, :, :, :, 0, :], fin[2][:, :, :, :, 0, 0])
```

```python
import functools
import math

import numpy as np
import jax
import jax.numpy as jnp
from jax import lax
from jax.experimental import pallas as pl
from jax.experimental.pallas import tpu as pltpu

F32 = jnp.float32
BF16 = jnp.bfloat16
HIGHEST = lax.Precision.HIGHEST

HY_ORDER = 2
HY_BANDS = 16
HY_SHIFT = 0.05
ML_CHUNK = 128
EPS = 1e-6
NEG = -1e30

LANES = 128
SUBLANES = 8
VMEM_LIMIT = 58 * 1024 * 1024
HY_CB = 256
ML_ROWS = 2048
HY_ROWS = 2048
ML_STREAMS = 8
OUT_TM = 512


def _dot(a, b):
    return jnp.dot(a, b, preferred_element_type=F32)


def _dot_nt(a, bt):
    return lax.dot_general(a, bt, (((1,), (1,)), ((), ())), preferred_element_type=F32)


def _dot_hi(a, b):
    return jnp.dot(a, b, preferred_element_type=F32, precision=HIGHEST)


def _sigmoid(x):
    return 1.0 / (1.0 + jnp.exp(-x))


def _silu(x):
    return x * _sigmoid(x)


def _log_sigmoid(x):
    return jnp.minimum(x, 0.0) - jnp.log(1.0 + jnp.exp(-jnp.abs(x)))


def _params(*sem):
    return pltpu.CompilerParams(dimension_semantics=sem, vmem_limit_bytes=VMEM_LIMIT)


@functools.lru_cache(maxsize=None)
def _dft_table(L):
    N = 2 * L
    k = np.arange(L, dtype=np.int64)[:, None]
    n = np.arange(L, dtype=np.int64)[None, :]
    ang = 2.0 * np.pi * ((k * n) % N).astype(np.float64) / N
    c = np.cos(ang)
    s = -np.sin(ang)
    s[0, :] = 1.0 - 2.0 * (np.arange(L) % 2)
    return np.concatenate([c, s], axis=0).astype(np.float32)


def _dft_operands(L):
    f = jnp.asarray(_dft_table(L)).astype(BF16)
    return f, f.T


def _split_bf16(x):
    hi = x.astype(BF16)
    return hi, (x - hi.astype(F32)).astype(BF16)


def _mod_kernel(c_ref, w_ref, b_ref, o_ref):
    a_hi, a_lo = _split_bf16(_silu(c_ref[...]))
    w_hi, w_lo = _split_bf16(w_ref[...])
    o_ref[...] = (_dot(a_hi, w_hi) + (_dot(a_lo, w_hi) + _dot(a_hi, w_lo))) + b_ref[...]


def _mods(cvecs, w_ada, b_ada):
    depth, d, d3 = w_ada.shape
    r = cvecs.shape[0]
    tn = 1024
    return pl.pallas_call(
        _mod_kernel,
        grid=(depth, d3 // tn),
        in_specs=[pl.BlockSpec((r, d), lambda l, j: (0, 0)),
                  pl.BlockSpec((None, d, tn), lambda l, j: (l, 0, j)),
                  pl.BlockSpec((None, 1, tn), lambda l, j: (l, 0, j))],
        out_specs=pl.BlockSpec((None, r, tn), lambda l, j: (l, 0, j)),
        out_shape=jax.ShapeDtypeStruct((depth, r, d3), F32),
        compiler_params=_params("arbitrary", "arbitrary"),
        name="adaln_mod",
    )(cvecs, w_ada, b_ada.reshape(depth, 1, d3))


def _norm_mod_kernel(x_ref, g_ref, mod_ref, o_ref):
    x = x_ref[...]
    y = x * lax.rsqrt(jnp.mean(x * x, axis=-1, keepdims=True) + EPS) * g_ref[...]
    o_ref[...] = (y * (1.0 + mod_ref[1:2, :]) + mod_ref[0:1, :]).astype(o_ref.dtype)


def _mod_spec(d, l, row0):
    return pl.BlockSpec((None, None, 3, d), lambda b, i: (l, row0 + b, 0, 0))


def _norm_mod(x, norm_g, mods, l, row0):
    bm, t, d = x.shape
    tm = min(t, 2048)
    return pl.pallas_call(
        _norm_mod_kernel,
        grid=(bm, t // tm),
        in_specs=[pl.BlockSpec((None, tm, d), lambda b, i: (b, i, 0)),
                  pl.BlockSpec((None, 1, d), lambda b, i: (l, 0, 0)),
                  _mod_spec(d, l, row0)],
        out_specs=pl.BlockSpec((None, tm, d), lambda b, i: (b, i, 0)),
        out_shape=jax.ShapeDtypeStruct((bm, t, d), BF16),
        compiler_params=_params("arbitrary", "arbitrary"),
        name="norm_mod",
    )(x, norm_g, mods)


def _filter_kernel(w1_ref, b1_ref, w2_ref, b2_ref, freq_ref, w3_00, w3_01, w3_10, w3_11,
                   b3_ref, decay_ref, f_ref, spec_ref, hdn_ref, *, L):
    cb = spec_ref.shape[-1]

    @pl.when(pl.program_id(0) == 0)
    def _():
        fr = -(-(1 + 2 * HY_BANDS) // SUBLANES) * SUBLANES
        r = lax.broadcasted_iota(jnp.int32, (fr, L), 0)
        t = lax.broadcasted_iota(jnp.int32, (fr, L), 1).astype(F32) / L
        band = jnp.where(r <= HY_BANDS, r, r - HY_BANDS).astype(F32)
        ang = 2.0 * math.pi * t * band
        feats = jnp.where(r == 0, t,
                          jnp.where(r <= HY_BANDS, jnp.cos(ang),
                                    jnp.where(r <= 2 * HY_BANDS, jnp.sin(ang), 0.0)))
        feats = jnp.concatenate([feats, jnp.zeros((LANES - fr, L), F32)], axis=0)
        hdn = jnp.sin(freq_ref[:, 0:1] * (_dot_hi(w1_ref[...], feats) + b1_ref[...]))
        hdn = jnp.sin(freq_ref[:, 1:2] * (_dot_hi(w2_ref[...], hdn) + b2_ref[...]))
        hdn_ref[...] = hdn.T

    hdn = hdn_ref[...].astype(BF16)
    row_i = lax.broadcasted_iota(jnp.int32, (L, cb), 0)
    t = row_i.astype(F32) / L
    first = row_i == 0
    sign = jnp.where((row_i & 1) == 0, 1.0, -1.0)
    w3 = ((w3_00, w3_01), (w3_10, w3_11))
    for o in range(HY_ORDER):
        hs = []
        for dr in range(2):
            j = 2 * o + dr
            hv = _dot(hdn, w3[o][dr][...].astype(BF16)) + b3_ref[j:j + 1, :]
            win = jnp.exp(-t * jnp.abs(decay_ref[j:j + 1, :])) + HY_SHIFT
            hs.append(hv * win)
        hf = hs[0]
        hb = jnp.where(first, 0.0, hs[1])
        l1 = jnp.sum(jnp.abs(hf), axis=0, keepdims=True) + jnp.sum(jnp.abs(hb), axis=0, keepdims=True)
        inv_l1 = 1.0 / l1
        even = (hf + hb) * inv_l1
        odd = (hf - hb) * inv_l1
        re = _dot(f_ref[0:L, :], even.astype(BF16))
        im = _dot(f_ref[L:2 * L, :], odd.astype(BF16))
        nyq = jnp.sum(even * sign, axis=0, keepdims=True)
        a = re * jnp.where(first, 0.5 / L, 1.0 / L)
        spec_ref[o, 0] = a.astype(spec_ref.dtype)
        spec_ref[o, 1] = jnp.where(first, 0.0, im * (1.0 / L)).astype(spec_ref.dtype)
        spec_ref[o, 2] = jnp.where(first, nyq * (0.5 / L), a).astype(spec_ref.dtype)


def _filter_params(hy_w1, hy_b1, hy_w2, hy_b2, hy_w3, hy_b3, hy_freq, hy_decay):
    depth, emb, ff = hy_w1.shape
    d_hy = hy_decay.shape[-1]
    w1t = jnp.zeros((depth, ff, LANES), F32).at[:, :, :emb].set(hy_w1.transpose(0, 2, 1))
    return (w1t, hy_b1.reshape(depth, ff, 1), hy_w2.transpose(0, 2, 1), hy_b2.reshape(depth, ff, 1),
            hy_freq.transpose(0, 2, 1), hy_w3, hy_b3.reshape(depth, 2 * HY_ORDER, d_hy),
            hy_decay.reshape(depth, 2 * HY_ORDER, d_hy))


def _hyena_spectra(dft, l, filter_params):
    w1t, b1, w2t, b2, freq, w3, b3, decay = filter_params
    ff = w1t.shape[1]
    d_hy = decay.shape[-1]
    cb = HY_CB
    ncb = d_hy // cb
    f, _ = dft
    L = f.shape[1]
    layer = lambda j: (l, 0, 0)

    def w3_spec(o, dr):
        return pl.BlockSpec((None, ff, cb), lambda j, o=o, dr=dr: (l, 0, (2 * o + dr) * ncb + j))

    return pl.pallas_call(
        functools.partial(_filter_kernel, L=L),
        grid=(ncb,),
        in_specs=[pl.BlockSpec((None, ff, LANES), layer),
                  pl.BlockSpec((None, ff, 1), layer),
                  pl.BlockSpec((None, ff, ff), layer),
                  pl.BlockSpec((None, ff, 1), layer),
                  pl.BlockSpec((None, ff, 2), layer),
                  w3_spec(0, 0), w3_spec(0, 1), w3_spec(1, 0), w3_spec(1, 1),
                  pl.BlockSpec((None, 2 * HY_ORDER, cb), lambda j: (l, 0, j)),
                  pl.BlockSpec((None, 2 * HY_ORDER, cb), lambda j: (l, 0, j)),
                  pl.BlockSpec((2 * L, L), lambda j: (0, 0))],
        out_specs=pl.BlockSpec((HY_ORDER, 3, L, cb), lambda j: (0, 0, 0, j)),
        out_shape=jax.ShapeDtypeStruct((HY_ORDER, 3, L, d_hy), BF16),
        scratch_shapes=[pltpu.VMEM((L, ff), F32)],
        compiler_params=_params("arbitrary"),
        name="hyena_filter",
    )(w1t, b1, w2t, b2, freq, w3, w3, w3, w3, b3, decay, f)


def _short_conv3(x, w_ref, b, seq=None):
    n, c = x.shape
    seq = n if seq is None else seq
    prev = pltpu.roll(x, 1, axis=0)
    nxt = pltpu.roll(x, n - 1, axis=0)
    row = lax.broadcasted_iota(jnp.int32, (SUBLANES, c), 0)
    prev_parts, nxt_parts = [], []
    for s in range(0, n, seq):
        e = s + seq
        prev_parts += [jnp.where(row == 0, 0.0, prev[s:s + SUBLANES]), prev[s + SUBLANES:e]]
        nxt_parts += [nxt[s:e - SUBLANES], jnp.where(row == SUBLANES - 1, 0.0, nxt[e - SUBLANES:e])]
    prev = jnp.concatenate(prev_parts, axis=0)
    nxt = jnp.concatenate(nxt_parts, axis=0)
    return ((b + prev * w_ref[0:1, :]) + x * w_ref[1:2, :]) + nxt * w_ref[2:3, :]


def _hyena_kernel(h_ref, wx1_ref, wx2_ref, wv_ref, wz_ref, cw_ref, cb_ref, spec_ref, hbias_ref,
                  f_ref, g_ref, o_ref, *, nb, L):
    seqs = range(nb)
    wv = wv_ref[...].astype(BF16)
    z = [_short_conv3(_dot_nt(h_ref[b], wv), cw_ref.at[2], cb_ref[2:3, :]) for b in seqs]
    gate_w = (wx1_ref, wx2_ref)
    for o in range(HY_ORDER):
        zf = [_dot(f_ref[...], z[b].astype(BF16)) for b in seqs]
        wg = gate_w[o][...].astype(BF16)
        gate = [_short_conv3(_dot_nt(h_ref[b], wg), cw_ref.at[o], cb_ref[o:o + 1, :]) for b in seqs]
        a, bm, dm = (spec_ref[o, i].astype(F32) for i in range(3))
        ycat = []
        for b in seqs:
            zc, zs = zf[b][0:L], zf[b][L:2 * L]
            ycat.append(jnp.concatenate([zc * a - zs * bm, zc * bm + zs * dm], axis=0).astype(BF16))
        y = [_dot(g_ref[...], ycat[b]) for b in seqs]
        z = [gate[b] * (y[b] + hbias_ref[o:o + 1, :] * z[b]) for b in seqs]
    wz = wz_ref[...].astype(BF16)
    for b in seqs:
        o_ref[b] = (z[b] * _silu(_dot_nt(h_ref[b], wz))).astype(o_ref.dtype)


def _hyena(h, l, w_t, conv_w, conv_b, spec, hy_bias, dft):
    bsz, L, d = h.shape
    d_hy = hy_bias.shape[-1]
    cb = HY_CB
    ncb = d_hy // cb
    nb = max(1, min(bsz, HY_ROWS // L))
    f, g = dft
    once = pl.Buffered(1)

    def w_spec(seg):
        return pl.BlockSpec((None, cb, d), lambda j, i, seg=seg: (l, seg * ncb + j, 0))

    return pl.pallas_call(
        functools.partial(_hyena_kernel, nb=nb, L=L),
        grid=(ncb, bsz // nb),
        in_specs=[pl.BlockSpec((nb, L, d), lambda j, i: (i, 0, 0)),
                  w_spec(0), w_spec(1), w_spec(2), w_spec(3),
                  pl.BlockSpec((None, 3, 3, cb), lambda j, i: (l, 0, 0, j)),
                  pl.BlockSpec((None, 3, cb), lambda j, i: (l, 0, j)),
                  pl.BlockSpec((HY_ORDER, 3, L, cb), lambda j, i: (0, 0, 0, j)),
                  pl.BlockSpec((None, HY_ORDER, cb), lambda j, i: (l, 0, j)),
                  pl.BlockSpec((2 * L, L), lambda j, i: (0, 0), pipeline_mode=once),
                  pl.BlockSpec((L, 2 * L), lambda j, i: (0, 0), pipeline_mode=once)],
        out_specs=pl.BlockSpec((nb, L, cb), lambda j, i: (i, 0, j)),
        out_shape=jax.ShapeDtypeStruct((bsz, L, d_hy), BF16),
        compiler_params=_params("arbitrary", "arbitrary"),
        name="hyena",
    )(h, w_t, w_t, w_t, w_t, conv_w, conv_b, spec, hy_bias, f, g)


def _mlstm_kernel(*refs, nb, L, has_state, n_aliased):
    (h_ref, wq_ref, wk_ref, wv_ref, wo_ref, wz_ref, wg_ref, gb_ref, cwq_ref, cwk_ref,
     cbq_ref, cbk_ref, ng_ref) = refs[:13]
    refs = refs[13:]
    if has_state:
        c0_ref, n0_ref, m0_ref, yb_ref = refs[:4]
        refs = refs[4:]
    else:
        yb_ref, cn_ref, nn_ref, mn_ref = refs[n_aliased:n_aliased + 4]
        refs = refs[n_aliased + 4:]
    q_s, kt_s, v_s, g2_s, g2t_s, sc_s, hdir_s, og_s, c_s = refs

    T = ML_CHUNK
    nc = L // T
    R = nb * L
    d = h_ref.shape[-1]
    dh = q_s.shape[-1]
    hb = h_ref[...].reshape(R, d)

    g = _dot_nt(hb, wg_ref[...]) + gb_ref[...]
    lf = _log_sigmoid(g)
    ti = lax.broadcasted_iota(jnp.int32, (T, T), 0)
    si = lax.broadcasted_iota(jnp.int32, (T, T), 1)
    causal = si <= ti
    anti = si >= ti
    lower = causal.astype(BF16)
    lf_hi = lf.astype(BF16)
    lf_r = lf - lf_hi.astype(F32)
    lf_mid = lf_r.astype(BF16)
    lf_lo = (lf_r - lf_mid.astype(F32)).astype(BF16)
    lane = lax.broadcasted_iota(jnp.int32, (T, LANES), 1)
    for j in range(nb * nc):
        rows = slice(j * T, (j + 1) * T)
        pre = _dot(lower, lf_hi[rows]) + (_dot(lower, lf_mid[rows]) + _dot(lower, lf_lo[rows]))
        suf = pre[T - 1:T, :] - pre + lf[rows]
        g2_s[rows, :] = jnp.where(lane == 1, pre, jnp.where(lane == 3, suf, g[rows]))
    g2t_s[...] = g2_s[...].T[0:8, :]

    zero11 = jnp.zeros((1, 1), F32)
    m_fin = []
    for b in range(nb):
        vals = [[None] * 6 for _ in range(nc)]
        for dr in range(2):
            m = m0_ref[b, dr][:, 0:1] if has_state else zero11
            for ci in range(nc):
                c = ci if dr == 0 else nc - 1 - ci
                cols = slice(b * L + c * T, b * L + (c + 1) * T)
                irow = g2t_s[2 * dr:2 * dr + 1, cols]
                brow = g2t_s[2 * dr + 1:2 * dr + 2, cols]
                bl = brow[:, T - 1:T] if dr == 0 else brow[:, 0:1]
                mw = jnp.max(bl - brow + irow, axis=-1, keepdims=True)
                mn = jnp.maximum(bl + m, mw)
                vals[c][3 * dr:3 * dr + 3] = [m, mn, bl]
                m = mn
            m_fin.append(m)
        for c in range(nc):
            rows8 = [jnp.broadcast_to(v, (1, LANES)) for v in vals[c]] + [jnp.zeros((2, LANES), F32)]
            sc_s[b * nc + c] = jnp.concatenate(rows8, axis=0)

    uq =_dot_nt(hb, wq_ref[...].astype(BF16))
    uk = _dot_nt(hb, wk_ref[...].astype(BF16))
    q_s[...] = _silu(_short_conv3(uq, cwq_ref, cbq_ref[...], L)).astype(BF16)
    uv = _dot_nt(hb, wv_ref[...].astype(BF16))
    k = _silu(_short_conv3(uk, cwk_ref, cbk_ref[...], L)) * (dh ** -0.5)
    kt_s[...] = k.T.astype(BF16)
    uo = _dot_nt(hb, wo_ref[...].astype(BF16))
    v_s[:, 0:dh] = uv.astype(BF16)
    v_s[:, dh:dh + LANES] = jnp.ones((R, LANES), BF16)
    uz = _dot_nt(hb, wz_ref[...].astype(BF16))
    og_s[...] = _sigmoid(uo) * _silu(uz)

    for b in range(nb):
        for dr in range(2):
            if has_state:
                c_s[2 * b + dr, :, 0:dh] = c0_ref[b, dr]
                c_s[2 * b + dr, :, dh:dh + LANES] = jnp.broadcast_to(n0_ref[b, dr], (LANES, dh)).T
            else:
                c_s[2 * b + dr] = jnp.zeros((dh, dh + LANES), F32)

    def step(ci, carry):
        for b in range(nb):
            for dr in range(2):
                c = ci if dr == 0 else nc - 1 - ci
                r0 = pl.multiple_of(b * L + c * T, T)
                qc = q_s[pl.ds(r0, T), :]
                vc = v_s[pl.ds(r0, T), :]
                ktc = kt_s[:, pl.ds(r0, T)]
                b_rep = jnp.broadcast_to(g2_s[pl.ds(r0, T), :][:, 2 * dr + 1:2 * dr + 2], (T, LANES))
                g2tc = g2t_s[:, pl.ds(r0, T)]
                irow = g2tc[2 * dr:2 * dr + 1, :]
                brow = g2tc[2 * dr + 1:2 * dr + 2, :]
                sc = sc_s[b * nc + c]
                m_prev = sc[3 * dr:3 * dr + 1, :]
                m_new = sc[3 * dr + 1:3 * dr + 2, :]
                bl = sc[3 * dr + 2:3 * dr + 3, :]
                c_old = c_s[2 * b + dr]

                dlog = jnp.where(causal if dr == 0 else anti, b_rep - brow + irow, NEG)
                inter = b_rep + m_prev
                mt = jnp.maximum(inter, jnp.max(dlog, axis=-1, keepdims=True))
                smat = _dot(qc, ktc) * jnp.exp(dlog - mt)
                iw = jnp.exp(inter - mt)
                ext = (_dot(smat.astype(BF16), vc)
                       + jnp.concatenate([iw, iw, iw], axis=-1) * _dot(qc, c_old.astype(BF16)))
                inv = 1.0 / jnp.maximum(jnp.abs(ext[:, dh:dh + LANES]), jnp.exp(-mt))
                hh = ext[:, 0:dh] * jnp.concatenate([inv, inv], axis=-1)
                hdir_s[dr, pl.ds(r0, T), :] = hh

                ws = jnp.exp(bl - brow + irow - m_new)
                dec = jnp.exp(bl + m_prev - m_new)[:, 0:1]
                kw = (ktc.astype(F32) * ws).astype(BF16)
                c_s[2 * b + dr] = dec * c_old + _dot(kw, vc)
        return carry

    lax.fori_loop(0, nc, step, 0, unroll=max(1, min(nc, ML_STREAMS // (2 * nb))))
    if not has_state:
        for b in range(nb):
            for dr in range(2):
                cn_ref[b, dr] = c_s[2 * b + dr, :, 0:dh]
                nn_ref[b, dr] = c_s[2 * b + dr, :, dh:dh + LANES].T[0:1, :]
                mn_ref[b, dr] = jnp.broadcast_to(m_fin[2 * b + dr], (1, LANES))

    hs = hdir_s[0] + hdir_s[1]
    hm = hs * lax.rsqrt(jnp.mean(hs * hs, axis=-1, keepdims=True) + EPS) * ng_ref[...]
    yb_ref[...] = (hm * og_s[...]).astype(yb_ref.dtype).reshape(yb_ref.shape)


def _mlstm(h, l, depth, w_in, wg_heads, gb_heads, conv_w, conv_b, norm_g, states, carried,
           seg_off):
    bsz, L, d = h.shape
    heads = wg_heads.shape[1]
    d_ml = norm_g.shape[-1]
    dh = d_ml // heads
    has_state = states is not None
    nb = max(1, min(bsz, ML_ROWS // L))
    nc = L // ML_CHUNK
    qo, ko, vo, oo, zo = (s // dh for s in seg_off)

    def w_spec(off):
        return pl.BlockSpec((None, dh, d), lambda hd, b, off=off: (l, off + hd, 0))

    def vec_spec(rows, off):
        return pl.BlockSpec((None, rows, dh), lambda hd, b, off=off: (l, 0, off + hd))

    in_specs = [pl.BlockSpec((nb, L, d), lambda hd, b: (b, 0, 0)),
                w_spec(qo), w_spec(ko), w_spec(vo), w_spec(oo), w_spec(zo),
                pl.BlockSpec((None, None, LANES, d), lambda hd, b: (l, hd, 0, 0)),
                pl.BlockSpec((None, None, 1, LANES), lambda hd, b: (l, hd, 0, 0)),
                vec_spec(3, 0), vec_spec(3, heads), vec_spec(1, 0), vec_spec(1, heads),
                vec_spec(1, 0)]
    args = [h, w_in, w_in, w_in, w_in, w_in, wg_heads, gb_heads,
            conv_w, conv_w, conv_b, conv_b, norm_g]
    yb_spec = pl.BlockSpec((nb, L, dh), lambda hd, b: (b, 0, hd))
    yb_shape = jax.ShapeDtypeStruct((bsz, L, d_ml), BF16)
    state_idx = lambda hd, b: (b, l, 0, hd, 0, 0)
    c_spec = pl.BlockSpec((nb, None, 2, None, dh, dh), state_idx)
    n_spec = pl.BlockSpec((nb, None, 2, None, 1, dh), state_idx)
    m_spec = pl.BlockSpec((nb, None, 2, None, 1, LANES), state_idx)
    aliases = {}
    if has_state:
        in_specs += [c_spec, n_spec, m_spec]
        args += list(states)
        out_specs, out_shape = yb_spec, yb_shape
    else:
        if carried is not None:
            aliases = {len(args) + i: 1 + i for i in range(3)}
            in_specs += [pl.BlockSpec(memory_space=pl.ANY)] * 3
            args += list(carried)
        out_specs = (yb_spec, c_spec, n_spec, m_spec)
        out_shape = (yb_shape,
                     jax.ShapeDtypeStruct((bsz, depth, 2, heads, dh, dh), F32),
                     jax.ShapeDtypeStruct((bsz, depth, 2, heads, 1, dh), F32),
                     jax.ShapeDtypeStruct((bsz, depth, 2, heads, 1, LANES), F32))
    rows = nb * L
    scratch = [pltpu.VMEM((rows, dh), BF16),
               pltpu.VMEM((dh, rows), BF16),
               pltpu.VMEM((rows, dh + LANES), BF16),
               pltpu.VMEM((rows, LANES), F32),
               pltpu.VMEM((8, rows), F32),
               pltpu.VMEM((nb * nc, 8, LANES), F32),
               pltpu.VMEM((2, rows, dh), F32),
               pltpu.VMEM((rows, dh), F32),
               pltpu.VMEM((2 * nb, dh, dh + LANES), F32)]
    return pl.pallas_call(
        functools.partial(_mlstm_kernel, nb=nb, L=L, has_state=has_state, n_aliased=len(aliases)),
        grid=(heads, bsz // nb),
        in_specs=in_specs,
        out_specs=out_specs,
        out_shape=out_shape,
        input_output_aliases=aliases,
        scratch_shapes=scratch,
        compiler_params=_params("arbitrary", "arbitrary"),
        name="mlstm",
    )(*args)


OUT_WEIGHTS = 5
OUT_STAGES = 2


def _out_kernel(x_ref, h_ref, ya_ref, yb_ref, mod_ref, wt_hbm, wpa_hbm, wpb_hbm, wout_hbm,
                g_ref, *rest, last, l, off_ga):
    w_s, stage, sem = rest[-3:]
    rest = rest[:-3]
    d = w_s.shape[-1]
    srcs = (wt_hbm.at[l, pl.ds(off_ga, d), :], wt_hbm.at[l, pl.ds(off_ga + d, d), :],
            wpa_hbm.at[l], wpb_hbm.at[l], wout_hbm.at[l])

    def copy(i):
        slot = i % OUT_STAGES
        return pltpu.make_async_copy(srcs[i], stage.at[slot], sem.at[slot])

    def fetched(i):
        copy(i).wait()
        w_s[i] = stage[i % OUT_STAGES].astype(BF16)
        if i + OUT_STAGES < OUT_WEIGHTS:
            copy(i + OUT_STAGES).start()
        return w_s[i]

    def body(weight):
        hb = h_ref[...]
        ga = _sigmoid(_dot_nt(hb, weight(0)))
        gb = _sigmoid(_dot_nt(hb, weight(1)))
        merged = ga * _dot(ya_ref[...], weight(2)) + gb * _dot(yb_ref[...], weight(3))
        xn = x_ref[...] + mod_ref[2:3, :] * _dot(merged.astype(BF16), weight(4))
        normed = xn * lax.rsqrt(jnp.mean(xn * xn, axis=-1, keepdims=True) + EPS) * g_ref[...]
        if last:
            (y_ref,) = rest
            y_ref[...] = normed
        else:
            modn_ref, xo_ref, ho_ref = rest
            xo_ref[...] = xn
            ho_ref[...] = (normed * (1.0 + modn_ref[1:2, :]) + modn_ref[0:1, :]).astype(ho_ref.dtype)

    first = (pl.program_id(0) == 0) & (pl.program_id(1) == 0)

    @pl.when(first)
    def _():
        for i in range(OUT_STAGES):
            copy(i).start()
        body(fetched)

    @pl.when(jnp.logical_not(first))
    def _():
        body(lambda i: w_s[i])


def _out(x, h, ya, yb, mods, row0, l, w_t, off_ga, w_pa, w_pb, w_out, norm_g, final_g):
    bm, t, d = x.shape
    tm = min(t, OUT_TM)
    last = l == mods.shape[0] - 1
    tok = lambda b, i: (b, i, 0)
    tok_spec = pl.BlockSpec((None, tm, d), tok)
    hbm = pl.BlockSpec(memory_space=pl.ANY)
    g_spec = pl.BlockSpec((None, 1, d), lambda b, i: (0 if last else l + 1, 0, 0))
    in_specs = [tok_spec, tok_spec, tok_spec, tok_spec, _mod_spec(d, l, row0), hbm, hbm, hbm, hbm,
                g_spec]
    args = [x, h, ya, yb, mods, w_t, w_pa, w_pb, w_out, final_g.reshape(1, 1, d) if last else norm_g]
    if last:
        out_specs, out_shape = tok_spec, jax.ShapeDtypeStruct((bm, t, d), F32)
    else:
        in_specs.append(_mod_spec(d, l + 1, row0))
        args.append(mods)
        out_specs = (tok_spec, tok_spec)
        out_shape = (jax.ShapeDtypeStruct((bm, t, d), F32), jax.ShapeDtypeStruct((bm, t, d), BF16))
    return pl.pallas_call(
        functools.partial(_out_kernel, last=last, l=l, off_ga=off_ga),
        grid=(bm, t // tm),
        in_specs=in_specs,
        out_specs=out_specs,
        out_shape=out_shape,
        scratch_shapes=[pltpu.VMEM((OUT_WEIGHTS, d, d), BF16),
                        pltpu.VMEM((OUT_STAGES, d, d), F32),
                        pltpu.SemaphoreType.DMA((OUT_STAGES,))],
        compiler_params=_params("arbitrary", "arbitrary"),
        name="merge_out",
    )(*args)


def kernel(x_prompt, x_sample, state_C, state_n, state_m, c, c_ctx, norm_g, w_ada, b_ada, w_in, hy_conv_w, hy_conv_b, hy_w1, hy_b1, hy_w2, hy_b2, hy_w3, hy_b3, hy_freq, hy_decay, hy_bias, ml_conv_w, ml_conv_b, ml_if_b, ml_norm_g, w_pa, w_pb, w_out, final_g):
    depth, d, _ = w_in.shape
    d_hy = hy_bias.shape[-1]
    d_ml = ml_norm_g.shape[-1]
    heads = ml_if_b.shape[-1]
    bp, lp, _ = x_prompt.shape
    bs, ls, _ = x_sample.shape

    off_q = 4 * d_hy
    off_v = off_q + 2 * d_ml
    off_o = off_v + d_ml
    off_z = off_o + d_ml
    off_g = off_z + d_ml
    off_ga = off_g + 4 * heads
    seg_off = (off_q, off_q + d_ml, off_v, off_o, off_z)

    w_t = jnp.transpose(w_in, (0, 2, 1))
    wg = w_t[:, off_g:off_ga, :].reshape(depth, 4, heads, d).transpose(0, 2, 1, 3)
    wg_heads = jnp.zeros((depth, heads, LANES, d), BF16).at[:, :, :4, :].set(wg.astype(BF16))
    gb = ml_if_b.reshape(depth, 4, heads).transpose(0, 2, 1)
    gb_heads = jnp.zeros((depth, heads, 1, LANES), F32).at[:, :, 0, :4].set(gb)

    cvecs = jnp.concatenate([c_ctx[None], c, jnp.zeros((-(1 + bs) % SUBLANES, d), F32)], axis=0)
    mods = _mods(cvecs, w_ada, b_ada).reshape(depth, cvecs.shape[0], 3, d)

    norm_g3 = norm_g.reshape(depth, 1, d)
    hy_cw = hy_conv_w.reshape(depth, 3, 3, d_hy).transpose(0, 2, 1, 3)
    hy_cb = hy_conv_b.reshape(depth, 3, d_hy)
    ml_cb = ml_conv_b.reshape(depth, 1, 2 * d_ml)
    ml_ng = ml_norm_g.reshape(depth, 1, d_ml)
    filter_params = _filter_params(hy_w1, hy_b1, hy_w2, hy_b2, hy_w3, hy_b3, hy_freq, hy_decay)

    xp = x_prompt.reshape(1, bp * lp, d)
    xs = x_sample
    cached = (state_C, state_n.reshape(bs, depth, 2, heads, 1, -1),
              jnp.broadcast_to(state_m[..., None, None], (bs, depth, 2, heads, 1, LANES)))
    fin = None
    hp = _norm_mod(xp, norm_g3, mods, 0, 0)
    hs = _norm_mod(xs, norm_g3, mods, 0, 1)
    dft = {L: _dft_operands(L) for L in sorted({lp, ls})}
    for l in range(depth):
        spectra = {L: _hyena_spectra(dft[L], l, filter_params) for L in sorted({lp, ls})}
        last = l == depth - 1

        def layer(x, h, row0, bsz, L, states, carried):
            hseq = h.reshape(bsz, L, d)
            ya = _hyena(hseq, l, w_t, hy_cw, hy_cb, spectra[L], hy_bias, dft[L])
            res = _mlstm(hseq, l, depth, w_t, wg_heads, gb_heads, ml_conv_w, ml_cb, ml_ng,
                         states, carried, seg_off)
            yb, fin = (res, None) if states is not None else (res[0], res[1:])
            res = _out(x, h, ya.reshape(x.shape), yb.reshape(x.shape), mods, row0, l,
                       w_t, off_ga, w_pa, w_pb, w_out, norm_g3, final_g)
            return (res, None, fin) if last else (res[0], res[1], fin)

        xp, hp, fin = layer(xp, hp, 0, bp, lp, None, fin)
        xs, hs, _ = layer(xs, hs, 1, bs, ls, cached, None)

    return (xp.reshape(bp, lp, d), xs, fin[0], fin[1][:, :, :, :, 0, :], fin[2][:, :, :, :, 0, 0])
```

```python
import functools
import math

import numpy as np
import jax
import jax.numpy as jnp
from jax import lax
from jax.experimental import pallas as pl
from jax.experimental.pallas import tpu as pltpu

F32 = jnp.float32
BF16 = jnp.bfloat16
HIGHEST = lax.Precision.HIGHEST

HY_ORDER = 2
HY_BANDS = 16
HY_SHIFT = 0.05
ML_CHUNK = 128
EPS = 1e-6
NEG = -1e30

LANES = 128
SUBLANES = 8
VMEM_LIMIT = 58 * 1024 * 1024
HY_CB = 256
ML_ROWS = 2048
HY_ROWS = 2048
ML_STREAMS = 8
OUT_TM = 512


def _dot(a, b):
    return jnp.dot(a, b, preferred_element_type=F32)


def _dot_nt(a, bt):
    return lax.dot_general(a, bt, (((1,), (1,)), ((), ())), preferred_element_type=F32)


def _dot_hi(a, b):
    return jnp.dot(a, b, preferred_element_type=F32, precision=HIGHEST)


def _sigmoid(x):
    return 1.0 / (1.0 + jnp.exp(-x))


def _silu(x):
    return x * _sigmoid(x)


def _log_sigmoid(x):
    return jnp.minimum(x, 0.0) - jnp.log(1.0 + jnp.exp(-jnp.abs(x)))


def _params(*sem):
    return pltpu.CompilerParams(dimension_semantics=sem, vmem_limit_bytes=VMEM_LIMIT)


@functools.lru_cache(maxsize=None)
def _dft_table(L):
    N = 2 * L
    k = np.arange(L, dtype=np.int64)[:, None]
    n = np.arange(L, dtype=np.int64)[None, :]
    ang = 2.0 * np.pi * ((k * n) % N).astype(np.float64) / N
    c = np.cos(ang)
    s = -np.sin(ang)
    s[0, :] = 1.0 - 2.0 * (np.arange(L) % 2)
    return np.concatenate([c, s], axis=0).astype(np.float32)


def _dft_operands(L):
    f = jnp.asarray(_dft_table(L)).astype(BF16)
    return f, f.T


def _split_bf16(x):
    hi = x.astype(BF16)
    return hi, (x - hi.astype(F32)).astype(BF16)


def _mod_kernel(c_ref, w_ref, b_ref, o_ref):
    a_hi, a_lo = _split_bf16(_silu(c_ref[...]))
    w_hi, w_lo = _split_bf16(w_ref[...])
    o_ref[...] = (_dot(a_hi, w_hi) + (_dot(a_lo, w_hi) + _dot(a_hi, w_lo))) + b_ref[...]


def _mods(cvecs, w_ada, b_ada):
    depth, d, d3 = w_ada.shape
    r = cvecs.shape[0]
    tn = 1024
    return pl.pallas_call(
        _mod_kernel,
        grid=(depth, d3 // tn),
        in_specs=[pl.BlockSpec((r, d), lambda l, j: (0, 0)),
                  pl.BlockSpec((None, d, tn), lambda l, j: (l, 0, j)),
                  pl.BlockSpec((None, 1, tn), lambda l, j: (l, 0, j))],
        out_specs=pl.BlockSpec((None, r, tn), lambda l, j: (l, 0, j)),
        out_shape=jax.ShapeDtypeStruct((depth, r, d3), F32),
        compiler_params=_params("arbitrary", "arbitrary"),
        name="adaln_mod",
    )(cvecs, w_ada, b_ada.reshape(depth, 1, d3))


def _norm_mod_kernel(x_ref, g_ref, mod_ref, o_ref):
    x = x_ref[...]
    y = x * lax.rsqrt(jnp.mean(x * x, axis=-1, keepdims=True) + EPS) * g_ref[...]
    o_ref[...] = (y * (1.0 + mod_ref[1:2, :]) + mod_ref[0:1, :]).astype(o_ref.dtype)


def _mod_spec(d, l, row0):
    return pl.BlockSpec((None, None, 3, d), lambda b, i: (l, row0 + b, 0, 0))


def _norm_mod(x, norm_g, mods, l, row0):
    bm, t, d = x.shape
    tm = min(t, 2048)
    return pl.pallas_call(
        _norm_mod_kernel,
        grid=(bm, t // tm),
        in_specs=[pl.BlockSpec((None, tm, d), lambda b, i: (b, i, 0)),
                  pl.BlockSpec((None, 1, d), lambda b, i: (l, 0, 0)),
                  _mod_spec(d, l, row0)],
        out_specs=pl.BlockSpec((None, tm, d), lambda b, i: (b, i, 0)),
        out_shape=jax.ShapeDtypeStruct((bm, t, d), BF16),
        compiler_params=_params("arbitrary", "arbitrary"),
        name="norm_mod",
    )(x, norm_g, mods)


def _filter_kernel(w1_ref, b1_ref, w2_ref, b2_ref, freq_ref, w3_00, w3_01, w3_10, w3_11,
                   b3_ref, decay_ref, *rest, lengths):
    n = len(lengths)
    f_refs, spec_refs, hdn_refs = rest[:n], rest[n:2 * n], rest[2 * n:]

    @pl.when(pl.program_id(1) == 0)
    def _():
        fr = -(-(1 + 2 * HY_BANDS) // SUBLANES) * SUBLANES
        for L, hdn_ref in zip(lengths, hdn_refs):
            r = lax.broadcasted_iota(jnp.int32, (fr, L), 0)
            t = lax.broadcasted_iota(jnp.int32, (fr, L), 1).astype(F32) / L
            band = jnp.where(r <= HY_BANDS, r, r - HY_BANDS).astype(F32)
            ang = 2.0 * math.pi * t * band
            feats = jnp.where(r == 0, t,
                              jnp.where(r <= HY_BANDS, jnp.cos(ang),
                                        jnp.where(r <= 2 * HY_BANDS, jnp.sin(ang), 0.0)))
            feats = jnp.concatenate([feats, jnp.zeros((LANES - fr, L), F32)], axis=0)
            hdn = jnp.sin(freq_ref[:, 0:1] * (_dot_hi(w1_ref[...], feats) + b1_ref[...]))
            hdn = jnp.sin(freq_ref[:, 1:2] * (_dot_hi(w2_ref[...], hdn) + b2_ref[...]))
            hdn_ref[...] = hdn.T

    w3 = [[w3_00[...].astype(BF16), w3_01[...].astype(BF16)],
          [w3_10[...].astype(BF16), w3_11[...].astype(BF16)]]
    for L, f_ref, spec_ref, hdn_ref in zip(lengths, f_refs, spec_refs, hdn_refs):
        _filter_spectrum(L, hdn_ref[...].astype(BF16), w3, b3_ref, decay_ref, f_ref, spec_ref)


def _filter_spectrum(L, hdn, w3, b3_ref, decay_ref, f_ref, spec_ref):
    cb = spec_ref.shape[-1]
    row_i = lax.broadcasted_iota(jnp.int32, (L, cb), 0)
    t = row_i.astype(F32) / L
    first = row_i == 0
    sign = jnp.where((row_i & 1) == 0, 1.0, -1.0)
    for o in range(HY_ORDER):
        hs = []
        for dr in range(2):
            j = 2 * o + dr
            hv = _dot(hdn, w3[o][dr]) + b3_ref[j:j + 1, :]
            win = jnp.exp(-t * jnp.abs(decay_ref[j:j + 1, :])) + HY_SHIFT
            hs.append(hv * win)
        hf = hs[0]
        hb = jnp.where(first, 0.0, hs[1])
        l1 = jnp.sum(jnp.abs(hf), axis=0, keepdims=True) + jnp.sum(jnp.abs(hb), axis=0, keepdims=True)
        inv_l1 = 1.0 / l1
        even = (hf + hb) * inv_l1
        odd = (hf - hb) * inv_l1
        re = _dot(f_ref[0:L, :], even.astype(BF16))
        im = _dot(f_ref[L:2 * L, :], odd.astype(BF16))
        nyq = jnp.sum(even * sign, axis=0, keepdims=True)
        a = re * jnp.where(first, 0.5 / L, 1.0 / L)
        spec_ref[o, 0] = a.astype(spec_ref.dtype)
        spec_ref[o, 1] = jnp.where(first, 0.0, im * (1.0 / L)).astype(spec_ref.dtype)
        spec_ref[o, 2] = jnp.where(first, nyq * (0.5 / L), a).astype(spec_ref.dtype)


def _filter_params(hy_w1, hy_b1, hy_w2, hy_b2, hy_w3, hy_b3, hy_freq, hy_decay):
    depth, emb, ff = hy_w1.shape
    d_hy = hy_decay.shape[-1]
    w1t = jnp.zeros((depth, ff, LANES), F32).at[:, :, :emb].set(hy_w1.transpose(0, 2, 1))
    return (w1t, hy_b1.reshape(depth, ff, 1), hy_w2.transpose(0, 2, 1), hy_b2.reshape(depth, ff, 1),
            hy_freq.transpose(0, 2, 1), hy_w3, hy_b3.reshape(depth, 2 * HY_ORDER, d_hy),
            hy_decay.reshape(depth, 2 * HY_ORDER, d_hy))


def _hyena_spectra(dfts, filter_params):
    w1t, b1, w2t, b2, freq, w3, b3, decay = filter_params
    depth, ff = w1t.shape[:2]
    d_hy = decay.shape[-1]
    cb = HY_CB
    ncb = d_hy // cb
    lengths = tuple(sorted(dfts))
    layer = lambda l, j: (l, 0, 0)

    def w3_spec(o, dr):
        return pl.BlockSpec((None, ff, cb), lambda l, j, o=o, dr=dr: (l, 0, (2 * o + dr) * ncb + j))

    spectra = pl.pallas_call(
        functools.partial(_filter_kernel, lengths=lengths),
        grid=(depth, ncb),
        in_specs=[pl.BlockSpec((None, ff, LANES), layer),
                  pl.BlockSpec((None, ff, 1), layer),
                  pl.BlockSpec((None, ff, ff), layer),
                  pl.BlockSpec((None, ff, 1), layer),
                  pl.BlockSpec((None, ff, 2), layer),
                  w3_spec(0, 0), w3_spec(0, 1), w3_spec(1, 0), w3_spec(1, 1),
                  pl.BlockSpec((None, 2 * HY_ORDER, cb), lambda l, j: (l, 0, j)),
                  pl.BlockSpec((None, 2 * HY_ORDER, cb), lambda l, j: (l, 0, j))]
                 + [pl.BlockSpec((2 * L, L), lambda l, j: (0, 0)) for L in lengths],
        out_specs=tuple(pl.BlockSpec((None, HY_ORDER, 3, L, cb), lambda l, j: (l, 0, 0, 0, j))
                        for L in lengths),
        out_shape=tuple(jax.ShapeDtypeStruct((depth, HY_ORDER, 3, L, d_hy), BF16) for L in lengths),
        scratch_shapes=[pltpu.VMEM((L, ff), F32) for L in lengths],
        compiler_params=_params("arbitrary", "arbitrary"),
        name="hyena_filter",
    )(w1t, b1, w2t, b2, freq, w3, w3, w3, w3, b3, decay, *[dfts[L][0] for L in lengths])
    return dict(zip(lengths, spectra))


def _short_conv3(x, w_ref, b, seq=None):
    n, c = x.shape
    seq = n if seq is None else seq
    prev = pltpu.roll(x, 1, axis=0)
    nxt = pltpu.roll(x, n - 1, axis=0)
    row = lax.broadcasted_iota(jnp.int32, (SUBLANES, c), 0)
    prev_parts, nxt_parts = [], []
    for s in range(0, n, seq):
        e = s + seq
        prev_parts += [jnp.where(row == 0, 0.0, prev[s:s + SUBLANES]), prev[s + SUBLANES:e]]
        nxt_parts += [nxt[s:e - SUBLANES], jnp.where(row == SUBLANES - 1, 0.0, nxt[e - SUBLANES:e])]
    prev = jnp.concatenate(prev_parts, axis=0)
    nxt = jnp.concatenate(nxt_parts, axis=0)
    return ((b + prev * w_ref[0:1, :]) + x * w_ref[1:2, :]) + nxt * w_ref[2:3, :]


def _hyena_kernel(h_ref, wx1_ref, wx2_ref, wv_ref, wz_ref, cw_ref, cb_ref, spec_ref, hbias_ref,
                  f_ref, g_ref, o_ref, *, nb, L):
    seqs = range(nb)
    wv = wv_ref[...].astype(BF16)
    z = [_short_conv3(_dot_nt(h_ref[b], wv), cw_ref.at[2], cb_ref[2:3, :]) for b in seqs]
    gate_w = (wx1_ref, wx2_ref)
    for o in range(HY_ORDER):
        zf = [_dot(f_ref[...], z[b].astype(BF16)) for b in seqs]
        wg = gate_w[o][...].astype(BF16)
        gate = [_short_conv3(_dot_nt(h_ref[b], wg), cw_ref.at[o], cb_ref[o:o + 1, :]) for b in seqs]
        a, bm, dm = (spec_ref[o, i].astype(F32) for i in range(3))
        ycat = []
        for b in seqs:
            zc, zs = zf[b][0:L], zf[b][L:2 * L]
            ycat.append(jnp.concatenate([zc * a - zs * bm, zc * bm + zs * dm], axis=0).astype(BF16))
        y = [_dot(g_ref[...], ycat[b]) for b in seqs]
        z = [gate[b] * (y[b] + hbias_ref[o:o + 1, :] * z[b]) for b in seqs]
    wz = wz_ref[...].astype(BF16)
    for b in seqs:
        o_ref[b] = (z[b] * _silu(_dot_nt(h_ref[b], wz))).astype(o_ref.dtype)


def _hyena(h, l, w_t, conv_w, conv_b, spec, hy_bias, dft):
    bsz, L, d = h.shape
    d_hy = hy_bias.shape[-1]
    cb = HY_CB
    ncb = d_hy // cb
    nb = max(1, min(bsz, HY_ROWS // L))
    f, g = dft
    once = pl.Buffered(1)

    def w_spec(seg):
        return pl.BlockSpec((None, cb, d), lambda j, i, seg=seg: (l, seg * ncb + j, 0))

    return pl.pallas_call(
        functools.partial(_hyena_kernel, nb=nb, L=L),
        grid=(ncb, bsz // nb),
        in_specs=[pl.BlockSpec((nb, L, d), lambda j, i: (i, 0, 0)),
                  w_spec(0), w_spec(1), w_spec(2), w_spec(3),
                  pl.BlockSpec((None, 3, 3, cb), lambda j, i: (l, 0, 0, j)),
                  pl.BlockSpec((None, 3, cb), lambda j, i: (l, 0, j)),
                  pl.BlockSpec((None, HY_ORDER, 3, L, cb), lambda j, i: (l, 0, 0, 0, j)),
                  pl.BlockSpec((None, HY_ORDER, cb), lambda j, i: (l, 0, j)),
                  pl.BlockSpec((2 * L, L), lambda j, i: (0, 0), pipeline_mode=once),
                  pl.BlockSpec((L, 2 * L), lambda j, i: (0, 0), pipeline_mode=once)],
        out_specs=pl.BlockSpec((nb, L, cb), lambda j, i: (i, 0, j)),
        out_shape=jax.ShapeDtypeStruct((bsz, L, d_hy), BF16),
        compiler_params=_params("arbitrary", "arbitrary"),
        name="hyena",
    )(h, w_t, w_t, w_t, w_t, conv_w, conv_b, spec, hy_bias, f, g)


def _mlstm_kernel(*refs, nb, L, has_state, n_aliased):
    (h_ref, wq_ref, wk_ref, wv_ref, wo_ref, wz_ref, wg_ref, gb_ref, cwq_ref, cwk_ref,
     cbq_ref, cbk_ref, ng_ref) = refs[:13]
    refs = refs[13:]
    if has_state:
        c0_ref, n0_ref, m0_ref, yb_ref = refs[:4]
        refs = refs[4:]
    else:
        yb_ref, cn_ref, nn_ref, mn_ref = refs[n_aliased:n_aliased + 4]
        refs = refs[n_aliased + 4:]
    q_s, kt_s, v_s, g2_s, g2t_s, sc_s, hdir_s, og_s, c_s = refs

    T = ML_CHUNK
    nc = L // T
    R = nb * L
    d = h_ref.shape[-1]
    dh = q_s.shape[-1]
    hb = h_ref[...].reshape(R, d)

    g = _dot_nt(hb, wg_ref[...]) + gb_ref[...]
    lf = _log_sigmoid(g)
    ti = lax.broadcasted_iota(jnp.int32, (T, T), 0)
    si = lax.broadcasted_iota(jnp.int32, (T, T), 1)
    causal = si <= ti
    anti = si >= ti
    lower = causal.astype(BF16)
    lf_hi = lf.astype(BF16)
    lf_r = lf - lf_hi.astype(F32)
    lf_mid = lf_r.astype(BF16)
    lf_lo = (lf_r - lf_mid.astype(F32)).astype(BF16)
    lane = lax.broadcasted_iota(jnp.int32, (T, LANES), 1)
    for j in range(nb * nc):
        rows = slice(j * T, (j + 1) * T)
        pre = _dot(lower, lf_hi[rows]) + (_dot(lower, lf_mid[rows]) + _dot(lower, lf_lo[rows]))
        suf = pre[T - 1:T, :] - pre + lf[rows]
        g2_s[rows, :] = jnp.where(lane == 1, pre, jnp.where(lane == 3, suf, g[rows]))
    g2t_s[...] = g2_s[...].T[0:8, :]

    zero11 = jnp.zeros((1, 1), F32)
    m_fin = []
    for b in range(nb):
        vals = [[None] * 6 for _ in range(nc)]
        for dr in range(2):
            m = m0_ref[b, dr][:, 0:1] if has_state else zero11
            for ci in range(nc):
                c = ci if dr == 0 else nc - 1 - ci
                cols = slice(b * L + c * T, b * L + (c + 1) * T)
                irow = g2t_s[2 * dr:2 * dr + 1, cols]
                brow = g2t_s[2 * dr + 1:2 * dr + 2, cols]
                bl = brow[:, T - 1:T] if dr == 0 else brow[:, 0:1]
                mw = jnp.max(bl - brow + irow, axis=-1, keepdims=True)
                mn = jnp.maximum(bl + m, mw)
                vals[c][3 * dr:3 * dr + 3] = [m, mn, bl]
                m = mn
            m_fin.append(m)
        for c in range(nc):
            rows8 = [jnp.broadcast_to(v, (1, LANES)) for v in vals[c]] + [jnp.zeros((2, LANES), F32)]
            sc_s[b * nc + c] = jnp.concatenate(rows8, axis=0)

    uq =_dot_nt(hb, wq_ref[...].astype(BF16))
    uk = _dot_nt(hb, wk_ref[...].astype(BF16))
    q_s[...] = _silu(_short_conv3(uq, cwq_ref, cbq_ref[...], L)).astype(BF16)
    uv = _dot_nt(hb, wv_ref[...].astype(BF16))
    k = _silu(_short_conv3(uk, cwk_ref, cbk_ref[...], L)) * (dh ** -0.5)
    kt_s[...] = k.T.astype(BF16)
    uo = _dot_nt(hb, wo_ref[...].astype(BF16))
    v_s[:, 0:dh] = uv.astype(BF16)
    v_s[:, dh:dh + LANES] = jnp.ones((R, LANES), BF16)
    uz = _dot_nt(hb, wz_ref[...].astype(BF16))
    og_s[...] = _sigmoid(uo) * _silu(uz)

    for b in range(nb):
        for dr in range(2):
            if has_state:
                c_s[2 * b + dr, :, 0:dh] = c0_ref[b, dr]
                c_s[2 * b + dr, :, dh:dh + LANES] = jnp.broadcast_to(n0_ref[b, dr], (LANES, dh)).T
            else:
                c_s[2 * b + dr] = jnp.zeros((dh, dh + LANES), F32)

    def step(ci, carry):
        for b in range(nb):
            for dr in range(2):
                c = ci if dr == 0 else nc - 1 - ci
                r0 = pl.multiple_of(b * L + c * T, T)
                qc = q_s[pl.ds(r0, T), :]
                vc = v_s[pl.ds(r0, T), :]
                ktc = kt_s[:, pl.ds(r0, T)]
                b_rep = jnp.broadcast_to(g2_s[pl.ds(r0, T), :][:, 2 * dr + 1:2 * dr + 2], (T, LANES))
                g2tc = g2t_s[:, pl.ds(r0, T)]
                irow = g2tc[2 * dr:2 * dr + 1, :]
                brow = g2tc[2 * dr + 1:2 * dr + 2, :]
                sc = sc_s[b * nc + c]
                m_prev = sc[3 * dr:3 * dr + 1, :]
                m_new = sc[3 * dr + 1:3 * dr + 2, :]
                bl = sc[3 * dr + 2:3 * dr + 3, :]
                c_old = c_s[2 * b + dr]

                dlog = jnp.where(causal if dr == 0 else anti, b_rep - brow + irow, NEG)
                inter = b_rep + m_prev
                mt = jnp.maximum(inter, jnp.max(dlog, axis=-1, keepdims=True))
                smat = _dot(qc, ktc) * jnp.exp(dlog - mt)
                iw = jnp.exp(inter - mt)
                ext = (_dot(smat.astype(BF16), vc)
                       + jnp.concatenate([iw, iw, iw], axis=-1) * _dot(qc, c_old.astype(BF16)))
                inv = 1.0 / jnp.maximum(jnp.abs(ext[:, dh:dh + LANES]), jnp.exp(-mt))
                hh = ext[:, 0:dh] * jnp.concatenate([inv, inv], axis=-1)
                hdir_s[dr, pl.ds(r0, T), :] = hh

                ws = jnp.exp(bl - brow + irow - m_new)
                dec = jnp.exp(bl + m_prev - m_new)[:, 0:1]
                kw = (ktc.astype(F32) * ws).astype(BF16)
                c_s[2 * b + dr] = dec * c_old + _dot(kw, vc)
        return carry

    lax.fori_loop(0, nc, step, 0, unroll=max(1, min(nc, ML_STREAMS // (2 * nb))))
    if not has_state:
        for b in range(nb):
            for dr in range(2):
                cn_ref[b, dr] = c_s[2 * b + dr, :, 0:dh]
                nn_ref[b, dr] = c_s[2 * b + dr, :, dh:dh + LANES].T[0:1, :]
                mn_ref[b, dr] = jnp.broadcast_to(m_fin[2 * b + dr], (1, LANES))

    hs = hdir_s[0] + hdir_s[1]
    hm = hs * lax.rsqrt(jnp.mean(hs * hs, axis=-1, keepdims=True) + EPS) * ng_ref[...]
    yb_ref[...] = (hm * og_s[...]).astype(yb_ref.dtype).reshape(yb_ref.shape)


def _mlstm(h, l, depth, w_in, wg_heads, gb_heads, conv_w, conv_b, norm_g, states, carried,
           seg_off):
    bsz, L, d = h.shape
    heads = wg_heads.shape[1]
    d_ml = norm_g.shape[-1]
    dh = d_ml // heads
    has_state = states is not None
    nb = max(1, min(bsz, ML_ROWS // L))
    nc = L // ML_CHUNK
    qo, ko, vo, oo, zo = (s // dh for s in seg_off)

    def w_spec(off):
        return pl.BlockSpec((None, dh, d), lambda hd, b, off=off: (l, off + hd, 0))

    def vec_spec(rows, off):
        return pl.BlockSpec((None, rows, dh), lambda hd, b, off=off: (l, 0, off + hd))

    in_specs = [pl.BlockSpec((nb, L, d), lambda hd, b: (b, 0, 0)),
                w_spec(qo), w_spec(ko), w_spec(vo), w_spec(oo), w_spec(zo),
                pl.BlockSpec((None, None, LANES, d), lambda hd, b: (l, hd, 0, 0)),
                pl.BlockSpec((None, None, 1, LANES), lambda hd, b: (l, hd, 0, 0)),
                vec_spec(3, 0), vec_spec(3, heads), vec_spec(1, 0), vec_spec(1, heads),
                vec_spec(1, 0)]
    args = [h, w_in, w_in, w_in, w_in, w_in, wg_heads, gb_heads,
            conv_w, conv_w, conv_b, conv_b, norm_g]
    yb_spec = pl.BlockSpec((nb, L, dh), lambda hd, b: (b, 0, hd))
    yb_shape = jax.ShapeDtypeStruct((bsz, L, d_ml), BF16)
    state_idx = lambda hd, b: (b, l, 0, hd, 0, 0)
    c_spec = pl.BlockSpec((nb, None, 2, None, dh, dh), state_idx)
    n_spec = pl.BlockSpec((nb, None, 2, None, 1, dh), state_idx)
    m_spec = pl.BlockSpec((nb, None, 2, None, 1, LANES), state_idx)
    aliases = {}
    if has_state:
        in_specs += [c_spec, n_spec, m_spec]
        args += list(states)
        out_specs, out_shape = yb_spec, yb_shape
    else:
        if carried is not None:
            aliases = {len(args) + i: 1 + i for i in range(3)}
            in_specs += [pl.BlockSpec(memory_space=pl.ANY)] * 3
            args += list(carried)
        out_specs = (yb_spec, c_spec, n_spec, m_spec)
        out_shape = (yb_shape,
                     jax.ShapeDtypeStruct((bsz, depth, 2, heads, dh, dh), F32),
                     jax.ShapeDtypeStruct((bsz, depth, 2, heads, 1, dh), F32),
                     jax.ShapeDtypeStruct((bsz, depth, 2, heads, 1, LANES), F32))
    rows = nb * L
    scratch = [pltpu.VMEM((rows, dh), BF16),
               pltpu.VMEM((dh, rows), BF16),
               pltpu.VMEM((rows, dh + LANES), BF16),
               pltpu.VMEM((rows, LANES), F32),
               pltpu.VMEM((8, rows), F32),
               pltpu.VMEM((nb * nc, 8, LANES), F32),
               pltpu.VMEM((2, rows, dh), F32),
               pltpu.VMEM((rows, dh), F32),
               pltpu.VMEM((2 * nb, dh, dh + LANES), F32)]
    return pl.pallas_call(
        functools.partial(_mlstm_kernel, nb=nb, L=L, has_state=has_state, n_aliased=len(aliases)),
        grid=(heads, bsz // nb),
        in_specs=in_specs,
        out_specs=out_specs,
        out_shape=out_shape,
        input_output_aliases=aliases,
        scratch_shapes=scratch,
        compiler_params=_params("arbitrary", "arbitrary"),
        name="mlstm",
    )(*args)


OUT_WEIGHTS = 5
OUT_STAGES = 2


def _out_kernel(x_ref, h_ref, ya_ref, yb_ref, mod_ref, wt_hbm, wpa_hbm, wpb_hbm, wout_hbm,
                g_ref, *rest, last, l, off_ga):
    w_s, stage, sem = rest[-3:]
    rest = rest[:-3]
    d = w_s.shape[-1]
    srcs = (wt_hbm.at[l, pl.ds(off_ga, d), :], wt_hbm.at[l, pl.ds(off_ga + d, d), :],
            wpa_hbm.at[l], wpb_hbm.at[l], wout_hbm.at[l])

    def copy(i):
        slot = i % OUT_STAGES
        return pltpu.make_async_copy(srcs[i], stage.at[slot], sem.at[slot])

    def fetched(i):
        copy(i).wait()
        w_s[i] = stage[i % OUT_STAGES].astype(BF16)
        if i + OUT_STAGES < OUT_WEIGHTS:
            copy(i + OUT_STAGES).start()
        return w_s[i]

    def body(weight):
        hb = h_ref[...]
        ga = _sigmoid(_dot_nt(hb, weight(0)))
        gb = _sigmoid(_dot_nt(hb, weight(1)))
        merged = ga * _dot(ya_ref[...], weight(2)) + gb * _dot(yb_ref[...], weight(3))
        xn = x_ref[...] + mod_ref[2:3, :] * _dot(merged.astype(BF16), weight(4))
        normed = xn * lax.rsqrt(jnp.mean(xn * xn, axis=-1, keepdims=True) + EPS) * g_ref[...]
        if last:
            (y_ref,) = rest
            y_ref[...] = normed
        else:
            modn_ref, xo_ref, ho_ref = rest
            xo_ref[...] = xn
            ho_ref[...] = (normed * (1.0 + modn_ref[1:2, :]) + modn_ref[0:1, :]).astype(ho_ref.dtype)

    first = (pl.program_id(0) == 0) & (pl.program_id(1) == 0)

    @pl.when(first)
    def _():
        for i in range(OUT_STAGES):
            copy(i).start()
        body(fetched)

    @pl.when(jnp.logical_not(first))
    def _():
        body(lambda i: w_s[i])


def _out(x, h, ya, yb, mods, row0, l, w_t, off_ga, w_pa, w_pb, w_out, norm_g, final_g):
    bm, t, d = x.shape
    tm = min(t, OUT_TM)
    last = l == mods.shape[0] - 1
    tok = lambda b, i: (b, i, 0)
    tok_spec = pl.BlockSpec((None, tm, d), tok)
    hbm = pl.BlockSpec(memory_space=pl.ANY)
    g_spec = pl.BlockSpec((None, 1, d), lambda b, i: (0 if last else l + 1, 0, 0))
    in_specs = [tok_spec, tok_spec, tok_spec, tok_spec, _mod_spec(d, l, row0), hbm, hbm, hbm, hbm,
                g_spec]
    args = [x, h, ya, yb, mods, w_t, w_pa, w_pb, w_out, final_g.reshape(1, 1, d) if last else norm_g]
    if last:
        out_specs, out_shape = tok_spec, jax.ShapeDtypeStruct((bm, t, d), F32)
    else:
        in_specs.append(_mod_spec(d, l + 1, row0))
        args.append(mods)
        out_specs = (tok_spec, tok_spec)
        out_shape = (jax.ShapeDtypeStruct((bm, t, d), F32), jax.ShapeDtypeStruct((bm, t, d), BF16))
    return pl.pallas_call(
        functools.partial(_out_kernel, last=last, l=l, off_ga=off_ga),
        grid=(bm, t // tm),
        in_specs=in_specs,
        out_specs=out_specs,
        out_shape=out_shape,
        scratch_shapes=[pltpu.VMEM((OUT_WEIGHTS, d, d), BF16),
                        pltpu.VMEM((OUT_STAGES, d, d), F32),
                        pltpu.SemaphoreType.DMA((OUT_STAGES,))],
        compiler_params=_params("arbitrary", "arbitrary"),
        name="merge_out",
    )(*args)


def kernel(x_prompt, x_sample, state_C, state_n, state_m, c, c_ctx, norm_g, w_ada, b_ada, w_in, hy_conv_w, hy_conv_b, hy_w1, hy_b1, hy_w2, hy_b2, hy_w3, hy_b3, hy_freq, hy_decay, hy_bias, ml_conv_w, ml_conv_b, ml_if_b, ml_norm_g, w_pa, w_pb, w_out, final_g):
    depth, d, _ = w_in.shape
    d_hy = hy_bias.shape[-1]
    d_ml = ml_norm_g.shape[-1]
    heads = ml_if_b.shape[-1]
    bp, lp, _ = x_prompt.shape
    bs, ls, _ = x_sample.shape

    off_q = 4 * d_hy
    off_v = off_q + 2 * d_ml
    off_o = off_v + d_ml
    off_z = off_o + d_ml
    off_g = off_z + d_ml
    off_ga = off_g + 4 * heads
    seg_off = (off_q, off_q + d_ml, off_v, off_o, off_z)

    w_t = jnp.transpose(w_in, (0, 2, 1))
    wg = w_t[:, off_g:off_ga, :].reshape(depth, 4, heads, d).transpose(0, 2, 1, 3)
    wg_heads = jnp.zeros((depth, heads, LANES, d), BF16).at[:, :, :4, :].set(wg.astype(BF16))
    gb = ml_if_b.reshape(depth, 4, heads).transpose(0, 2, 1)
    gb_heads = jnp.zeros((depth, heads, 1, LANES), F32).at[:, :, 0, :4].set(gb)

    cvecs = jnp.concatenate([c_ctx[None], c, jnp.zeros((-(1 + bs) % SUBLANES, d), F32)], axis=0)
    mods = _mods(cvecs, w_ada, b_ada).reshape(depth, cvecs.shape[0], 3, d)

    norm_g3 = norm_g.reshape(depth, 1, d)
    hy_cw = hy_conv_w.reshape(depth, 3, 3, d_hy).transpose(0, 2, 1, 3)
    hy_cb = hy_conv_b.reshape(depth, 3, d_hy)
    ml_cb = ml_conv_b.reshape(depth, 1, 2 * d_ml)
    ml_ng = ml_norm_g.reshape(depth, 1, d_ml)
    filter_params = _filter_params(hy_w1, hy_b1, hy_w2, hy_b2, hy_w3, hy_b3, hy_freq, hy_decay)

    xp = x_prompt.reshape(1, bp * lp, d)
    xs = x_sample
    cached = (state_C, state_n.reshape(bs, depth, 2, heads, 1, -1),
              jnp.broadcast_to(state_m[..., None, None], (bs, depth, 2, heads, 1, LANES)))
    fin = None
    hp = _norm_mod(xp, norm_g3, mods, 0, 0)
    hs = _norm_mod(xs, norm_g3, mods, 0, 1)
    dft = {L: _dft_operands(L) for L in sorted({lp, ls})}
    spectra = _hyena_spectra(dft, filter_params)
    for l in range(depth):
        last = l == depth - 1

        def layer(x, h, row0, bsz, L, states, carried):
            hseq = h.reshape(bsz, L, d)
            ya = _hyena(hseq, l, w_t, hy_cw, hy_cb, spectra[L], hy_bias, dft[L])
            res = _mlstm(hseq, l, depth, w_t, wg_heads, gb_heads, ml_conv_w, ml_cb, ml_ng,
                         states, carried, seg_off)
            yb, fin = (res, None) if states is not None else (res[0], res[1:])
            res = _out(x, h, ya.reshape(x.shape), yb.reshape(x.shape), mods, row0, l,
                       w_t, off_ga, w_pa, w_pb, w_out, norm_g3, final_g)
            return (res, None, fin) if last else (res[0], res[1], fin)

        xp, hp, fin = layer(xp, hp, 0, bp, lp, None, fin)
        xs, hs, _ = layer(xs, hs, 1, bs, ls, cached, None)

    return (xp.reshape(bp, lp, d), xs, fin[0], fin[1][:, :, :, :, 0, :], fin[2][:, :, :, :, 0, 0])
```

```python
import functools
import math

import numpy as np
import jax
import jax.numpy as jnp
from jax import lax
from jax.experimental import pallas as pl
from jax.experimental.pallas import tpu as pltpu

F32 = jnp.float32
BF16 = jnp.bfloat16
HIGHEST = lax.Precision.HIGHEST

HY_ORDER = 2
HY_BANDS = 16
HY_SHIFT = 0.05
ML_CHUNK = 128
EPS = 1e-6
NEG = -1e30

LANES = 128
SUBLANES = 8
VMEM_LIMIT = 58 * 1024 * 1024
HY_CB = 256
ML_ROWS = 2048
HY_ROWS = 2048
ML_STREAMS = 8
OUT_TM = 512
NORM_TM = 2048
MOD_TN = 1024


def _dot(a, b):
    return jnp.dot(a, b, preferred_element_type=F32)


def _dot_nt(a, bt):
    return lax.dot_general(a, bt, (((1,), (1,)), ((), ())), preferred_element_type=F32)


def _dot_hi(a, b):
    return jnp.dot(a, b, preferred_element_type=F32, precision=HIGHEST)


def _sigmoid(x):
    return 1.0 / (1.0 + jnp.exp(-x))


def _silu(x):
    return x * _sigmoid(x)


def _log_sigmoid(x):
    return jnp.minimum(x, 0.0) - jnp.log(1.0 + jnp.exp(-jnp.abs(x)))


def _params(*sem):
    return pltpu.CompilerParams(dimension_semantics=sem, vmem_limit_bytes=VMEM_LIMIT)


@functools.lru_cache(maxsize=None)
def _dft_table(L):
    N = 2 * L
    k = np.arange(L, dtype=np.int64)[:, None]
    n = np.arange(L, dtype=np.int64)[None, :]
    ang = 2.0 * np.pi * ((k * n) % N).astype(np.float64) / N
    c = np.cos(ang)
    s = -np.sin(ang)
    s[0, :] = 1.0 - 2.0 * (np.arange(L) % 2)
    return np.concatenate([c, s], axis=0).astype(np.float32)


def _dft_operands(L):
    f = jnp.asarray(_dft_table(L)).astype(BF16)
    return f, f.T


def _split_bf16(x):
    hi = x.astype(BF16)
    return hi, (x - hi.astype(F32)).astype(BF16)


def _mod_kernel(c_ref, w_ref, b_ref, o_ref):
    a_hi, a_lo = _split_bf16(_silu(c_ref[...]))
    w_hi, w_lo = _split_bf16(w_ref[...])
    o_ref[...] = (_dot(a_hi, w_hi) + (_dot(a_lo, w_hi) + _dot(a_hi, w_lo))) + b_ref[...]


def _mods(cvecs, w_ada, b_ada):
    depth, d, d3 = w_ada.shape
    r = cvecs.shape[0]
    tn = min(d3, MOD_TN)
    return pl.pallas_call(
        _mod_kernel,
        grid=(depth, d3 // tn),
        in_specs=[pl.BlockSpec((r, d), lambda l, j: (0, 0)),
                  pl.BlockSpec((None, d, tn), lambda l, j: (l, 0, j)),
                  pl.BlockSpec((None, 1, tn), lambda l, j: (l, 0, j))],
        out_specs=pl.BlockSpec((None, r, tn), lambda l, j: (l, 0, j)),
        out_shape=jax.ShapeDtypeStruct((depth, r, d3), F32),
        compiler_params=_params("arbitrary", "arbitrary"),
        name="adaln_mod",
    )(cvecs, w_ada, b_ada.reshape(depth, 1, d3))


def _norm_mod_kernel(x_ref, g_ref, mod_ref, o_ref):
    x = x_ref[...]
    y = x * lax.rsqrt(jnp.mean(x * x, axis=-1, keepdims=True) + EPS) * g_ref[...]
    o_ref[...] = (y * (1.0 + mod_ref[1:2, :]) + mod_ref[0:1, :]).astype(o_ref.dtype)


def _mod_spec(d, l, row0):
    return pl.BlockSpec((None, None, 3, d), lambda b, i: (l, row0 + b, 0, 0))


def _norm_mod(x, norm_g, mods, l, row0):
    bm, t, d = x.shape
    tm = min(t, NORM_TM)
    return pl.pallas_call(
        _norm_mod_kernel,
        grid=(bm, t // tm),
        in_specs=[pl.BlockSpec((None, tm, d), lambda b, i: (b, i, 0)),
                  pl.BlockSpec((None, 1, d), lambda b, i: (l, 0, 0)),
                  _mod_spec(d, l, row0)],
        out_specs=pl.BlockSpec((None, tm, d), lambda b, i: (b, i, 0)),
        out_shape=jax.ShapeDtypeStruct((bm, t, d), BF16),
        compiler_params=_params("arbitrary", "arbitrary"),
        name="norm_mod",
    )(x, norm_g, mods)


def _filter_kernel(w1_ref, b1_ref, w2_ref, b2_ref, freq_ref, w3_00, w3_01, w3_10, w3_11,
                   b3_ref, decay_ref, *rest, lengths):
    n = len(lengths)
    f_refs, spec_refs, hdn_refs = rest[:n], rest[n:2 * n], rest[2 * n:]

    @pl.when(pl.program_id(1) == 0)
    def _():
        fr = -(-(1 + 2 * HY_BANDS) // SUBLANES) * SUBLANES
        for L, hdn_ref in zip(lengths, hdn_refs):
            r = lax.broadcasted_iota(jnp.int32, (fr, L), 0)
            t = lax.broadcasted_iota(jnp.int32, (fr, L), 1).astype(F32) / L
            band = jnp.where(r <= HY_BANDS, r, r - HY_BANDS).astype(F32)
            ang = 2.0 * math.pi * t * band
            feats = jnp.where(r == 0, t,
                              jnp.where(r <= HY_BANDS, jnp.cos(ang),
                                        jnp.where(r <= 2 * HY_BANDS, jnp.sin(ang), 0.0)))
            feats = jnp.concatenate([feats, jnp.zeros((LANES - fr, L), F32)], axis=0)
            hdn = jnp.sin(freq_ref[:, 0:1] * (_dot_hi(w1_ref[...], feats) + b1_ref[...]))
            hdn = jnp.sin(freq_ref[:, 1:2] * (_dot_hi(w2_ref[...], hdn) + b2_ref[...]))
            hdn_ref[...] = hdn.T

    w3 = [[w3_00[...].astype(BF16), w3_01[...].astype(BF16)],
          [w3_10[...].astype(BF16), w3_11[...].astype(BF16)]]
    for L, f_ref, spec_ref, hdn_ref in zip(lengths, f_refs, spec_refs, hdn_refs):
        _filter_spectrum(L, hdn_ref[...].astype(BF16), w3, b3_ref, decay_ref, f_ref, spec_ref)


def _filter_spectrum(L, hdn, w3, b3_ref, decay_ref, f_ref, spec_ref):
    cb = spec_ref.shape[-1]
    row_i = lax.broadcasted_iota(jnp.int32, (L, cb), 0)
    t = row_i.astype(F32) / L
    first = row_i == 0
    sign = jnp.where((row_i & 1) == 0, 1.0, -1.0)
    for o in range(HY_ORDER):
        hs = []
        for dr in range(2):
            j = 2 * o + dr
            hv = _dot(hdn, w3[o][dr]) + b3_ref[j:j + 1, :]
            win = jnp.exp(-t * jnp.abs(decay_ref[j:j + 1, :])) + HY_SHIFT
            hs.append(hv * win)
        hf = hs[0]
        hb = jnp.where(first, 0.0, hs[1])
        l1 = jnp.sum(jnp.abs(hf), axis=0, keepdims=True) + jnp.sum(jnp.abs(hb), axis=0, keepdims=True)
        inv_l1 = 1.0 / l1
        even = (hf + hb) * inv_l1
        odd = (hf - hb) * inv_l1
        re = _dot(f_ref[0:L, :], even.astype(BF16))
        im = _dot(f_ref[L:2 * L, :], odd.astype(BF16))
        nyq = jnp.sum(even * sign, axis=0, keepdims=True)
        a = re * jnp.where(first, 0.5 / L, 1.0 / L)
        spec_ref[o, 0] = a.astype(spec_ref.dtype)
        spec_ref[o, 1] = jnp.where(first, 0.0, im * (1.0 / L)).astype(spec_ref.dtype)
        spec_ref[o, 2] = jnp.where(first, nyq * (0.5 / L), a).astype(spec_ref.dtype)


def _filter_params(hy_w1, hy_b1, hy_w2, hy_b2, hy_w3, hy_b3, hy_freq, hy_decay):
    depth, emb, ff = hy_w1.shape
    d_hy = hy_decay.shape[-1]
    w1t = jnp.zeros((depth, ff, LANES), F32).at[:, :, :emb].set(hy_w1.transpose(0, 2, 1))
    return (w1t, hy_b1.reshape(depth, ff, 1), hy_w2.transpose(0, 2, 1), hy_b2.reshape(depth, ff, 1),
            hy_freq.transpose(0, 2, 1), hy_w3, hy_b3.reshape(depth, 2 * HY_ORDER, d_hy),
            hy_decay.reshape(depth, 2 * HY_ORDER, d_hy))


def _hyena_spectra(dfts, filter_params):
    w1t, b1, w2t, b2, freq, w3, b3, decay = filter_params
    depth, ff = w1t.shape[:2]
    d_hy = decay.shape[-1]
    cb = HY_CB
    ncb = d_hy // cb
    lengths = tuple(sorted(dfts))
    layer = lambda l, j: (l, 0, 0)

    def w3_spec(o, dr):
        return pl.BlockSpec((None, ff, cb), lambda l, j, o=o, dr=dr: (l, 0, (2 * o + dr) * ncb + j))

    spectra = pl.pallas_call(
        functools.partial(_filter_kernel, lengths=lengths),
        grid=(depth, ncb),
        in_specs=[pl.BlockSpec((None, ff, LANES), layer),
                  pl.BlockSpec((None, ff, 1), layer),
                  pl.BlockSpec((None, ff, ff), layer),
                  pl.BlockSpec((None, ff, 1), layer),
                  pl.BlockSpec((None, ff, 2), layer),
                  w3_spec(0, 0), w3_spec(0, 1), w3_spec(1, 0), w3_spec(1, 1),
                  pl.BlockSpec((None, 2 * HY_ORDER, cb), lambda l, j: (l, 0, j)),
                  pl.BlockSpec((None, 2 * HY_ORDER, cb), lambda l, j: (l, 0, j))]
                 + [pl.BlockSpec((2 * L, L), lambda l, j: (0, 0)) for L in lengths],
        out_specs=tuple(pl.BlockSpec((None, HY_ORDER, 3, L, cb), lambda l, j: (l, 0, 0, 0, j))
                        for L in lengths),
        out_shape=tuple(jax.ShapeDtypeStruct((depth, HY_ORDER, 3, L, d_hy), BF16) for L in lengths),
        scratch_shapes=[pltpu.VMEM((L, ff), F32) for L in lengths],
        compiler_params=_params("arbitrary", "arbitrary"),
        name="hyena_filter",
    )(w1t, b1, w2t, b2, freq, w3, w3, w3, w3, b3, decay, *[dfts[L][0] for L in lengths])
    return dict(zip(lengths, spectra))


def _short_conv3(x, w_ref, b, seq=None):
    n, c = x.shape
    seq = n if seq is None else seq
    prev = pltpu.roll(x, 1, axis=0)
    nxt = pltpu.roll(x, n - 1, axis=0)
    row = lax.broadcasted_iota(jnp.int32, (SUBLANES, c), 0)
    prev_parts, nxt_parts = [], []
    for s in range(0, n, seq):
        e = s + seq
        prev_parts += [jnp.where(row == 0, 0.0, prev[s:s + SUBLANES]), prev[s + SUBLANES:e]]
        nxt_parts += [nxt[s:e - SUBLANES], jnp.where(row == SUBLANES - 1, 0.0, nxt[e - SUBLANES:e])]
    prev = jnp.concatenate(prev_parts, axis=0)
    nxt = jnp.concatenate(nxt_parts, axis=0)
    return ((b + prev * w_ref[0:1, :]) + x * w_ref[1:2, :]) + nxt * w_ref[2:3, :]


def _hyena_kernel(h_ref, wx1_ref, wx2_ref, wv_ref, wz_ref, cw_ref, cb_ref, spec_ref, hbias_ref,
                  f_ref, g_ref, o_ref, *, nb, L):
    seqs = range(nb)
    wv = wv_ref[...].astype(BF16)
    z = [_short_conv3(_dot_nt(h_ref[b], wv), cw_ref.at[2], cb_ref[2:3, :]) for b in seqs]
    gate_w = (wx1_ref, wx2_ref)
    for o in range(HY_ORDER):
        zf = [_dot(f_ref[...], z[b].astype(BF16)) for b in seqs]
        wg = gate_w[o][...].astype(BF16)
        gate = [_short_conv3(_dot_nt(h_ref[b], wg), cw_ref.at[o], cb_ref[o:o + 1, :]) for b in seqs]
        a, bm, dm = (spec_ref[o, i].astype(F32) for i in range(3))
        ycat = []
        for b in seqs:
            zc, zs = zf[b][0:L], zf[b][L:2 * L]
            ycat.append(jnp.concatenate([zc * a - zs * bm, zc * bm + zs * dm], axis=0).astype(BF16))
        y = [_dot(g_ref[...], ycat[b]) for b in seqs]
        z = [gate[b] * (y[b] + hbias_ref[o:o + 1, :] * z[b]) for b in seqs]
    wz = wz_ref[...].astype(BF16)
    for b in seqs:
        o_ref[b] = (z[b] * _silu(_dot_nt(h_ref[b], wz))).astype(o_ref.dtype)


def _hyena(h, l, w_t, conv_w, conv_b, spec, hy_bias, dft):
    bsz, L, d = h.shape
    d_hy = hy_bias.shape[-1]
    cb = HY_CB
    ncb = d_hy // cb
    nb = max(1, min(bsz, HY_ROWS // L))
    f, g = dft
    once = pl.Buffered(1)

    def w_spec(seg):
        return pl.BlockSpec((None, cb, d), lambda j, i, seg=seg: (l, seg * ncb + j, 0))

    return pl.pallas_call(
        functools.partial(_hyena_kernel, nb=nb, L=L),
        grid=(ncb, bsz // nb),
        in_specs=[pl.BlockSpec((nb, L, d), lambda j, i: (i, 0, 0)),
                  w_spec(0), w_spec(1), w_spec(2), w_spec(3),
                  pl.BlockSpec((None, 3, 3, cb), lambda j, i: (l, 0, 0, j)),
                  pl.BlockSpec((None, 3, cb), lambda j, i: (l, 0, j)),
                  pl.BlockSpec((None, HY_ORDER, 3, L, cb), lambda j, i: (l, 0, 0, 0, j)),
                  pl.BlockSpec((None, HY_ORDER, cb), lambda j, i: (l, 0, j)),
                  pl.BlockSpec((2 * L, L), lambda j, i: (0, 0), pipeline_mode=once),
                  pl.BlockSpec((L, 2 * L), lambda j, i: (0, 0), pipeline_mode=once)],
        out_specs=pl.BlockSpec((nb, L, cb), lambda j, i: (i, 0, j)),
        out_shape=jax.ShapeDtypeStruct((bsz, L, d_hy), BF16),
        compiler_params=_params("arbitrary", "arbitrary"),
        name="hyena",
    )(h, w_t, w_t, w_t, w_t, conv_w, conv_b, spec, hy_bias, f, g)


def _mlstm_kernel(*refs, nb, L, has_state, n_aliased):
    (h_ref, wq_ref, wk_ref, wv_ref, wo_ref, wz_ref, wg_ref, gb_ref, cwq_ref, cwk_ref,
     cbq_ref, cbk_ref, ng_ref) = refs[:13]
    refs = refs[13:]
    if has_state:
        c0_ref, n0_ref, m0_ref, yb_ref = refs[:4]
        refs = refs[4:]
    else:
        yb_ref, cn_ref, nn_ref, mn_ref = refs[n_aliased:n_aliased + 4]
        refs = refs[n_aliased + 4:]
    q_s, kt_s, v_s, g2_s, g2t_s, sc_s, hdir_s, og_s, c_s = refs

    T = ML_CHUNK
    nc = L // T
    R = nb * L
    d = h_ref.shape[-1]
    dh = q_s.shape[-1]
    hb = h_ref[...].reshape(R, d)

    g = _dot_nt(hb, wg_ref[...]) + gb_ref[...]
    lf = _log_sigmoid(g)
    ti = lax.broadcasted_iota(jnp.int32, (T, T), 0)
    si = lax.broadcasted_iota(jnp.int32, (T, T), 1)
    causal = si <= ti
    anti = si >= ti
    lower = causal.astype(BF16)
    lf_hi = lf.astype(BF16)
    lf_r = lf - lf_hi.astype(F32)
    lf_mid = lf_r.astype(BF16)
    lf_lo = (lf_r - lf_mid.astype(F32)).astype(BF16)
    lane = lax.broadcasted_iota(jnp.int32, (T, LANES), 1)
    for j in range(nb * nc):
        rows = slice(j * T, (j + 1) * T)
        pre = _dot(lower, lf_hi[rows]) + (_dot(lower, lf_mid[rows]) + _dot(lower, lf_lo[rows]))
        suf = pre[T - 1:T, :] - pre + lf[rows]
        g2_s[rows, :] = jnp.where(lane == 1, pre, jnp.where(lane == 3, suf, g[rows]))
    g2t_s[...] = g2_s[...].T[0:SUBLANES, :]

    zero11 = jnp.zeros((1, 1), F32)
    m_fin = []
    for b in range(nb):
        vals = [[None] * 6 for _ in range(nc)]
        for dr in range(2):
            m = m0_ref[b, dr][:, 0:1] if has_state else zero11
            for ci in range(nc):
                c = ci if dr == 0 else nc - 1 - ci
                cols = slice(b * L + c * T, b * L + (c + 1) * T)
                irow = g2t_s[2 * dr:2 * dr + 1, cols]
                brow = g2t_s[2 * dr + 1:2 * dr + 2, cols]
                bl = brow[:, T - 1:T] if dr == 0 else brow[:, 0:1]
                mw = jnp.max(bl - brow + irow, axis=-1, keepdims=True)
                mn = jnp.maximum(bl + m, mw)
                vals[c][3 * dr:3 * dr + 3] = [m, mn, bl]
                m = mn
            m_fin.append(m)
        for c in range(nc):
            rows8 = ([jnp.broadcast_to(v, (1, LANES)) for v in vals[c]]
                     + [jnp.zeros((SUBLANES - len(vals[c]), LANES), F32)])
            sc_s[b * nc + c] = jnp.concatenate(rows8, axis=0)

    uq =_dot_nt(hb, wq_ref[...].astype(BF16))
    uk = _dot_nt(hb, wk_ref[...].astype(BF16))
    q_s[...] = _silu(_short_conv3(uq, cwq_ref, cbq_ref[...], L)).astype(BF16)
    uv = _dot_nt(hb, wv_ref[...].astype(BF16))
    k = _silu(_short_conv3(uk, cwk_ref, cbk_ref[...], L)) * (dh ** -0.5)
    kt_s[...] = k.T.astype(BF16)
    uo = _dot_nt(hb, wo_ref[...].astype(BF16))
    v_s[:, 0:dh] = uv.astype(BF16)
    v_s[:, dh:dh + LANES] = jnp.ones((R, LANES), BF16)
    uz = _dot_nt(hb, wz_ref[...].astype(BF16))
    og_s[...] = _sigmoid(uo) * _silu(uz)

    for b in range(nb):
        for dr in range(2):
            if has_state:
                c_s[2 * b + dr, :, 0:dh] = c0_ref[b, dr]
                c_s[2 * b + dr, :, dh:dh + LANES] = jnp.broadcast_to(n0_ref[b, dr], (LANES, dh)).T
            else:
                c_s[2 * b + dr] = jnp.zeros((dh, dh + LANES), F32)

    def step(ci, carry):
        for b in range(nb):
            for dr in range(2):
                c = ci if dr == 0 else nc - 1 - ci
                r0 = pl.multiple_of(b * L + c * T, T)
                qc = q_s[pl.ds(r0, T), :]
                vc = v_s[pl.ds(r0, T), :]
                ktc = kt_s[:, pl.ds(r0, T)]
                b_rep = jnp.broadcast_to(g2_s[pl.ds(r0, T), :][:, 2 * dr + 1:2 * dr + 2], (T, LANES))
                g2tc = g2t_s[:, pl.ds(r0, T)]
                irow = g2tc[2 * dr:2 * dr + 1, :]
                brow = g2tc[2 * dr + 1:2 * dr + 2, :]
                sc = sc_s[b * nc + c]
                m_prev = sc[3 * dr:3 * dr + 1, :]
                m_new = sc[3 * dr + 1:3 * dr + 2, :]
                bl = sc[3 * dr + 2:3 * dr + 3, :]
                c_old = c_s[2 * b + dr]

                dlog = jnp.where(causal if dr == 0 else anti, b_rep - brow + irow, NEG)
                inter = b_rep + m_prev
                mt = jnp.maximum(inter, jnp.max(dlog, axis=-1, keepdims=True))
                smat = _dot(qc, ktc) * jnp.exp(dlog - mt)
                iw = jnp.exp(inter - mt)
                ext = (_dot(smat.astype(BF16), vc)
                       + jnp.concatenate([iw, iw, iw], axis=-1) * _dot(qc, c_old.astype(BF16)))
                inv = 1.0 / jnp.maximum(jnp.abs(ext[:, dh:dh + LANES]), jnp.exp(-mt))
                hh = ext[:, 0:dh] * jnp.concatenate([inv, inv], axis=-1)
                hdir_s[dr, pl.ds(r0, T), :] = hh

                ws = jnp.exp(bl - brow + irow - m_new)
                dec = jnp.exp(bl + m_prev - m_new)[:, 0:1]
                kw = (ktc.astype(F32) * ws).astype(BF16)
                c_s[2 * b + dr] = dec * c_old + _dot(kw, vc)
        return carry

    lax.fori_loop(0, nc, step, 0, unroll=max(1, min(nc, ML_STREAMS // (2 * nb))))
    if not has_state:
        for b in range(nb):
            for dr in range(2):
                cn_ref[b, dr] = c_s[2 * b + dr, :, 0:dh]
                nn_ref[b, dr] = c_s[2 * b + dr, :, dh:dh + LANES].T[0:1, :]
                mn_ref[b, dr] = jnp.broadcast_to(m_fin[2 * b + dr], (1, LANES))

    hs = hdir_s[0] + hdir_s[1]
    hm = hs * lax.rsqrt(jnp.mean(hs * hs, axis=-1, keepdims=True) + EPS) * ng_ref[...]
    yb_ref[...] = (hm * og_s[...]).astype(yb_ref.dtype).reshape(yb_ref.shape)


def _mlstm(h, l, depth, w_in, wg_heads, gb_heads, conv_w, conv_b, norm_g, states, carried,
           seg_off):
    bsz, L, d = h.shape
    heads = wg_heads.shape[1]
    d_ml = norm_g.shape[-1]
    dh = d_ml // heads
    has_state = states is not None
    nb = max(1, min(bsz, ML_ROWS // L))
    nc = L // ML_CHUNK
    qo, ko, vo, oo, zo = (s // dh for s in seg_off)

    def w_spec(off):
        return pl.BlockSpec((None, dh, d), lambda hd, b, off=off: (l, off + hd, 0))

    def vec_spec(rows, off):
        return pl.BlockSpec((None, rows, dh), lambda hd, b, off=off: (l, 0, off + hd))

    in_specs = [pl.BlockSpec((nb, L, d), lambda hd, b: (b, 0, 0)),
                w_spec(qo), w_spec(ko), w_spec(vo), w_spec(oo), w_spec(zo),
                pl.BlockSpec((None, None, LANES, d), lambda hd, b: (l, hd, 0, 0)),
                pl.BlockSpec((None, None, 1, LANES), lambda hd, b: (l, hd, 0, 0)),
                vec_spec(3, 0), vec_spec(3, heads), vec_spec(1, 0), vec_spec(1, heads),
                vec_spec(1, 0)]
    args = [h, w_in, w_in, w_in, w_in, w_in, wg_heads, gb_heads,
            conv_w, conv_w, conv_b, conv_b, norm_g]
    yb_spec = pl.BlockSpec((nb, L, dh), lambda hd, b: (b, 0, hd))
    yb_shape = jax.ShapeDtypeStruct((bsz, L, d_ml), BF16)
    state_idx = lambda hd, b: (b, l, 0, hd, 0, 0)
    c_spec = pl.BlockSpec((nb, None, 2, None, dh, dh), state_idx)
    n_spec = pl.BlockSpec((nb, None, 2, None, 1, dh), state_idx)
    m_spec = pl.BlockSpec((nb, None, 2, None, 1, LANES), state_idx)
    aliases = {}
    if has_state:
        in_specs += [c_spec, n_spec, m_spec]
        args += list(states)
        out_specs, out_shape = yb_spec, yb_shape
    else:
        if carried is not None:
            aliases = {len(args) + i: 1 + i for i in range(3)}
            in_specs += [pl.BlockSpec(memory_space=pl.ANY)] * 3
            args += list(carried)
        out_specs = (yb_spec, c_spec, n_spec, m_spec)
        out_shape = (yb_shape,
                     jax.ShapeDtypeStruct((bsz, depth, 2, heads, dh, dh), F32),
                     jax.ShapeDtypeStruct((bsz, depth, 2, heads, 1, dh), F32),
                     jax.ShapeDtypeStruct((bsz, depth, 2, heads, 1, LANES), F32))
    rows = nb * L
    scratch = [pltpu.VMEM((rows, dh), BF16),
               pltpu.VMEM((dh, rows), BF16),
               pltpu.VMEM((rows, dh + LANES), BF16),
               pltpu.VMEM((rows, LANES), F32),
               pltpu.VMEM((SUBLANES, rows), F32),
               pltpu.VMEM((nb * nc, SUBLANES, LANES), F32),
               pltpu.VMEM((2, rows, dh), F32),
               pltpu.VMEM((rows, dh), F32),
               pltpu.VMEM((2 * nb, dh, dh + LANES), F32)]
    return pl.pallas_call(
        functools.partial(_mlstm_kernel, nb=nb, L=L, has_state=has_state, n_aliased=len(aliases)),
        grid=(heads, bsz // nb),
        in_specs=in_specs,
        out_specs=out_specs,
        out_shape=out_shape,
        input_output_aliases=aliases,
        scratch_shapes=scratch,
        compiler_params=_params("arbitrary", "arbitrary"),
        name="mlstm",
    )(*args)


OUT_WEIGHTS = 5
OUT_STAGES = 2


def _out_kernel(x_ref, h_ref, ya_ref, yb_ref, mod_ref, wt_hbm, wpa_hbm, wpb_hbm, wout_hbm,
                g_ref, *rest, last, l, off_ga):
    w_s, stage, sem = rest[-3:]
    rest = rest[:-3]
    d = w_s.shape[-1]
    srcs = (wt_hbm.at[l, pl.ds(off_ga, d), :], wt_hbm.at[l, pl.ds(off_ga + d, d), :],
            wpa_hbm.at[l], wpb_hbm.at[l], wout_hbm.at[l])

    def copy(i):
        slot = i % OUT_STAGES
        return pltpu.make_async_copy(srcs[i], stage.at[slot], sem.at[slot])

    def fetched(i):
        copy(i).wait()
        w_s[i] = stage[i % OUT_STAGES].astype(BF16)
        if i + OUT_STAGES < OUT_WEIGHTS:
            copy(i + OUT_STAGES).start()
        return w_s[i]

    def body(weight):
        hb = h_ref[...]
        ga = _sigmoid(_dot_nt(hb, weight(0)))
        gb = _sigmoid(_dot_nt(hb, weight(1)))
        merged = ga * _dot(ya_ref[...], weight(2)) + gb * _dot(yb_ref[...], weight(3))
        xn = x_ref[...] + mod_ref[2:3, :] * _dot(merged.astype(BF16), weight(4))
        normed = xn * lax.rsqrt(jnp.mean(xn * xn, axis=-1, keepdims=True) + EPS) * g_ref[...]
        if last:
            (y_ref,) = rest
            y_ref[...] = normed
        else:
            modn_ref, xo_ref, ho_ref = rest
            xo_ref[...] = xn
            ho_ref[...] = (normed * (1.0 + modn_ref[1:2, :]) + modn_ref[0:1, :]).astype(ho_ref.dtype)

    first = (pl.program_id(0) == 0) & (pl.program_id(1) == 0)

    @pl.when(first)
    def _():
        for i in range(OUT_STAGES):
            copy(i).start()
        body(fetched)

    @pl.when(jnp.logical_not(first))
    def _():
        body(lambda i: w_s[i])


def _out(x, h, ya, yb, mods, row0, l, w_t, off_ga, w_pa, w_pb, w_out, norm_g, final_g):
    bm, t, d = x.shape
    tm = min(t, OUT_TM)
    last = l == mods.shape[0] - 1
    tok = lambda b, i: (b, i, 0)
    tok_spec = pl.BlockSpec((None, tm, d), tok)
    hbm = pl.BlockSpec(memory_space=pl.ANY)
    g_spec = pl.BlockSpec((None, 1, d), lambda b, i: (0 if last else l + 1, 0, 0))
    in_specs = [tok_spec, tok_spec, tok_spec, tok_spec, _mod_spec(d, l, row0), hbm, hbm, hbm, hbm,
                g_spec]
    args = [x, h, ya, yb, mods, w_t, w_pa, w_pb, w_out, final_g.reshape(1, 1, d) if last else norm_g]
    if last:
        out_specs, out_shape = tok_spec, jax.ShapeDtypeStruct((bm, t, d), F32)
    else:
        in_specs.append(_mod_spec(d, l + 1, row0))
        args.append(mods)
        out_specs = (tok_spec, tok_spec)
        out_shape = (jax.ShapeDtypeStruct((bm, t, d), F32), jax.ShapeDtypeStruct((bm, t, d), BF16))
    return pl.pallas_call(
        functools.partial(_out_kernel, last=last, l=l, off_ga=off_ga),
        grid=(bm, t // tm),
        in_specs=in_specs,
        out_specs=out_specs,
        out_shape=out_shape,
        scratch_shapes=[pltpu.VMEM((OUT_WEIGHTS, d, d), BF16),
                        pltpu.VMEM((OUT_STAGES, d, d), F32),
                        pltpu.SemaphoreType.DMA((OUT_STAGES,))],
        compiler_params=_params("arbitrary", "arbitrary"),
        name="merge_out",
    )(*args)


def kernel(x_prompt, x_sample, state_C, state_n, state_m, c, c_ctx, norm_g, w_ada, b_ada, w_in, hy_conv_w, hy_conv_b, hy_w1, hy_b1, hy_w2, hy_b2, hy_w3, hy_b3, hy_freq, hy_decay, hy_bias, ml_conv_w, ml_conv_b, ml_if_b, ml_norm_g, w_pa, w_pb, w_out, final_g):
    depth, d, _ = w_in.shape
    d_hy = hy_bias.shape[-1]
    d_ml = ml_norm_g.shape[-1]
    heads = ml_if_b.shape[-1]
    bp, lp, _ = x_prompt.shape
    bs, ls, _ = x_sample.shape

    off_q = 4 * d_hy
    off_v = off_q + 2 * d_ml
    off_o = off_v + d_ml
    off_z = off_o + d_ml
    off_g = off_z + d_ml
    off_ga = off_g + 4 * heads
    seg_off = (off_q, off_q + d_ml, off_v, off_o, off_z)

    w_t = jnp.transpose(w_in, (0, 2, 1))
    wg = w_t[:, off_g:off_ga, :].reshape(depth, 4, heads, d).transpose(0, 2, 1, 3)
    wg_heads = jnp.zeros((depth, heads, LANES, d), BF16).at[:, :, :4, :].set(wg.astype(BF16))
    gb = ml_if_b.reshape(depth, 4, heads).transpose(0, 2, 1)
    gb_heads = jnp.zeros((depth, heads, 1, LANES), F32).at[:, :, 0, :4].set(gb)

    cvecs = jnp.concatenate([c_ctx[None], c, jnp.zeros((-(1 + bs) % SUBLANES, d), F32)], axis=0)
    mods = _mods(cvecs, w_ada, b_ada).reshape(depth, cvecs.shape[0], 3, d)

    norm_g3 = norm_g.reshape(depth, 1, d)
    hy_cw = hy_conv_w.reshape(depth, 3, 3, d_hy).transpose(0, 2, 1, 3)
    hy_cb = hy_conv_b.reshape(depth, 3, d_hy)
    ml_cb = ml_conv_b.reshape(depth, 1, 2 * d_ml)
    ml_ng = ml_norm_g.reshape(depth, 1, d_ml)
    filter_params = _filter_params(hy_w1, hy_b1, hy_w2, hy_b2, hy_w3, hy_b3, hy_freq, hy_decay)

    xp = x_prompt.reshape(1, bp * lp, d)
    xs = x_sample
    cached = (state_C, state_n.reshape(bs, depth, 2, heads, 1, -1),
              jnp.broadcast_to(state_m[..., None, None], (bs, depth, 2, heads, 1, LANES)))
    fin = None
    hp = _norm_mod(xp, norm_g3, mods, 0, 0)
    hs = _norm_mod(xs, norm_g3, mods, 0, 1)
    dft = {L: _dft_operands(L) for L in sorted({lp, ls})}
    spectra = _hyena_spectra(dft, filter_params)
    for l in range(depth):
        last = l == depth - 1

        def layer(x, h, row0, bsz, L, states, carried):
            hseq = h.reshape(bsz, L, d)
            ya = _hyena(hseq, l, w_t, hy_cw, hy_cb, spectra[L], hy_bias, dft[L])
            res = _mlstm(hseq, l, depth, w_t, wg_heads, gb_heads, ml_conv_w, ml_cb, ml_ng,
                         states, carried, seg_off)
            yb, fin = (res, None) if states is not None else (res[0], res[1:])
            res = _out(x, h, ya.reshape(x.shape), yb.reshape(x.shape), mods, row0, l,
                       w_t, off_ga, w_pa, w_pb, w_out, norm_g3, final_g)
            return (res, None, fin) if last else (res[0], res[1], fin)

        xp, hp, fin = layer(xp, hp, 0, bp, lp, None, fin)
        xs, hs, _ = layer(xs, hs, 1, bs, ls, cached, None)

    return (xp.reshape(bp, lp, d), xs, fin[0], fin[1][:, :, :, :, 0, :], fin[2][:, :, :, :, 0, 0])
```
